```python
import math
import jax, jax.numpy as jnp
from jax import lax
import numpy as np

D_MODEL = 1024
BATCH = 16
SEQ = 256
DEPTH = 2
DEC_BATCH = 2
DEC_SEQ = 4096
PAST_LEN = 512

GRID_W = 64
N_MIXERS = 2
N_MLSTM_LAYERS = (DEPTH + N_MIXERS - 1) // N_MIXERS
N_POOL_LAYERS = DEPTH // N_MIXERS
N_SUB = 3
MLSTM_INNER = 2 * D_MODEL
MLSTM_HEADS = 4
MLSTM_DH = MLSTM_INNER // MLSTM_HEADS
QKV_BLOCK = 4
N_QKV_BLOCKS = MLSTM_INNER // QKV_BLOCK
CONV_K = 5
CHUNK = 64
POOL_WINDOWS = (2, 4, 8, 16)
N_POOL_GROUPS = len(POOL_WINDOWS)
POOL_GC = D_MODEL // N_POOL_GROUPS
D_FF = 2816
EPS = 1e-6

kernel_name = "mlstm_pool_macaron_dit_step"


def rmsnorm(x, g):
    xf = x.astype(jnp.float32)
    y = xf * lax.rsqrt(jnp.mean(xf * xf, axis=-1, keepdims=True) + EPS)
    return (y * g.astype(jnp.float32)).astype(x.dtype)


def adaln(cond, w, b):
    m = jax.nn.silu(cond) @ w + b
    return m.reshape(cond.shape[0], 3 * N_SUB, D_MODEL)


def modulate(x, g, mod, j):
    return rmsnorm(x, g) * (1 + mod[:, None, 3 * j + 1]) + mod[:, None, 3 * j]


def swiglu(h, w_in, w_out):
    a, b = jnp.split(h @ w_in, 2, axis=-1)
    return (jax.nn.silu(a) * b) @ w_out


def mlstm_chunk_scan(q, k, v, logf, ig, C0, n0, m0):
    B, S, H, Dh = q.shape
    nc = S // CHUNK
    f32 = jnp.float32

    def to_chunks(a):
        a = a.astype(f32).reshape((B, nc, CHUNK) + a.shape[2:])
        return jnp.swapaxes(jnp.moveaxis(a, 1, 0), 2, 3)

    causal = jnp.tril(jnp.ones((CHUNK, CHUNK), dtype=bool))

    def step(carry, inp):
        C, n, m = carry
        qc, kc, vc, lf, ic = inp
        b = jnp.cumsum(lf, axis=-1)
        logD = b[..., :, None] - b[..., None, :] + ic[..., None, :]
        logD = jnp.where(causal, logD, -jnp.inf)
        inter = b + m[..., None]
        mt = jnp.maximum(inter, jnp.max(logD, axis=-1))
        s = jnp.einsum('bhtd,bhsd->bhts', qc, kc) * jnp.exp(logD - mt[..., None])
        ws = jnp.exp(inter - mt)
        num = jnp.einsum('bhts,bhsd->bhtd', s, vc) + ws[..., None] * jnp.einsum('bhvk,bhtk->bhtv', C, qc)
        den = jnp.sum(s, axis=-1) + ws * jnp.einsum('bhk,bhtk->bht', n, qc)
        h = num / jnp.maximum(jnp.abs(den), jnp.exp(-mt))[..., None]
        bL = b[..., -1]
        g = bL[..., None] - b + ic
        m_new = jnp.maximum(bL + m, jnp.max(g, axis=-1))
        w = jnp.exp(g - m_new[..., None])
        dec = jnp.exp(bL + m - m_new)
        C_new = dec[..., None, None] * C + jnp.einsum('bhsv,bhsk->bhvk', vc * w[..., None], kc)
        n_new = dec[..., None] * n + jnp.einsum('bhs,bhsk->bhk', w, kc)
        return (C_new, n_new, m_new), h

    xs = tuple(to_chunks(a) for a in (q, k, v, logf, ig))
    (C, n, m), hs = lax.scan(step, (C0.astype(f32), n0.astype(f32), m0.astype(f32)), xs)
    h = jnp.swapaxes(jnp.moveaxis(hs, 0, 1), 2, 3).reshape(B, S, H, Dh)
    return h, C, n, m


def mlstm_mixer(h, w_up, conv_w, conv_b, w_q, w_k, w_v, w_gate, b_gate, norm_g, skip, w_down, C0, n0, m0):
    B, S, _ = h.shape
    xm, z = jnp.split(h @ w_up, 2, axis=-1)
    xc = jax.nn.silu(lax.conv_general_dilated(
        xm, conv_w, (1,), [(CONV_K // 2, CONV_K // 2)],
        dimension_numbers=('NWC', 'WIO', 'NWC'), feature_group_count=MLSTM_INNER) + conv_b)

    def headwise(a, w):
        return jnp.einsum('bsnj,nij->bsni', a.reshape(B, S, N_QKV_BLOCKS, QKV_BLOCK), w).reshape(B, S, MLSTM_INNER)

    q = headwise(xc, w_q)
    k = headwise(xc, w_k) / math.sqrt(MLSTM_DH)
    v = headwise(xm, w_v)
    gates = (jnp.einsum('bsc,cdgh->bsdgh', jnp.concatenate([q, k, v], axis=-1), w_gate) + b_gate).astype(jnp.float32)
    shp = (B, S, MLSTM_HEADS, MLSTM_DH)
    qh, kh, vh = q.reshape(shp), k.reshape(shp), v.reshape(shp)
    outs, Cs, ns, ms = [], [], [], []
    for d in range(2):
        seq = (qh, kh, vh, jax.nn.log_sigmoid(gates[:, :, d, 1]), gates[:, :, d, 0])
        if d == 1:
            seq = tuple(jnp.flip(a, axis=1) for a in seq)
        hd, C, n, m = mlstm_chunk_scan(*seq, C0[:, d], n0[:, d], m0[:, d])
        if d == 1:
            hd = jnp.flip(hd, axis=1)
        outs.append(hd)
        Cs.append(C)
        ns.append(n)
        ms.append(m)
    hsum = outs[0] + outs[1]
    mu = jnp.mean(hsum, axis=-1, keepdims=True)
    var = jnp.mean(jnp.square(hsum - mu), axis=-1, keepdims=True)
    hn = ((hsum - mu) * lax.rsqrt(var + EPS)).reshape(B, S, MLSTM_INNER) * norm_g.astype(jnp.float32)
    out = (hn.astype(h.dtype) + skip * xc) * jax.nn.silu(z)
    y = out @ w_down
    return y, jnp.stack(Cs, axis=1), jnp.stack(ns, axis=1), jnp.stack(ms, axis=1)


def box_mean(x, win, axis):
    n = x.shape[axis]
    cs = jnp.cumsum(x.astype(jnp.float32), axis=axis)
    pad = [(0, 0)] * x.ndim
    pad[axis] = (1, 0)
    cs = jnp.pad(cs, pad)
    t = jnp.arange(n)
    lo = jnp.clip(t - win // 2, 0, n)
    hi = jnp.clip(t - win // 2 + win, 0, n)
    s = jnp.take(cs, hi, axis=axis) - jnp.take(cs, lo, axis=axis)
    bshape = [1] * x.ndim
    bshape[axis] = n
    cnt = (hi - lo).astype(jnp.float32).reshape(bshape)
    return (s / cnt).astype(x.dtype)


def pool_mixer(h, w_pool, scale, rows):
    B, S, _ = h.shape
    hg = h.reshape(B, S, N_POOL_GROUPS, POOL_GC)
    outs = []
    for g, win in enumerate(POOL_WINDOWS):
        xg = hg[:, :, g]
        if rows is None:
            p = box_mean(xg, win, 1)
        else:
            x2 = xg.reshape(B, rows, GRID_W, POOL_GC)
            p = box_mean(box_mean(x2, win, 1), win, 2).reshape(B, S, POOL_GC)
        outs.append(p - xg)
    d = jnp.stack(outs, axis=2)
    return jnp.einsum('bsgc,gcd->bsgd', d, w_pool).reshape(B, S, D_MODEL) * scale


def setup_inputs(seed: int = 0) -> dict:
    key = jax.random.key(seed)
    ks = jax.random.split(key, 32)
    f32 = jnp.float32
    nrm = lambda k, shp, s: jax.random.normal(k, shp, f32) * s
    NA, NP, H, Dh = N_MLSTM_LAYERS, N_POOL_LAYERS, MLSTM_HEADS, MLSTM_DH
    b_ig = nrm(ks[20], (NA, 2, H), 0.1)
    b_fg = jnp.linspace(3.0, 6.0, H, dtype=f32) + nrm(ks[21], (NA, 2, H), 0.1)
    return {
        "x_prompt": nrm(ks[0], (BATCH, SEQ, D_MODEL), 1.0),
        "x_sample": nrm(ks[1], (DEC_BATCH, DEC_SEQ, D_MODEL), 1.0),
        "state_C": nrm(ks[2], (DEC_BATCH, NA, 2, H, Dh, Dh), 0.05),
        "state_n": nrm(ks[3], (DEC_BATCH, NA, 2, H, Dh), 0.1),
        "state_m": nrm(ks[4], (DEC_BATCH, NA, 2, H), 0.5),
        "c": nrm(ks[5], (DEC_BATCH, D_MODEL), 1.0),
        "c_ctx": nrm(ks[6], (D_MODEL,), 1.0),
        "ada_w": nrm(ks[7], (DEPTH, D_MODEL, 3 * N_SUB * D_MODEL), D_MODEL ** -0.5),
        "ada_b": nrm(ks[8], (DEPTH, 3 * N_SUB * D_MODEL), 0.02),
        "norm_g": 1.0 + nrm(ks[9], (DEPTH, N_SUB, D_MODEL), 0.02),
        "ffn_w_in": nrm(ks[10], (DEPTH, 2, D_MODEL, 2 * D_FF), D_MODEL ** -0.5),
        "ffn_w_out": nrm(ks[11], (DEPTH, 2, D_FF, D_MODEL), D_FF ** -0.5),
        "mlstm_w_up": nrm(ks[12], (NA, D_MODEL, 2 * MLSTM_INNER), D_MODEL ** -0.5),
        "mlstm_conv_w": nrm(ks[13], (NA, CONV_K, 1, MLSTM_INNER), CONV_K ** -0.5),
        "mlstm_conv_b": nrm(ks[14], (NA, MLSTM_INNER), 0.02),
        "mlstm_w_q": nrm(ks[15], (NA, N_QKV_BLOCKS, QKV_BLOCK, QKV_BLOCK), QKV_BLOCK ** -0.5),
        "mlstm_w_k": nrm(ks[16], (NA, N_QKV_BLOCKS, QKV_BLOCK, QKV_BLOCK), QKV_BLOCK ** -0.5),
        "mlstm_w_v": nrm(ks[17], (NA, N_QKV_BLOCKS, QKV_BLOCK, QKV_BLOCK), QKV_BLOCK ** -0.5),
        "mlstm_w_gate": nrm(ks[18], (NA, 3 * MLSTM_INNER, 2, 2, H), (3 * MLSTM_INNER) ** -0.5),
        "mlstm_b_gate": jnp.stack([b_ig, b_fg], axis=2),
        "mlstm_norm_g": 1.0 + nrm(ks[22], (NA, MLSTM_INNER), 0.02),
        "mlstm_skip": 1.0 + nrm(ks[23], (NA, MLSTM_INNER), 0.02),
        "mlstm_w_down": nrm(ks[24], (NA, MLSTM_INNER, D_MODEL), MLSTM_INNER ** -0.5),
        "pool_w": nrm(ks[25], (NP, N_POOL_GROUPS, POOL_GC, POOL_GC), POOL_GC ** -0.5),
        "pool_scale": 1.0 + nrm(ks[26], (NP, D_MODEL), 0.02),
        "final_g": 1.0 + nrm(ks[27], (D_MODEL,), 0.02),
    }


def reference(x_prompt, x_sample, state_C, state_n, state_m, c, c_ctx, ada_w, ada_b, norm_g,
              ffn_w_in, ffn_w_out, mlstm_w_up, mlstm_conv_w, mlstm_conv_b, mlstm_w_q, mlstm_w_k,
              mlstm_w_v, mlstm_w_gate, mlstm_b_gate, mlstm_norm_g, mlstm_skip, mlstm_w_down,
              pool_w, pool_scale, final_g):
    Bp = x_prompt.shape[0]
    rows = x_sample.shape[1] // GRID_W
    xp, xs = x_prompt, x_sample
    new_C, new_n, new_m = [], [], []
    for l in range(DEPTH):
        mod_p = adaln(c_ctx[None, :], ada_w[l], ada_b[l])
        mod_s = adaln(c, ada_w[l], ada_b[l])
        xp = xp + 0.5 * mod_p[:, None, 2] * swiglu(modulate(xp, norm_g[l, 0], mod_p, 0), ffn_w_in[l, 0], ffn_w_out[l, 0])
        xs = xs + 0.5 * mod_s[:, None, 2] * swiglu(modulate(xs, norm_g[l, 0], mod_s, 0), ffn_w_in[l, 0], ffn_w_out[l, 0])
        hp = modulate(xp, norm_g[l, 1], mod_p, 1)
        hs = modulate(xs, norm_g[l, 1], mod_s, 1)
        if l % N_MIXERS == 0:
            a = l // N_MIXERS
            prm = (mlstm_w_up[a], mlstm_conv_w[a], mlstm_conv_b[a], mlstm_w_q[a], mlstm_w_k[a], mlstm_w_v[a],
                   mlstm_w_gate[a], mlstm_b_gate[a], mlstm_norm_g[a], mlstm_skip[a], mlstm_w_down[a])
            zC = jnp.zeros((Bp, 2, MLSTM_HEADS, MLSTM_DH, MLSTM_DH), jnp.float32)
            zn = jnp.zeros((Bp, 2, MLSTM_HEADS, MLSTM_DH), jnp.float32)
            zm = jnp.zeros((Bp, 2, MLSTM_HEADS), jnp.float32)
            yp, Cc, nc_, mc = mlstm_mixer(hp, *prm, zC, zn, zm)
            ys, _, _, _ = mlstm_mixer(hs, *prm, state_C[:, a], state_n[:, a], state_m[:, a])
            new_C.append(Cc)
            new_n.append(nc_)
            new_m.append(mc)
        else:
            p = l // N_MIXERS
            yp = pool_mixer(hp, pool_w[p], pool_scale[p], None)
            ys = pool_mixer(hs, pool_w[p], pool_scale[p], rows)
        xp = xp + mod_p[:, None, 5] * yp
        xs = xs + mod_s[:, None, 5] * ys
        xp = xp + 0.5 * mod_p[:, None, 8] * swiglu(modulate(xp, norm_g[l, 2], mod_p, 2), ffn_w_in[l, 1], ffn_w_out[l, 1])
        xs = xs + 0.5 * mod_s[:, None, 8] * swiglu(modulate(xs, norm_g[l, 2], mod_s, 2), ffn_w_in[l, 1], ffn_w_out[l, 1])
    y_prompt = rmsnorm(xp, final_g)
    y_sample = rmsnorm(xs, final_g)
    state_C_new = jnp.stack(new_C, axis=1)
    state_n_new = jnp.stack(new_n, axis=1)
    state_m_new = jnp.stack(new_m, axis=1)
    return (y_prompt, y_sample, state_C_new, state_n_new, state_m_new)
```

```python
import functools
import math

import numpy as np
import jax
import jax.numpy as jnp
from jax import lax
from jax.experimental import pallas as pl
from jax.experimental.pallas import tpu as pltpu

F32 = jnp.float32
BF16 = jnp.bfloat16

D = 1024
BATCH = 16
SEQ = 256
DEPTH = 2
DEC_BATCH = 2
DEC_SEQ = 4096
GRID_W = 64
N_SUB = 3
INNER = 2 * D
HEADS = 4
DH = INNER // HEADS
QKV_BLOCK = 4
CONV_K = 5
POOL_WINDOWS = (2, 4, 8, 16)
POOL_GC = D // len(POOL_WINDOWS)
D_FF = 2816
EPS = 1e-6

NP_TOK = BATCH * SEQ
NS_TOK = DEC_BATCH * DEC_SEQ
NTOK = NP_TOK + NS_TOK
GRP = 4096
assert NP_TOK == GRP and DEC_SEQ == GRP

MXU_TILE = 256
GATE_LANES = 128
CHUNK = 256
HALO = 8

FFN_TM = 512
PRE_TM = 256
OUT_TM = 512
POOL_TM = 512
VMEM_LIMIT = 56 * 1024 * 1024


def _silu(x):
    return x * jax.nn.sigmoid(x)


def _dot(a, b):
    return jnp.dot(a, b, preferred_element_type=F32)


def _dot_nt(a, b):
    return lax.dot_general(a, b, (((1,), (1,)), ((), ())), preferred_element_type=F32)


def _dot_tn(a, b):
    return lax.dot_general(a, b, (((0,), (0,)), ((), ())), preferred_element_type=F32)


def _resident(shape, index_map):
    return pl.BlockSpec(shape, index_map, pipeline_mode=pl.Buffered(1))


def _params(n_axes):
    return pltpu.CompilerParams(
        dimension_semantics=("arbitrary",) * n_axes, vmem_limit_bytes=VMEM_LIMIT)


def _modulate(x, g, mod_ref, sub):
    ms = jnp.mean(x * x, axis=-1, keepdims=True)
    y = x * lax.rsqrt(ms + EPS) * g
    return y * (1.0 + mod_ref[3 * sub + 1:3 * sub + 2, :]) + mod_ref[3 * sub:3 * sub + 1, :]


def _split_dot(a, x, terms, left=True):
    parts = []
    r = x
    for _ in range(terms):
        p = r.astype(BF16)
        parts.append(p)
        r = r - p.astype(F32)
    acc = None
    for p in reversed(parts):
        t = _dot(a, p) if left else _dot(p, a)
        acc = t if acc is None else acc + t
    return acc


def _adaln_kernel(cond_ref, w_ref, b_ref, o_ref):
    s = _silu(cond_ref[...]).astype(BF16)
    o_ref[...] = _dot(s, w_ref[...].astype(BF16)) + b_ref[...]


def _adaln(cond8, ada_w, ada_b):
    return pl.pallas_call(
        _adaln_kernel,
        grid=(DEPTH, 3 * N_SUB),
        in_specs=[
            pl.BlockSpec((8, D), lambda l, j: (0, 0)),
            pl.BlockSpec((None, D, D), lambda l, j: (l, 0, j)),
            pl.BlockSpec((None, None, 1, D), lambda l, j: (l, j, 0, 0)),
        ],
        out_specs=pl.BlockSpec((None, None, 8, D), lambda l, j: (l, j, 0, 0)),
        out_shape=jax.ShapeDtypeStruct((DEPTH, 3 * N_SUB, 8, D), F32),
        compiler_params=_params(2),
        name="adaln",
    )(cond8, ada_w, ada_b.reshape(DEPTH, 3 * N_SUB, 1, D))


def _ffn_kernel(x_ref, mod_ref, g_ref, win_ref, wout_ref, fg_ref, o_ref, *, layer, sub, final):
    x = x_ref[...]
    h = _modulate(x, g_ref[layer, sub:sub + 1, :], mod_ref, sub).astype(BF16)
    acc = jnp.zeros(x.shape, F32)
    for c in range(D_FF // MXU_TILE):
        lo = c * MXU_TILE
        a = _dot(h, win_ref[:, lo:lo + MXU_TILE])
        b = _dot(h, win_ref[:, D_FF + lo:D_FF + lo + MXU_TILE])
        u = (_silu(a) * b).astype(BF16)
        acc = acc + _dot(u, wout_ref[lo:lo + MXU_TILE, :])
    out = x + (0.5 * mod_ref[3 * sub + 2:3 * sub + 3, :]) * acc
    if final:
        ms = jnp.mean(out * out, axis=-1, keepdims=True)
        out = out * lax.rsqrt(ms + EPS) * fg_ref[...]
    o_ref[...] = out


def _ffn(x, mods, norm_g, w_in, w_out, final_g, *, layer, sub, which, final=False):
    tm = FFN_TM
    return pl.pallas_call(
        functools.partial(_ffn_kernel, layer=layer, sub=sub, final=final),
        grid=(NTOK // tm,),
        in_specs=[
            pl.BlockSpec((tm, D), lambda i: (i, 0)),
            pl.BlockSpec((None, None, 3 * N_SUB, D), lambda i: (layer, (i * tm) // GRP, 0, 0)),
            _resident((DEPTH, N_SUB, D), lambda i: (0, 0, 0)),
            _resident((None, None, D, 2 * D_FF), lambda i: (layer, which, 0, 0)),
            _resident((None, None, D_FF, D), lambda i: (layer, which, 0, 0)),
            _resident((1, D), lambda i: (0, 0)),
        ],
        out_specs=pl.BlockSpec((tm, D), lambda i: (i, 0)),
        out_shape=jax.ShapeDtypeStruct((NTOK, D), F32),
        compiler_params=_params(1),
        name=f"ffn_l{layer}_{which}",
    )(x, mods, norm_g, w_in, w_out, final_g)


def _mlstm_pre_kernel(xp_ref, xc_ref, xn_ref, mod_ref, g_ref, wup_ref, cw_ref, cb_ref, wqk_ref,
                      wv_ref, wg_ref, bg_ref, q_ref, k_ref, v_ref, xco_ref, z_ref, gate_ref,
                      xm_s, *, layer):
    tm = PRE_TM
    i = pl.program_id(0)
    xe = jnp.concatenate([xp_ref[...], xc_ref[...], xn_ref[...]], axis=0)
    h = _modulate(xe, g_ref[layer, 1:2, :], mod_ref, 1).astype(BF16)
    xm_s[...] = _dot(h, wup_ref[:, :INNER])
    z = _dot(h[HALO:HALO + tm], wup_ref[:, INNER:])
    z_ref[...] = z.astype(BF16)

    tok = i * tm + lax.broadcasted_iota(jnp.int32, (tm, 1), 0)
    seqlen = jnp.where(i < NP_TOK // tm, SEQ, DEC_SEQ)
    pos = tok & (seqlen - 1)
    acc = jnp.zeros((tm, INNER), F32) + cb_ref[...]
    for j in range(CONV_K):
        off = j - CONV_K // 2
        src = xm_s[pl.ds(HALO + off, tm), :]
        if off < 0:
            src = jnp.where(pos + off >= 0, src, 0.0)
        elif off > 0:
            src = jnp.where(pos + off < seqlen, src, 0.0)
        acc = acc + src * cw_ref[j]
    xc = _silu(acc)
    xco_ref[...] = xc.astype(BF16)

    for t in range(INNER // MXU_TILE):
        lo = t * MXU_TILE
        qk = _dot(xc[:, lo:lo + MXU_TILE].astype(BF16), wqk_ref[t])
        q_ref[:, lo:lo + MXU_TILE] = qk[:, :MXU_TILE].astype(BF16)
        k_ref[:, lo:lo + MXU_TILE] = (qk[:, MXU_TILE:] / math.sqrt(DH)).astype(BF16)
        xm = xm_s[pl.ds(HALO, tm), lo:lo + MXU_TILE].astype(BF16)
        v_ref[:, lo:lo + MXU_TILE] = _dot(xm, wv_ref[t]).astype(BF16)

    g = (_dot(q_ref[...], wg_ref[0:INNER, :]) + _dot(k_ref[...], wg_ref[INNER:2 * INNER, :])
         + _dot(v_ref[...], wg_ref[2 * INNER:, :]) + bg_ref[...])
    lane = lax.broadcasted_iota(jnp.int32, g.shape, 1)
    logsig = jnp.minimum(g, 0.0) - jnp.log1p(jnp.exp(-jnp.abs(g)))
    gate_ref[...] = jnp.where((lane & HEADS) != 0, logsig, g)


def _mlstm_pre(x, mods, norm_g, wup, conv_w, conv_b, wqk, wv, wg, bg, *, layer, a):
    tm = PRE_TM
    nb8 = NTOK // HALO
    act = jax.ShapeDtypeStruct((NTOK, INNER), BF16)
    act_spec = pl.BlockSpec((tm, INNER), lambda i: (i, 0))
    return pl.pallas_call(
        functools.partial(_mlstm_pre_kernel, layer=layer),
        grid=(NTOK // tm,),
        in_specs=[
            pl.BlockSpec((HALO, D), lambda i: (jnp.maximum(i * (tm // HALO) - 1, 0), 0)),
            pl.BlockSpec((tm, D), lambda i: (i, 0)),
            pl.BlockSpec((HALO, D), lambda i: (jnp.minimum((i + 1) * (tm // HALO), nb8 - 1), 0)),
            pl.BlockSpec((None, None, 3 * N_SUB, D), lambda i: (layer, (i * tm) // GRP, 0, 0)),
            _resident((DEPTH, N_SUB, D), lambda i: (0, 0, 0)),
            _resident((None, D, 2 * INNER), lambda i: (a, 0, 0)),
            _resident((None, CONV_K, 1, INNER), lambda i: (a, 0, 0, 0)),
            _resident((None, 1, INNER), lambda i: (a, 0, 0)),
            _resident((None, INNER // MXU_TILE, MXU_TILE, 2 * MXU_TILE), lambda i: (a, 0, 0, 0)),
            _resident((None, INNER // MXU_TILE, MXU_TILE, MXU_TILE), lambda i: (a, 0, 0, 0)),
            _resident((None, 3 * INNER, GATE_LANES), lambda i: (a, 0, 0)),
            _resident((None, 1, GATE_LANES), lambda i: (a, 0, 0)),
        ],
        out_specs=[act_spec] * 5 + [pl.BlockSpec((tm, GATE_LANES), lambda i: (i, 0))],
        out_shape=[act] * 5 + [jax.ShapeDtypeStruct((NTOK, GATE_LANES), F32)],
        scratch_shapes=[pltpu.VMEM((tm + 2 * HALO, INNER), F32)],
        compiler_params=_params(1),
        name="mlstm_pre",
    )(x, x, x, mods, norm_g, wup, conv_w, conv_b, wqk, wv, wg, bg)


def _tri(lower):
    r = lax.broadcasted_iota(jnp.int32, (CHUNK, CHUNK), 0)
    c = lax.broadcasted_iota(jnp.int32, (CHUNK, CHUNK), 1)
    return (c <= r) if lower else (c >= r)


def _pick_col(x, col):
    lane = lax.broadcasted_iota(jnp.int32, x.shape, 1)
    return jnp.sum(jnp.where(lane == col, x, 0.0), axis=1, keepdims=True)


def _pick_row(x, row):
    sub = lax.broadcasted_iota(jnp.int32, x.shape, 0)
    return jnp.sum(jnp.where(sub == row, x, 0.0), axis=0, keepdims=True)


def _gate_cumsums(gc, gt, lower):
    tri = jnp.where(_tri(lower), 1.0, 0.0).astype(BF16)
    tri_t = jnp.where(_tri(not lower), 1.0, 0.0).astype(BF16)
    return _split_dot(tri, gc, 3, left=True), _split_dot(tri_t, gt, 3, left=False)


def _mlstm_unit(q, k, v, qk, gc, gt, cum_c, cum_r, head, lower, state):
    d = 0 if lower else 1
    col_i = d * 2 * HEADS + head
    col_f = col_i + HEADS
    bc = _pick_col(cum_c, col_f)
    ic = _pick_col(gc, col_i)
    br = _pick_row(cum_r, col_f)
    ir = _pick_row(gt, col_i)
    if state is None:
        m0 = jnp.zeros((1, 1), F32)
    else:
        c0, n0, m0 = state
    log_d = jnp.where(_tri(lower), bc - br + ir, -jnp.inf)
    inter = bc + m0
    mt = jnp.maximum(inter, jnp.max(log_d, axis=-1, keepdims=True))
    s = qk * jnp.exp(log_d - mt)
    num = _dot(s.astype(BF16), v)
    den = jnp.sum(s, axis=-1, keepdims=True)
    if state is not None:
        ws = jnp.exp(inter - mt)
        num = num + ws * _dot_nt(q, c0.astype(BF16))
        den = den + ws * jnp.sum(q.astype(F32) * n0, axis=-1, keepdims=True)
    h = num / jnp.maximum(jnp.abs(den), jnp.exp(-mt))

    b_last = bc[CHUNK - 1:CHUNK, :] if lower else bc[0:1, :]
    g = b_last - bc + ic
    m_new = jnp.maximum(b_last + m0, jnp.max(g, axis=0, keepdims=True))
    w = jnp.exp(g - m_new)
    c_new = _dot_tn((v.astype(F32) * w).astype(BF16), k)
    n_new = jnp.sum(w * k.astype(F32), axis=0, keepdims=True)
    if state is not None:
        dec = jnp.exp(b_last + m0 - m_new)
        c_new = dec * c0 + c_new
        n_new = dec * n0 + n_new
    return h, c_new, n_new, m_new


def _scan_prompt_kernel(q_ref, k_ref, v_ref, gc_ref, gt_ref, hf_ref, hb_ref, c_ref, n_ref, m_ref):
    head = pl.program_id(1)
    q, k, v = q_ref[...], k_ref[...], v_ref[...]
    gc, gt = gc_ref[...], gt_ref[...]
    qk = _dot_nt(q, k)
    for d, h_ref in enumerate((hf_ref, hb_ref)):
        lower = d == 0
        cum_c, cum_r = _gate_cumsums(gc, gt, lower)
        h, c_new, n_new, m_new = _mlstm_unit(q, k, v, qk, gc, gt, cum_c, cum_r, head, lower, None)
        h_ref[...] = h.astype(BF16)
        c_ref[d] = c_new
        n_ref[d:d + 1, :] = n_new
        m_ref[d:d + 1, :] = jnp.broadcast_to(m_new, (1, GATE_LANES))


def _scan_prompt(q, k, v, gates, gates_t):
    act_spec = pl.BlockSpec((CHUNK, DH), lambda b, h: (b, h))
    hshape = jax.ShapeDtypeStruct((NTOK, INNER), BF16)
    return pl.pallas_call(
        _scan_prompt_kernel,
        grid=(BATCH, HEADS),
        in_specs=[act_spec, act_spec, act_spec,
                  pl.BlockSpec((CHUNK, GATE_LANES), lambda b, h: (b, 0)),
                  pl.BlockSpec((4 * HEADS, CHUNK), lambda b, h: (0, b))],
        out_specs=[act_spec, act_spec,
                   pl.BlockSpec((None, None, 2, None, DH, DH), lambda b, h: (b, 0, 0, h, 0, 0)),
                   pl.BlockSpec((None, None, 2, DH), lambda b, h: (b, h, 0, 0)),
                   pl.BlockSpec((None, None, 2, GATE_LANES), lambda b, h: (b, h, 0, 0))],
        out_shape=[hshape, hshape,
                   jax.ShapeDtypeStruct((BATCH, 1, 2, HEADS, DH, DH), F32),
                   jax.ShapeDtypeStruct((BATCH, HEADS, 2, DH), F32),
                   jax.ShapeDtypeStruct((BATCH, HEADS, 2, GATE_LANES), F32)],
        compiler_params=_params(2),
        name="scan_prompt",
    )(q, k, v, gates, gates_t)


def _scan_sample_kernel(qf_ref, kf_ref, vf_ref, gcf_ref, gtf_ref, qb_ref, kb_ref, vb_ref, gcb_ref,
                        gtb_ref, c0_ref, n0_ref, m0_ref, hf_in, hb_in, hf_ref, hb_ref,
                        c_s, n_s, m_s):
    del hf_in, hb_in
    head = pl.program_id(1)

    @pl.when(pl.program_id(2) == 0)
    def _():
        c_s[...] = c0_ref[...]
        n_s[...] = n0_ref[...]
        m_s[...] = m0_ref[...]

    units = ((qf_ref, kf_ref, vf_ref, gcf_ref, gtf_ref, hf_ref), (qb_ref, kb_ref, vb_ref, gcb_ref, gtb_ref, hb_ref))
    for d, (q_ref, k_ref, v_ref, gc_ref, gt_ref, h_ref) in enumerate(units):
        lower = d == 0
        q, k, v = q_ref[...], k_ref[...], v_ref[...]
        gc, gt = gc_ref[...], gt_ref[...]
        qk = _dot_nt(q, k)
        cum_c, cum_r = _gate_cumsums(gc, gt, lower)
        state = (c_s[d], n_s[d:d + 1, :], m_s[d:d + 1, 0:1])
        h, c_new, n_new, m_new = _mlstm_unit(q, k, v, qk, gc, gt, cum_c, cum_r, head, lower, state)
        h_ref[...] = h.astype(BF16)
        c_s[d] = c_new
        n_s[d:d + 1, :] = n_new
        m_s[d:d + 1, :] = jnp.broadcast_to(m_new, (1, GATE_LANES))


def _scan_sample(q, k, v, gates, gates_t, c0, n0, m0, hf, hb, *, a):
    nc = DEC_SEQ // CHUNK
    base = NP_TOK // CHUNK

    def fwd(b, h, c):
        return base + b * nc + c

    def bwd(b, h, c):
        return base + b * nc + (nc - 1 - c)

    def specs(blk):
        act = pl.BlockSpec((CHUNK, DH), lambda b, h, c: (blk(b, h, c), h))
        return [act, act, act,
                pl.BlockSpec((CHUNK, GATE_LANES), lambda b, h, c: (blk(b, h, c), 0)),
                pl.BlockSpec((4 * HEADS, CHUNK), lambda b, h, c: (0, blk(b, h, c)))]

    any_spec = pl.BlockSpec(memory_space=pl.ANY)
    return pl.pallas_call(
        _scan_sample_kernel,
        grid=(DEC_BATCH, HEADS, nc),
        in_specs=specs(fwd) + specs(bwd) + [
            pl.BlockSpec((None, None, 2, None, DH, DH), lambda b, h, c: (b, a, 0, h, 0, 0)),
            pl.BlockSpec((None, None, 2, DH), lambda b, h, c: (b, h, 0, 0)),
            pl.BlockSpec((None, None, 2, GATE_LANES), lambda b, h, c: (b, h, 0, 0)),
            any_spec, any_spec],
        out_specs=[pl.BlockSpec((CHUNK, DH), lambda b, h, c: (fwd(b, h, c), h)),
                   pl.BlockSpec((CHUNK, DH), lambda b, h, c: (bwd(b, h, c), h))],
        out_shape=[jax.ShapeDtypeStruct((NTOK, INNER), BF16)] * 2,
        scratch_shapes=[pltpu.VMEM((2, DH, DH), F32), pltpu.VMEM((2, DH), F32),
                        pltpu.VMEM((2, GATE_LANES), F32)],
        input_output_aliases={13: 0, 14: 1},
        compiler_params=_params(3),
        name="scan_sample",
    )(q, k, v, gates, gates_t, q, k, v, gates, gates_t, c0, n0, m0, hf, hb)


def _mlstm_out_kernel(x_ref, hf_ref, hb_ref, xc_ref, z_ref, mod_ref, ng_ref, skip_ref, wd_ref, o_ref):
    hs = hf_ref[...].astype(F32) + hb_ref[...].astype(F32)
    parts = []
    for h in range(HEADS):
        seg = hs[:, h * DH:(h + 1) * DH]
        mu = jnp.mean(seg, axis=-1, keepdims=True)
        cen = seg - mu
        var = jnp.mean(cen * cen, axis=-1, keepdims=True)
        parts.append(cen * lax.rsqrt(var + EPS))
    hn = jnp.concatenate(parts, axis=-1) * ng_ref[...]
    out = (hn + skip_ref[...] * xc_ref[...].astype(F32)) * _silu(z_ref[...].astype(F32))
    y = _dot(out.astype(BF16), wd_ref[...])
    o_ref[...] = x_ref[...] + mod_ref[5:6, :] * y


def _mlstm_out(x, hf, hb, xc, z, mods, norm_g, skip, w_down, *, layer, a):
    tm = OUT_TM
    act_spec = pl.BlockSpec((tm, INNER), lambda i: (i, 0))
    return pl.pallas_call(
        _mlstm_out_kernel,
        grid=(NTOK // tm,),
        in_specs=[
            pl.BlockSpec((tm, D), lambda i: (i, 0)),
            act_spec, act_spec, act_spec, act_spec,
            pl.BlockSpec((None, None, 3 * N_SUB, D), lambda i: (layer, (i * tm) // GRP, 0, 0)),
            _resident((None, 1, INNER), lambda i: (a, 0, 0)),
            _resident((None, 1, INNER), lambda i: (a, 0, 0)),
            _resident((None, INNER, D), lambda i: (a, 0, 0)),
        ],
        out_specs=pl.BlockSpec((tm, D), lambda i: (i, 0)),
        out_shape=jax.ShapeDtypeStruct((NTOK, D), F32),
        compiler_params=_params(1),
        name="mlstm_out",
    )(x, hf, hb, xc, z, mods, norm_g, skip, w_down)


def _window_count(idx, win, n):
    lo = jnp.maximum(idx - win // 2, 0)
    hi = jnp.minimum(idx - win // 2 + win, n)
    return (hi - lo).astype(F32)


def _pool_finish(x_ref, o_ref, h_g, p, g, mod_ref, wp_ref, ps_ref, rows):
    cols = slice(g * POOL_GC, (g + 1) * POOL_GC)
    y = _dot((p - h_g).astype(BF16), wp_ref[g]) * ps_ref[:, cols]
    o_ref[rows, cols] = x_ref[rows, cols] + mod_ref[5:6, cols] * y


def _pool_prompt_kernel(x_ref, mod_ref, g_ref, a_ref, wp_ref, ps_ref, o_ref, *, layer):
    h = _modulate(x_ref[...], g_ref[layer, 1:2, :], mod_ref, 1)
    t = lax.broadcasted_iota(jnp.int32, (SEQ, 1), 0)
    for g, win in enumerate(POOL_WINDOWS):
        cnt = _window_count(t, win, SEQ)
        for s in range(POOL_TM // SEQ):
            rows = slice(s * SEQ, (s + 1) * SEQ)
            h_g = h[rows, g * POOL_GC:(g + 1) * POOL_GC]
            p = _split_dot(a_ref[g], h_g, 2) / cnt
            _pool_finish(x_ref, o_ref, h_g, p, g, mod_ref, wp_ref, ps_ref, rows)


def _pool_sample_kernel(xp_ref, xc_ref, xn_ref, mod_ref, g_ref, a_ref, wp_ref, ps_ref, xin, o_ref,
                        hext, *, layer):
    del xin
    tm = POOL_TM
    rows_per_tile = tm // GRID_W
    j = pl.program_id(0) % (DEC_SEQ // tm)
    gain = g_ref[layer, 1:2, :]
    hext[0:tm, :] = jnp.where(j > 0, _modulate(xp_ref[...], gain, mod_ref, 1), 0.0)
    hext[tm:2 * tm, :] = _modulate(xc_ref[...], gain, mod_ref, 1)
    hext[2 * tm:3 * tm, :] = jnp.where(j < DEC_SEQ // tm - 1, _modulate(xn_ref[...], gain, mod_ref, 1), 0.0)
    t = lax.broadcasted_iota(jnp.int32, (tm, 1), 0)
    row = j * rows_per_tile + (t >> int(math.log2(GRID_W)))
    col = t & (GRID_W - 1)
    all_rows = slice(0, tm)
    for g, win in enumerate(POOL_WINDOWS):
        cols = slice(g * POOL_GC, (g + 1) * POOL_GC)
        acc = None
        for dr in range(-(win // 2), win - win // 2):
            term = hext[pl.ds(tm + GRID_W * dr, tm), cols]
            acc = term if acc is None else acc + term
        p_rows = acc / _window_count(row, win, DEC_SEQ // GRID_W)
        p = _split_dot(a_ref[g], p_rows, 2) / _window_count(col, win, GRID_W)
        _pool_finish(xc_ref, o_ref, hext[tm:2 * tm, cols], p, g, mod_ref, wp_ref, ps_ref, all_rows)


def _band(n, win):
    t = np.arange(n)
    lo = np.clip(t - win // 2, 0, n)
    hi = np.clip(t - win // 2 + win, 0, n)
    u = np.arange(n)
    return ((u[None, :] >= lo[:, None]) & (u[None, :] < hi[:, None])).astype(np.float32)


def _pool(x, mods, norm_g, pool_w, pool_scale, *, layer, p):
    tm = POOL_TM
    a1 = jnp.asarray(np.stack([_band(SEQ, w) for w in POOL_WINDOWS]), BF16)
    a2 = jnp.asarray(np.stack([np.kron(np.eye(tm // GRID_W, dtype=np.float32), _band(GRID_W, w))
                               for w in POOL_WINDOWS]), BF16)
    ngroups = len(POOL_WINDOWS)
    common = [
        _resident((DEPTH, N_SUB, D), lambda i: (0, 0, 0)),
    ]
    wp_spec = _resident((None, ngroups, POOL_GC, POOL_GC), lambda i: (p, 0, 0, 0))
    ps_spec = _resident((None, 1, D), lambda i: (p, 0, 0))
    out = pl.pallas_call(
        functools.partial(_pool_prompt_kernel, layer=layer),
        grid=(NP_TOK // tm,),
        in_specs=[pl.BlockSpec((tm, D), lambda i: (i, 0)),
                  pl.BlockSpec((None, None, 3 * N_SUB, D), lambda i: (layer, 0, 0, 0))] + common + [
                  _resident((ngroups, SEQ, SEQ), lambda i: (0, 0, 0)), wp_spec, ps_spec],
        out_specs=pl.BlockSpec((tm, D), lambda i: (i, 0)),
        out_shape=jax.ShapeDtypeStruct((NTOK, D), F32),
        compiler_params=_params(1),
        name="pool_prompt",
    )(x, mods, norm_g, a1, pool_w, pool_scale)

    base = NP_TOK // tm
    per_seq = DEC_SEQ // tm
    last = NTOK // tm - 1
    return pl.pallas_call(
        functools.partial(_pool_sample_kernel, layer=layer),
        grid=(NS_TOK // tm,),
        in_specs=[pl.BlockSpec((tm, D), lambda i: (jnp.maximum(base + i - 1, base), 0)),
                  pl.BlockSpec((tm, D), lambda i: (base + i, 0)),
                  pl.BlockSpec((tm, D), lambda i: (jnp.minimum(base + i + 1, last), 0)),
                  pl.BlockSpec((None, None, 3 * N_SUB, D), lambda i: (layer, 1 + i // per_seq, 0, 0))]
                 + common + [_resident((ngroups, tm, tm), lambda i: (0, 0, 0)), wp_spec, ps_spec,
                             pl.BlockSpec(memory_space=pl.ANY)],
        out_specs=pl.BlockSpec((tm, D), lambda i: (base + i, 0)),
        out_shape=jax.ShapeDtypeStruct((NTOK, D), F32),
        scratch_shapes=[pltpu.VMEM((3 * tm, D), F32)],
        input_output_aliases={8: 0},
        compiler_params=_params(1),
        name="pool_sample",
    )(x, x, x, mods, norm_g, a2, pool_w, pool_scale, out)


def _block_diag_tiles(w):
    na = w.shape[0]
    per = MXU_TILE // QKV_BLOCK
    wt = w.reshape(na, INNER // MXU_TILE, per, QKV_BLOCK, QKV_BLOCK)
    eye = jnp.eye(per, dtype=w.dtype)
    dense = jnp.einsum('atnij,nm->atnjmi', wt, eye)
    return dense.reshape(na, INNER // MXU_TILE, MXU_TILE, MXU_TILE)


def kernel(x_prompt, x_sample, state_C, state_n, state_m, c, c_ctx, ada_w, ada_b, norm_g, ffn_w_in,
           ffn_w_out, mlstm_w_up, mlstm_conv_w, mlstm_conv_b, mlstm_w_q, mlstm_w_k, mlstm_w_v,
           mlstm_w_gate, mlstm_b_gate, mlstm_norm_g, mlstm_skip, mlstm_w_down, pool_w, pool_scale,
           final_g):
    na = mlstm_w_up.shape[0]
    x = jnp.concatenate([x_prompt.reshape(NP_TOK, D), x_sample.reshape(NS_TOK, D)], axis=0)
    cond8 = jnp.concatenate([c_ctx[None, :], c, jnp.zeros((8 - 1 - DEC_BATCH, D), F32)], axis=0)
    mods = _adaln(cond8, ada_w, ada_b)
    mods = jnp.transpose(mods[:, :, :1 + DEC_BATCH], (0, 2, 1, 3))

    w_in = ffn_w_in.astype(BF16)
    w_out = ffn_w_out.astype(BF16)
    fg = final_g.reshape(1, D)
    wup = mlstm_w_up.astype(BF16)
    wqk = jnp.concatenate([_block_diag_tiles(mlstm_w_q), _block_diag_tiles(mlstm_w_k)], axis=-1).astype(BF16)
    wv = _block_diag_tiles(mlstm_w_v).astype(BF16)
    ngate = 4 * HEADS
    wg = jnp.pad(mlstm_w_gate.reshape(na, 3 * INNER, ngate), ((0, 0), (0, 0), (0, GATE_LANES - ngate))).astype(BF16)
    bg = jnp.pad(mlstm_b_gate.reshape(na, 1, ngate), ((0, 0), (0, 0), (0, GATE_LANES - ngate)))
    conv_b = mlstm_conv_b.reshape(na, 1, INNER)
    mng = mlstm_norm_g.reshape(na, 1, INNER)
    skip = mlstm_skip.reshape(na, 1, INNER)
    wdown = mlstm_w_down.astype(BF16)
    pw = pool_w.astype(BF16)
    ps = pool_scale.reshape(-1, 1, D)

    new_c, new_n, new_m = [], [], []
    for l in range(DEPTH):
        x = _ffn(x, mods, norm_g, w_in, w_out, fg, layer=l, sub=0, which=0)
        if l % 2 == 0:
            a = l // 2
            q, k, v, xc, z, gates = _mlstm_pre(x, mods, norm_g, wup, mlstm_conv_w, conv_b, wqk, wv, wg, bg,
                                               layer=l, a=a)
            gates_t = jnp.transpose(gates[:, :ngate])
            hf, hb, c_new, n_new, m_new = _scan_prompt(q, k, v, gates, gates_t)
            n0 = jnp.transpose(state_n[:, a], (0, 2, 1, 3))
            m0 = jnp.broadcast_to(jnp.transpose(state_m[:, a], (0, 2, 1))[..., None],
                                  (DEC_BATCH, HEADS, 2, GATE_LANES))
            hf, hb = _scan_sample(q, k, v, gates, gates_t, state_C, n0, m0, hf, hb, a=a)
            x = _mlstm_out(x, hf, hb, xc, z, mods, mng, skip, wdown, layer=l, a=a)
            new_c.append(c_new)
            new_n.append(jnp.transpose(n_new, (0, 2, 1, 3))[:, None])
            new_m.append(jnp.transpose(m_new[..., 0], (0, 2, 1))[:, None])
        else:
            x = _pool(x, mods, norm_g, pw, ps, layer=l, p=l // 2)
        x = _ffn(x, mods, norm_g, w_in, w_out, fg, layer=l, sub=2, which=1, final=(l == DEPTH - 1))
    y_prompt = x[:NP_TOK].reshape(BATCH, SEQ, D)
    y_sample = x[NP_TOK:].reshape(DEC_BATCH, DEC_SEQ, D)
    return (y_prompt, y_sample, jnp.concatenate(new_c, axis=1), jnp.concatenate(new_n, axis=1),
            jnp.concatenate(new_m, axis=1))
```

```python
import functools
import math

import numpy as np
import jax
import jax.numpy as jnp
from jax import lax
from jax.experimental import pallas as pl
from jax.experimental.pallas import tpu as pltpu

F32 = jnp.float32
BF16 = jnp.bfloat16

D = 1024
BATCH = 16
SEQ = 256
DEPTH = 2
DEC_BATCH = 2
DEC_SEQ = 4096
GRID_W = 64
N_SUB = 3
INNER = 2 * D
HEADS = 4
DH = INNER // HEADS
QKV_BLOCK = 4
CONV_K = 5
POOL_WINDOWS = (2, 4, 8, 16)
POOL_GC = D // len(POOL_WINDOWS)
D_FF = 2816
EPS = 1e-6

NP_TOK = BATCH * SEQ
NS_TOK = DEC_BATCH * DEC_SEQ
NTOK = NP_TOK + NS_TOK
GRP = 4096
assert NP_TOK == GRP and DEC_SEQ == GRP

MXU_TILE = 256
GATE_LANES = 128
CHUNK = 256
HALO = 8

NGATE = 2 * 2 * HEADS

FFN_TM = 512
PRE_TM = CHUNK
OUT_TM = 512
POOL_TM = 512
VMEM_LIMIT = 56 * 1024 * 1024


def _silu(x):
    return x * jax.nn.sigmoid(x)


def _dot(a, b):
    return jnp.dot(a, b, preferred_element_type=F32)


def _dot_nt(a, b):
    return lax.dot_general(a, b, (((1,), (1,)), ((), ())), preferred_element_type=F32)


def _dot_tn(a, b):
    return lax.dot_general(a, b, (((0,), (0,)), ((), ())), preferred_element_type=F32)


def _resident(shape, index_map):
    return pl.BlockSpec(shape, index_map, pipeline_mode=pl.Buffered(1))


def _params(n_axes):
    return pltpu.CompilerParams(
        dimension_semantics=("arbitrary",) * n_axes, vmem_limit_bytes=VMEM_LIMIT)


def _modulate(x, g, mod_ref, sub):
    ms = jnp.mean(x * x, axis=-1, keepdims=True)
    y = x * lax.rsqrt(ms + EPS) * g
    return y * (1.0 + mod_ref[3 * sub + 1:3 * sub + 2, :]) + mod_ref[3 * sub:3 * sub + 1, :]


def _split_dot(a, x, terms, left=True):
    parts = []
    r = x
    for _ in range(terms):
        p = r.astype(BF16)
        parts.append(p)
        r = r - p.astype(F32)
    acc = None
    for p in reversed(parts):
        t = _dot(a, p) if left else _dot(p, a)
        acc = t if acc is None else acc + t
    return acc


def _adaln_kernel(cond_ref, w_ref, b_ref, o_ref):
    s = _silu(cond_ref[...]).astype(BF16)
    o_ref[...] = _dot(s, w_ref[...].astype(BF16)) + b_ref[...]


def _adaln(cond8, ada_w, ada_b):
    return pl.pallas_call(
        _adaln_kernel,
        grid=(DEPTH, 3 * N_SUB),
        in_specs=[
            pl.BlockSpec((8, D), lambda l, j: (0, 0)),
            pl.BlockSpec((None, D, D), lambda l, j: (l, 0, j)),
            pl.BlockSpec((None, None, 1, D), lambda l, j: (l, j, 0, 0)),
        ],
        out_specs=pl.BlockSpec((None, None, 8, D), lambda l, j: (l, j, 0, 0)),
        out_shape=jax.ShapeDtypeStruct((DEPTH, 3 * N_SUB, 8, D), F32),
        compiler_params=_params(2),
        name="adaln",
    )(cond8, ada_w, ada_b.reshape(DEPTH, 3 * N_SUB, 1, D))


def _ffn_kernel(*refs, layer, sub, split_in, final):
    n_x = 2 if split_in else 1
    x_refs, (mod_ref, g_ref, win_ref, wout_ref, fg_ref), o_refs = refs[:n_x], refs[n_x:n_x + 5], refs[n_x + 5:]
    is_prompt = pl.program_id(0) < NP_TOK // FFN_TM
    x = jnp.where(is_prompt, x_refs[0][...], x_refs[1][...]) if split_in else x_refs[0][...]
    h = _modulate(x, g_ref[layer, sub:sub + 1, :], mod_ref, sub).astype(BF16)
    acc = jnp.zeros(x.shape, F32)
    for c in range(D_FF // MXU_TILE):
        lo = c * MXU_TILE
        a = _dot(h, win_ref[:, lo:lo + MXU_TILE])
        b = _dot(h, win_ref[:, D_FF + lo:D_FF + lo + MXU_TILE])
        u = (_silu(a) * b).astype(BF16)
        acc = acc + _dot(u, wout_ref[lo:lo + MXU_TILE, :])
    out = x + (0.5 * mod_ref[3 * sub + 2:3 * sub + 3, :]) * acc
    if not final:
        o_refs[0][...] = out
        return
    ms = jnp.mean(out * out, axis=-1, keepdims=True)
    out = out * lax.rsqrt(ms + EPS) * fg_ref[...]

    @pl.when(is_prompt)
    def _():
        o_refs[0][...] = out

    @pl.when(jnp.logical_not(is_prompt))
    def _():
        o_refs[1][...] = out


def _ffn(xs, mods, norm_g, w_in, w_out, final_g, *, layer, sub, which, final=False):
    tm = FFN_TM
    npt = NP_TOK // tm
    prompt_spec = pl.BlockSpec((tm, D), lambda i: (jnp.minimum(i, npt - 1), 0))
    sample_spec = pl.BlockSpec((tm, D), lambda i: (jnp.maximum(i - npt, 0), 0))
    whole_spec = pl.BlockSpec((tm, D), lambda i: (i, 0))
    split_in = len(xs) == 2
    if final:
        out_specs = [prompt_spec, sample_spec]
        out_shape = [jax.ShapeDtypeStruct((NP_TOK, D), F32), jax.ShapeDtypeStruct((NS_TOK, D), F32)]
    else:
        out_specs = whole_spec
        out_shape = jax.ShapeDtypeStruct((NTOK, D), F32)
    return pl.pallas_call(
        functools.partial(_ffn_kernel, layer=layer, sub=sub, split_in=split_in, final=final),
        grid=(NTOK // tm,),
        in_specs=([prompt_spec, sample_spec] if split_in else [whole_spec]) + [
            pl.BlockSpec((None, None, 3 * N_SUB, D), lambda i: (layer, (i * tm) // GRP, 0, 0)),
            _resident((DEPTH, N_SUB, D), lambda i: (0, 0, 0)),
            _resident((None, None, D, 2 * D_FF), lambda i: (layer, which, 0, 0)),
            _resident((None, None, D_FF, D), lambda i: (layer, which, 0, 0)),
            _resident((1, D), lambda i: (0, 0)),
        ],
        out_specs=out_specs,
        out_shape=out_shape,
        compiler_params=_params(1),
        name=f"ffn_l{layer}_{which}",
    )(*xs, mods, norm_g, w_in, w_out, final_g)


def _mlstm_pre_kernel(xp_ref, xc_ref, xn_ref, mod_ref, g_ref, wup_ref, cw_ref, cb_ref, wqk_ref,
                      wv_ref, wg_ref, bg_ref, q_ref, k_ref, v_ref, xco_ref, z_ref, gate_ref,
                      xm_s, *, layer):
    tm = PRE_TM
    i = pl.program_id(0)
    xe = jnp.concatenate([xp_ref[...], xc_ref[...], xn_ref[...]], axis=0)
    h = _modulate(xe, g_ref[layer, 1:2, :], mod_ref, 1).astype(BF16)
    xm_s[...] = _dot(h, wup_ref[:, :INNER])
    z = _dot(h[HALO:HALO + tm], wup_ref[:, INNER:])
    z_ref[...] = z.astype(BF16)

    tok = i * tm + lax.broadcasted_iota(jnp.int32, (tm, 1), 0)
    seqlen = jnp.where(i < NP_TOK // tm, SEQ, DEC_SEQ)
    pos = tok & (seqlen - 1)
    acc = jnp.zeros((tm, INNER), F32) + cb_ref[...]
    for j in range(CONV_K):
        off = j - CONV_K // 2
        src = xm_s[pl.ds(HALO + off, tm), :]
        if off < 0:
            src = jnp.where(pos + off >= 0, src, 0.0)
        elif off > 0:
            src = jnp.where(pos + off < seqlen, src, 0.0)
        acc = acc + src * cw_ref[j]
    xc = _silu(acc)
    xco_ref[...] = xc.astype(BF16)

    for t in range(INNER // MXU_TILE):
        lo = t * MXU_TILE
        qk = _dot(xc[:, lo:lo + MXU_TILE].astype(BF16), wqk_ref[t])
        q_ref[:, lo:lo + MXU_TILE] = qk[:, :MXU_TILE].astype(BF16)
        k_ref[:, lo:lo + MXU_TILE] = (qk[:, MXU_TILE:] / math.sqrt(DH)).astype(BF16)
        xm = xm_s[pl.ds(HALO, tm), lo:lo + MXU_TILE].astype(BF16)
        v_ref[:, lo:lo + MXU_TILE] = _dot(xm, wv_ref[t]).astype(BF16)

    g = (_dot(q_ref[...], wg_ref[0:INNER, :]) + _dot(k_ref[...], wg_ref[INNER:2 * INNER, :])
         + _dot(v_ref[...], wg_ref[2 * INNER:, :]) + bg_ref[...])
    lane = lax.broadcasted_iota(jnp.int32, g.shape, 1)
    logsig = jnp.minimum(g, 0.0) - jnp.log1p(jnp.exp(-jnp.abs(g)))
    gl = jnp.where(lane < 2 * NGATE, jnp.where((lane & HEADS) != 0, logsig, g), 0.0)
    prefix = _split_dot(jnp.where(_tri(True), 1.0, 0.0).astype(BF16), gl, 3)
    suffix = _split_dot(jnp.where(_tri(False), 1.0, 0.0).astype(BF16), gl, 3)
    cum = jnp.where((lane & (2 * HEADS)) == 0, prefix, suffix)
    gate_ref[...] = jnp.where(lane < NGATE, gl, cum)


def _mlstm_pre(x, mods, norm_g, wup, conv_w, conv_b, wqk, wv, wg, bg, *, layer, a):
    tm = PRE_TM
    nb8 = NTOK // HALO
    act = jax.ShapeDtypeStruct((NTOK, INNER), BF16)
    act_spec = pl.BlockSpec((tm, INNER), lambda i: (i, 0))
    return pl.pallas_call(
        functools.partial(_mlstm_pre_kernel, layer=layer),
        grid=(NTOK // tm,),
        in_specs=[
            pl.BlockSpec((HALO, D), lambda i: (jnp.maximum(i * (tm // HALO) - 1, 0), 0)),
            pl.BlockSpec((tm, D), lambda i: (i, 0)),
            pl.BlockSpec((HALO, D), lambda i: (jnp.minimum((i + 1) * (tm // HALO), nb8 - 1), 0)),
            pl.BlockSpec((None, None, 3 * N_SUB, D), lambda i: (layer, (i * tm) // GRP, 0, 0)),
            _resident((DEPTH, N_SUB, D), lambda i: (0, 0, 0)),
            _resident((None, D, 2 * INNER), lambda i: (a, 0, 0)),
            _resident((None, CONV_K, 1, INNER), lambda i: (a, 0, 0, 0)),
            _resident((None, 1, INNER), lambda i: (a, 0, 0)),
            _resident((None, INNER // MXU_TILE, MXU_TILE, 2 * MXU_TILE), lambda i: (a, 0, 0, 0)),
            _resident((None, INNER // MXU_TILE, MXU_TILE, MXU_TILE), lambda i: (a, 0, 0, 0)),
            _resident((None, 3 * INNER, GATE_LANES), lambda i: (a, 0, 0)),
            _resident((None, 1, GATE_LANES), lambda i: (a, 0, 0)),
        ],
        out_specs=[act_spec] * 5 + [pl.BlockSpec((tm, GATE_LANES), lambda i: (i, 0))],
        out_shape=[act] * 5 + [jax.ShapeDtypeStruct((NTOK, GATE_LANES), F32)],
        scratch_shapes=[pltpu.VMEM((tm + 2 * HALO, INNER), F32)],
        compiler_params=_params(1),
        name="mlstm_pre",
    )(x, x, x, mods, norm_g, wup, conv_w, conv_b, wqk, wv, wg, bg)


def _tri(lower):
    r = lax.broadcasted_iota(jnp.int32, (CHUNK, CHUNK), 0)
    c = lax.broadcasted_iota(jnp.int32, (CHUNK, CHUNK), 1)
    return (c <= r) if lower else (c >= r)


def _pick_col(x, col):
    lane = lax.broadcasted_iota(jnp.int32, x.shape, 1)
    return jnp.sum(jnp.where(lane == col, x, 0.0), axis=1, keepdims=True)


def _pick_row(x, row):
    sub = lax.broadcasted_iota(jnp.int32, x.shape, 0)
    return jnp.sum(jnp.where(sub == row, x, 0.0), axis=0, keepdims=True)


def _mlstm_unit(q, k, v, qk, gc, gt, head, lower, state):
    d = 0 if lower else 1
    col_i = d * 2 * HEADS + head
    col_f = NGATE + col_i + HEADS
    bc = _pick_col(gc, col_f)
    ic = _pick_col(gc, col_i)
    br = _pick_row(gt, col_f)
    ir = _pick_row(gt, col_i)
    if state is None:
        m0 = jnp.zeros((1, 1), F32)
    else:
        c0, n0, m0 = state
    log_d = jnp.where(_tri(lower), bc - br + ir, -jnp.inf)
    inter = bc + m0
    mt = jnp.maximum(inter, jnp.max(log_d, axis=-1, keepdims=True))
    s = qk * jnp.exp(log_d - mt)
    num = _dot(s.astype(BF16), v)
    den = jnp.sum(s, axis=-1, keepdims=True)
    if state is not None:
        ws = jnp.exp(inter - mt)
        num = num + ws * _dot_nt(q, c0.astype(BF16))
        den = den + ws * jnp.sum(q.astype(F32) * n0, axis=-1, keepdims=True)
    h = num / jnp.maximum(jnp.abs(den), jnp.exp(-mt))

    b_last = bc[CHUNK - 1:CHUNK, :] if lower else bc[0:1, :]
    g = b_last - bc + ic
    m_new = jnp.maximum(b_last + m0, jnp.max(g, axis=0, keepdims=True))
    w = jnp.exp(g - m_new)
    c_new = _dot_tn((v.astype(F32) * w).astype(BF16), k)
    n_new = jnp.sum(w * k.astype(F32), axis=0, keepdims=True)
    if state is not None:
        dec = jnp.exp(b_last + m0 - m_new)
        c_new = dec * c0 + c_new
        n_new = dec * n0 + n_new
    return h, c_new, n_new, m_new


def _scan_prompt_kernel(q_ref, k_ref, v_ref, gc_ref, gt_ref, hf_ref, hb_ref, c_ref, n_ref, m_ref):
    head = pl.program_id(1)
    q, k, v = q_ref[...], k_ref[...], v_ref[...]
    gc, gt = gc_ref[...], gt_ref[...]
    qk = _dot_nt(q, k)
    for d, h_ref in enumerate((hf_ref, hb_ref)):
        h, c_new, n_new, m_new = _mlstm_unit(q, k, v, qk, gc, gt, head, d == 0, None)
        h_ref[...] = h.astype(BF16)
        c_ref[d] = c_new
        n_ref[d:d + 1, :] = n_new
        m_ref[d:d + 1, :] = jnp.broadcast_to(m_new, (1, GATE_LANES))


def _scan_prompt(q, k, v, gates, gates_t):
    act_spec = pl.BlockSpec((CHUNK, DH), lambda b, h: (b, h))
    hshape = jax.ShapeDtypeStruct((NTOK, INNER), BF16)
    return pl.pallas_call(
        _scan_prompt_kernel,
        grid=(BATCH, HEADS),
        in_specs=[act_spec, act_spec, act_spec,
                  pl.BlockSpec((CHUNK, GATE_LANES), lambda b, h: (b, 0)),
                  pl.BlockSpec((2 * NGATE, CHUNK), lambda b, h: (0, b))],
        out_specs=[act_spec, act_spec,
                   pl.BlockSpec((None, None, 2, None, DH, DH), lambda b, h: (b, 0, 0, h, 0, 0)),
                   pl.BlockSpec((None, None, 2, DH), lambda b, h: (b, h, 0, 0)),
                   pl.BlockSpec((None, None, 2, GATE_LANES), lambda b, h: (b, h, 0, 0))],
        out_shape=[hshape, hshape,
                   jax.ShapeDtypeStruct((BATCH, 1, 2, HEADS, DH, DH), F32),
                   jax.ShapeDtypeStruct((BATCH, HEADS, 2, DH), F32),
                   jax.ShapeDtypeStruct((BATCH, HEADS, 2, GATE_LANES), F32)],
        compiler_params=_params(2),
        name="scan_prompt",
    )(q, k, v, gates, gates_t)


def _scan_sample_kernel(qf_ref, kf_ref, vf_ref, gcf_ref, gtf_ref, qb_ref, kb_ref, vb_ref, gcb_ref,
                        gtb_ref, c0_ref, n0_ref, m0_ref, hf_in, hb_in, hf_ref, hb_ref,
                        c_s, n_s, m_s):
    del hf_in, hb_in
    head = pl.program_id(1)

    @pl.when(pl.program_id(2) == 0)
    def _():
        c_s[...] = c0_ref[...]
        n_s[...] = n0_ref[...]
        m_s[...] = m0_ref[...]

    units = ((qf_ref, kf_ref, vf_ref, gcf_ref, gtf_ref, hf_ref), (qb_ref, kb_ref, vb_ref, gcb_ref, gtb_ref, hb_ref))
    for d, (q_ref, k_ref, v_ref, gc_ref, gt_ref, h_ref) in enumerate(units):
        q, k, v = q_ref[...], k_ref[...], v_ref[...]
        gc, gt = gc_ref[...], gt_ref[...]
        qk = _dot_nt(q, k)
        state = (c_s[d], n_s[d:d + 1, :], m_s[d:d + 1, 0:1])
        h, c_new, n_new, m_new = _mlstm_unit(q, k, v, qk, gc, gt, head, d == 0, state)
        h_ref[...] = h.astype(BF16)
        c_s[d] = c_new
        n_s[d:d + 1, :] = n_new
        m_s[d:d + 1, :] = jnp.broadcast_to(m_new, (1, GATE_LANES))


def _scan_sample(q, k, v, gates, gates_t, c0, n0, m0, hf, hb, *, a):
    nc = DEC_SEQ // CHUNK
    base = NP_TOK // CHUNK

    def fwd(b, h, c):
        return base + b * nc + c

    def bwd(b, h, c):
        return base + b * nc + (nc - 1 - c)

    def specs(blk):
        act = pl.BlockSpec((CHUNK, DH), lambda b, h, c: (blk(b, h, c), h))
        return [act, act, act,
                pl.BlockSpec((CHUNK, GATE_LANES), lambda b, h, c: (blk(b, h, c), 0)),
                pl.BlockSpec((2 * NGATE, CHUNK), lambda b, h, c: (0, blk(b, h, c)))]

    any_spec = pl.BlockSpec(memory_space=pl.ANY)
    return pl.pallas_call(
        _scan_sample_kernel,
        grid=(DEC_BATCH, HEADS, nc),
        in_specs=specs(fwd) + specs(bwd) + [
            pl.BlockSpec((None, None, 2, None, DH, DH), lambda b, h, c: (b, a, 0, h, 0, 0)),
            pl.BlockSpec((None, None, 2, DH), lambda b, h, c: (b, h, 0, 0)),
            pl.BlockSpec((None, None, 2, GATE_LANES), lambda b, h, c: (b, h, 0, 0)),
            any_spec, any_spec],
        out_specs=[pl.BlockSpec((CHUNK, DH), lambda b, h, c: (fwd(b, h, c), h)),
                   pl.BlockSpec((CHUNK, DH), lambda b, h, c: (bwd(b, h, c), h))],
        out_shape=[jax.ShapeDtypeStruct((NTOK, INNER), BF16)] * 2,
        scratch_shapes=[pltpu.VMEM((2, DH, DH), F32), pltpu.VMEM((2, DH), F32),
                        pltpu.VMEM((2, GATE_LANES), F32)],
        input_output_aliases={13: 0, 14: 1},
        compiler_params=_params(3),
        name="scan_sample",
    )(q, k, v, gates, gates_t, q, k, v, gates, gates_t, c0, n0, m0, hf, hb)


def _mlstm_out_kernel(x_ref, hf_ref, hb_ref, xc_ref, z_ref, mod_ref, ng_ref, skip_ref, wd_ref, o_ref):
    hs = hf_ref[...].astype(F32) + hb_ref[...].astype(F32)
    parts = []
    for h in range(HEADS):
        seg = hs[:, h * DH:(h + 1) * DH]
        mu = jnp.mean(seg, axis=-1, keepdims=True)
        cen = seg - mu
        var = jnp.mean(cen * cen, axis=-1, keepdims=True)
        parts.append(cen * lax.rsqrt(var + EPS))
    hn = jnp.concatenate(parts, axis=-1) * ng_ref[...]
    out = (hn + skip_ref[...] * xc_ref[...].astype(F32)) * _silu(z_ref[...].astype(F32))
    y = _dot(out.astype(BF16), wd_ref[...])
    o_ref[...] = x_ref[...] + mod_ref[5:6, :] * y


def _mlstm_out(x, hf, hb, xc, z, mods, norm_g, skip, w_down, *, layer, a):
    tm = OUT_TM
    act_spec = pl.BlockSpec((tm, INNER), lambda i: (i, 0))
    return pl.pallas_call(
        _mlstm_out_kernel,
        grid=(NTOK // tm,),
        in_specs=[
            pl.BlockSpec((tm, D), lambda i: (i, 0)),
            act_spec, act_spec, act_spec, act_spec,
            pl.BlockSpec((None, None, 3 * N_SUB, D), lambda i: (layer, (i * tm) // GRP, 0, 0)),
            _resident((None, 1, INNER), lambda i: (a, 0, 0)),
            _resident((None, 1, INNER), lambda i: (a, 0, 0)),
            _resident((None, INNER, D), lambda i: (a, 0, 0)),
        ],
        out_specs=pl.BlockSpec((tm, D), lambda i: (i, 0)),
        out_shape=jax.ShapeDtypeStruct((NTOK, D), F32),
        compiler_params=_params(1),
        name="mlstm_out",
    )(x, hf, hb, xc, z, mods, norm_g, skip, w_down)


def _window_count(idx, win, n):
    lo = jnp.maximum(idx - win // 2, 0)
    hi = jnp.minimum(idx - win // 2 + win, n)
    return (hi - lo).astype(F32)


def _pool_finish(x_ref, o_ref, h_g, p, g, mod_ref, wp_ref, ps_ref, rows):
    cols = slice(g * POOL_GC, (g + 1) * POOL_GC)
    y = _dot((p - h_g).astype(BF16), wp_ref[g]) * ps_ref[:, cols]
    o_ref[rows, cols] = x_ref[rows, cols] + mod_ref[5:6, cols] * y


def _pool_prompt_kernel(x_ref, mod_ref, g_ref, a_ref, wp_ref, ps_ref, o_ref, *, layer):
    h = _modulate(x_ref[...], g_ref[layer, 1:2, :], mod_ref, 1)
    t = lax.broadcasted_iota(jnp.int32, (SEQ, 1), 0)
    for g, win in enumerate(POOL_WINDOWS):
        cnt = _window_count(t, win, SEQ)
        for s in range(POOL_TM // SEQ):
            rows = slice(s * SEQ, (s + 1) * SEQ)
            h_g = h[rows, g * POOL_GC:(g + 1) * POOL_GC]
            p = _split_dot(a_ref[g], h_g, 2) / cnt
            _pool_finish(x_ref, o_ref, h_g, p, g, mod_ref, wp_ref, ps_ref, rows)


def _pool_sample_kernel(xp_ref, xc_ref, xn_ref, mod_ref, g_ref, a_ref, wp_ref, ps_ref, xin, o_ref,
                        hext, *, layer):
    del xin
    tm = POOL_TM
    rows_per_tile = tm // GRID_W
    j = pl.program_id(0) % (DEC_SEQ // tm)
    gain = g_ref[layer, 1:2, :]
    hext[0:tm, :] = jnp.where(j > 0, _modulate(xp_ref[...], gain, mod_ref, 1), 0.0)
    hext[tm:2 * tm, :] = _modulate(xc_ref[...], gain, mod_ref, 1)
    hext[2 * tm:3 * tm, :] = jnp.where(j < DEC_SEQ // tm - 1, _modulate(xn_ref[...], gain, mod_ref, 1), 0.0)
    t = lax.broadcasted_iota(jnp.int32, (tm, 1), 0)
    row = j * rows_per_tile + (t >> int(math.log2(GRID_W)))
    col = t & (GRID_W - 1)
    all_rows = slice(0, tm)
    for g, win in enumerate(POOL_WINDOWS):
        cols = slice(g * POOL_GC, (g + 1) * POOL_GC)
        acc = None
        for dr in range(-(win // 2), win - win // 2):
            term = hext[pl.ds(tm + GRID_W * dr, tm), cols]
            acc = term if acc is None else acc + term
        p_rows = acc / _window_count(row, win, DEC_SEQ // GRID_W)
        p = _split_dot(a_ref[g], p_rows, 2) / _window_count(col, win, GRID_W)
        _pool_finish(xc_ref, o_ref, hext[tm:2 * tm, cols], p, g, mod_ref, wp_ref, ps_ref, all_rows)


def _band(n, win):
    t = np.arange(n)
    lo = np.clip(t - win // 2, 0, n)
    hi = np.clip(t - win // 2 + win, 0, n)
    u = np.arange(n)
    return ((u[None, :] >= lo[:, None]) & (u[None, :] < hi[:, None])).astype(np.float32)


def _pool(x, mods, norm_g, pool_w, pool_scale, *, layer, p):
    tm = POOL_TM
    a1 = jnp.asarray(np.stack([_band(SEQ, w) for w in POOL_WINDOWS]), BF16)
    a2 = jnp.asarray(np.stack([np.kron(np.eye(tm // GRID_W, dtype=np.float32), _band(GRID_W, w))
                               for w in POOL_WINDOWS]), BF16)
    ngroups = len(POOL_WINDOWS)
    common = [
        _resident((DEPTH, N_SUB, D), lambda i: (0, 0, 0)),
    ]
    wp_spec = _resident((None, ngroups, POOL_GC, POOL_GC), lambda i: (p, 0, 0, 0))
    ps_spec = _resident((None, 1, D), lambda i: (p, 0, 0))
    out = pl.pallas_call(
        functools.partial(_pool_prompt_kernel, layer=layer),
        grid=(NP_TOK // tm,),
        in_specs=[pl.BlockSpec((tm, D), lambda i: (i, 0)),
                  pl.BlockSpec((None, None, 3 * N_SUB, D), lambda i: (layer, 0, 0, 0))] + common + [
                  _resident((ngroups, SEQ, SEQ), lambda i: (0, 0, 0)), wp_spec, ps_spec],
        out_specs=pl.BlockSpec((tm, D), lambda i: (i, 0)),
        out_shape=jax.ShapeDtypeStruct((NTOK, D), F32),
        compiler_params=_params(1),
        name="pool_prompt",
    )(x, mods, norm_g, a1, pool_w, pool_scale)

    base = NP_TOK // tm
    per_seq = DEC_SEQ // tm
    last = NTOK // tm - 1
    return pl.pallas_call(
        functools.partial(_pool_sample_kernel, layer=layer),
        grid=(NS_TOK // tm,),
        in_specs=[pl.BlockSpec((tm, D), lambda i: (jnp.maximum(base + i - 1, base), 0)),
                  pl.BlockSpec((tm, D), lambda i: (base + i, 0)),
                  pl.BlockSpec((tm, D), lambda i: (jnp.minimum(base + i + 1, last), 0)),
                  pl.BlockSpec((None, None, 3 * N_SUB, D), lambda i: (layer, 1 + i // per_seq, 0, 0))]
                 + common + [_resident((ngroups, tm, tm), lambda i: (0, 0, 0)), wp_spec, ps_spec,
                             pl.BlockSpec(memory_space=pl.ANY)],
        out_specs=pl.BlockSpec((tm, D), lambda i: (base + i, 0)),
        out_shape=jax.ShapeDtypeStruct((NTOK, D), F32),
        scratch_shapes=[pltpu.VMEM((3 * tm, D), F32)],
        input_output_aliases={8: 0},
        compiler_params=_params(1),
        name="pool_sample",
    )(x, x, x, mods, norm_g, a2, pool_w, pool_scale, out)


def _block_diag_tiles(w):
    na = w.shape[0]
    ntile = INNER // MXU_TILE
    rows = jnp.swapaxes(w, -1, -2).reshape(na, ntile, MXU_TILE, QKV_BLOCK)
    rows = jnp.tile(rows, (1, 1, 1, MXU_TILE // QKV_BLOCK))
    blk = np.arange(MXU_TILE) // QKV_BLOCK
    mask = jnp.asarray(blk[:, None] == blk[None, :])
    return jnp.where(mask, rows, 0.0).astype(BF16)


def kernel(x_prompt, x_sample, state_C, state_n, state_m, c, c_ctx, ada_w, ada_b, norm_g, ffn_w_in,
           ffn_w_out, mlstm_w_up, mlstm_conv_w, mlstm_conv_b, mlstm_w_q, mlstm_w_k, mlstm_w_v,
           mlstm_w_gate, mlstm_b_gate, mlstm_norm_g, mlstm_skip, mlstm_w_down, pool_w, pool_scale,
           final_g):
    na = mlstm_w_up.shape[0]
    xs = (x_prompt.reshape(NP_TOK, D), x_sample.reshape(NS_TOK, D))
    cond8 = jnp.concatenate([c_ctx[None, :], c, jnp.zeros((8 - 1 - DEC_BATCH, D), F32)], axis=0)
    mods = _adaln(cond8, ada_w, ada_b)
    mods = jnp.transpose(mods[:, :, :1 + DEC_BATCH], (0, 2, 1, 3))

    w_in = ffn_w_in.astype(BF16)
    w_out = ffn_w_out.astype(BF16)
    fg = final_g.reshape(1, D)
    wup = mlstm_w_up.astype(BF16)
    wqk = jnp.concatenate([_block_diag_tiles(mlstm_w_q), _block_diag_tiles(mlstm_w_k)], axis=-1)
    wv = _block_diag_tiles(mlstm_w_v)
    lane_pad = ((0, 0), (0, 0), (0, GATE_LANES - 2 * NGATE))
    wg = jnp.pad(jnp.tile(mlstm_w_gate.reshape(na, 3 * INNER, NGATE), (1, 1, 2)), lane_pad).astype(BF16)
    bg = jnp.pad(jnp.tile(mlstm_b_gate.reshape(na, 1, NGATE), (1, 1, 2)), lane_pad)
    conv_b = mlstm_conv_b.reshape(na, 1, INNER)
    mng = mlstm_norm_g.reshape(na, 1, INNER)
    skip = mlstm_skip.reshape(na, 1, INNER)
    wdown = mlstm_w_down.astype(BF16)
    pw = pool_w.astype(BF16)
    ps = pool_scale.reshape(-1, 1, D)

    new_c, new_n, new_m = [], [], []
    for l in range(DEPTH):
        x = _ffn(xs if l == 0 else (x,), mods, norm_g, w_in, w_out, fg, layer=l, sub=0, which=0)
        if l % 2 == 0:
            a = l // 2
            q, k, v, xc, z, gates = _mlstm_pre(x, mods, norm_g, wup, mlstm_conv_w, conv_b, wqk, wv, wg, bg,
                                               layer=l, a=a)
            gates_t = jnp.transpose(gates[:, :2 * NGATE])
            hf, hb, c_new, n_new, m_new = _scan_prompt(q, k, v, gates, gates_t)
            n0 = jnp.transpose(state_n[:, a], (0, 2, 1, 3))
            m0 = jnp.broadcast_to(jnp.transpose(state_m[:, a], (0, 2, 1))[..., None],
                                  (DEC_BATCH, HEADS, 2, GATE_LANES))
            hf, hb = _scan_sample(q, k, v, gates, gates_t, state_C, n0, m0, hf, hb, a=a)
            x = _mlstm_out(x, hf, hb, xc, z, mods, mng, skip, wdown, layer=l, a=a)
            new_c.append(c_new)
            new_n.append(jnp.transpose(n_new, (0, 2, 1, 3))[:, None])
            new_m.append(jnp.transpose(m_new[..., 0], (0, 2, 1))[:, None])
        else:
            x = _pool(x, mods, norm_g, pw, ps, layer=l, p=l // 2)
        x = _ffn((x,), mods, norm_g, w_in, w_out, fg, layer=l, sub=2, which=1, final=(l == DEPTH - 1))
    y_prompt = x[0].reshape(BATCH, SEQ, D)
    y_sample = x[1].reshape(DEC_BATCH, DEC_SEQ, D)
    return (y_prompt, y_sample, jnp.concatenate(new_c, axis=1), jnp.concatenate(new_n, axis=1),
            jnp.concatenate(new_m, axis=1))
```

```python
import functools
import math

import numpy as np
import jax
import jax.numpy as jnp
from jax import lax
from jax.experimental import pallas as pl
from jax.experimental.pallas import tpu as pltpu

F32 = jnp.float32
BF16 = jnp.bfloat16

D = 1024
BATCH = 16
SEQ = 256
DEPTH = 2
DEC_BATCH = 2
DEC_SEQ = 4096
GRID_W = 64
N_SUB = 3
INNER = 2 * D
HEADS = 4
DH = INNER // HEADS
QKV_BLOCK = 4
CONV_K = 5
POOL_WINDOWS = (2, 4, 8, 16)
POOL_GC = D // len(POOL_WINDOWS)
D_FF = 2816
EPS = 1e-6

NP_TOK = BATCH * SEQ
NS_TOK = DEC_BATCH * DEC_SEQ
NTOK = NP_TOK + NS_TOK
GRP = 4096
assert NP_TOK == GRP and DEC_SEQ == GRP

MXU_TILE = 256
GATE_LANES = 128
CHUNK = 256
HALO = 8

NGATE = 2 * 2 * HEADS

SCAN_HEADS = 2

FFN_TM = 512
PRE_TM = CHUNK
OUT_TM = 512
POOL_TM = 512
VMEM_LIMIT = 56 * 1024 * 1024


def _silu(x):
    return x * jax.nn.sigmoid(x)


def _dot(a, b):
    return jnp.dot(a, b, preferred_element_type=F32)


def _dot_nt(a, b):
    return lax.dot_general(a, b, (((1,), (1,)), ((), ())), preferred_element_type=F32)


def _dot_tn(a, b):
    return lax.dot_general(a, b, (((0,), (0,)), ((), ())), preferred_element_type=F32)


def _resident(shape, index_map):
    return pl.BlockSpec(shape, index_map, pipeline_mode=pl.Buffered(1))


def _params(n_axes):
    return pltpu.CompilerParams(
        dimension_semantics=("arbitrary",) * n_axes, vmem_limit_bytes=VMEM_LIMIT)


def _modulate(x, g, mod_ref, sub):
    ms = jnp.mean(x * x, axis=-1, keepdims=True)
    y = x * lax.rsqrt(ms + EPS) * g
    return y * (1.0 + mod_ref[3 * sub + 1:3 * sub + 2, :]) + mod_ref[3 * sub:3 * sub + 1, :]


def _split_dot(a, x, terms, left=True):
    parts = []
    r = x
    for _ in range(terms):
        p = r.astype(BF16)
        parts.append(p)
        r = r - p.astype(F32)
    acc = None
    for p in reversed(parts):
        t = _dot(a, p) if left else _dot(p, a)
        acc = t if acc is None else acc + t
    return acc


def _adaln_kernel(cond_ref, w_ref, b_ref, o_ref):
    s = _silu(cond_ref[...]).astype(BF16)
    o_ref[...] = _dot(s, w_ref[...].astype(BF16)) + b_ref[...]


def _adaln(cond8, ada_w, ada_b):
    return pl.pallas_call(
        _adaln_kernel,
        grid=(DEPTH, 3 * N_SUB),
        in_specs=[
            pl.BlockSpec((8, D), lambda l, j: (0, 0)),
            pl.BlockSpec((None, D, D), lambda l, j: (l, 0, j)),
            pl.BlockSpec((None, None, 1, D), lambda l, j: (l, j, 0, 0)),
        ],
        out_specs=pl.BlockSpec((None, None, 8, D), lambda l, j: (l, j, 0, 0)),
        out_shape=jax.ShapeDtypeStruct((DEPTH, 3 * N_SUB, 8, D), F32),
        compiler_params=_params(2),
        name="adaln",
    )(cond8, ada_w, ada_b.reshape(DEPTH, 3 * N_SUB, 1, D))


def _ffn_kernel(*refs, layer, sub, split_in, final):
    n_x = 2 if split_in else 1
    x_refs, (mod_ref, g_ref, win_ref, wout_ref, fg_ref), o_refs = refs[:n_x], refs[n_x:n_x + 5], refs[n_x + 5:]
    is_prompt = pl.program_id(0) < NP_TOK // FFN_TM
    x = jnp.where(is_prompt, x_refs[0][...], x_refs[1][...]) if split_in else x_refs[0][...]
    h = _modulate(x, g_ref[layer, sub:sub + 1, :], mod_ref, sub).astype(BF16)
    acc = jnp.zeros(x.shape, F32)
    for c in range(D_FF // MXU_TILE):
        lo = c * MXU_TILE
        a = _dot(h, win_ref[:, lo:lo + MXU_TILE])
        b = _dot(h, win_ref[:, D_FF + lo:D_FF + lo + MXU_TILE])
        u = (_silu(a) * b).astype(BF16)
        acc = acc + _dot(u, wout_ref[lo:lo + MXU_TILE, :])
    out = x + (0.5 * mod_ref[3 * sub + 2:3 * sub + 3, :]) * acc
    if not final:
        o_refs[0][...] = out
        return
    ms = jnp.mean(out * out, axis=-1, keepdims=True)
    out = out * lax.rsqrt(ms + EPS) * fg_ref[...]

    @pl.when(is_prompt)
    def _():
        o_refs[0][...] = out

    @pl.when(jnp.logical_not(is_prompt))
    def _():
        o_refs[1][...] = out


def _ffn(xs, mods, norm_g, w_in, w_out, final_g, *, layer, sub, which, final=False):
    tm = FFN_TM
    npt = NP_TOK // tm
    prompt_spec = pl.BlockSpec((tm, D), lambda i: (jnp.minimum(i, npt - 1), 0))
    sample_spec = pl.BlockSpec((tm, D), lambda i: (jnp.maximum(i - npt, 0), 0))
    whole_spec = pl.BlockSpec((tm, D), lambda i: (i, 0))
    split_in = len(xs) == 2
    if final:
        out_specs = [prompt_spec, sample_spec]
        out_shape = [jax.ShapeDtypeStruct((NP_TOK, D), F32), jax.ShapeDtypeStruct((NS_TOK, D), F32)]
    else:
        out_specs = whole_spec
        out_shape = jax.ShapeDtypeStruct((NTOK, D), F32)
    return pl.pallas_call(
        functools.partial(_ffn_kernel, layer=layer, sub=sub, split_in=split_in, final=final),
        grid=(NTOK // tm,),
        in_specs=([prompt_spec, sample_spec] if split_in else [whole_spec]) + [
            pl.BlockSpec((None, None, 3 * N_SUB, D), lambda i: (layer, (i * tm) // GRP, 0, 0)),
            _resident((DEPTH, N_SUB, D), lambda i: (0, 0, 0)),
            _resident((None, None, D, 2 * D_FF), lambda i: (layer, which, 0, 0)),
            _resident((None, None, D_FF, D), lambda i: (layer, which, 0, 0)),
            _resident((1, D), lambda i: (0, 0)),
        ],
        out_specs=out_specs,
        out_shape=out_shape,
        compiler_params=_params(1),
        name=f"ffn_l{layer}_{which}",
    )(*xs, mods, norm_g, w_in, w_out, final_g)


def _mlstm_pre_kernel(xp_ref, xc_ref, xn_ref, mod_ref, g_ref, wup_ref, cw_ref, cb_ref, wqk_ref,
                      wv_ref, wg_ref, bg_ref, q_ref, k_ref, v_ref, xco_ref, z_ref, gate_ref,
                      xm_s, *, layer):
    tm = PRE_TM
    i = pl.program_id(0)
    xe = jnp.concatenate([xp_ref[...], xc_ref[...], xn_ref[...]], axis=0)
    h = _modulate(xe, g_ref[layer, 1:2, :], mod_ref, 1)
    xm_s[...] = _dot(h.astype(BF16), wup_ref[:, :INNER])
    z = _dot(h[HALO:HALO + tm].astype(BF16), wup_ref[:, INNER:])
    z_ref[...] = z.astype(BF16)

    seqlen = jnp.where(i < NP_TOK // tm, SEQ, DEC_SEQ)
    starts_seq = ((i * tm) & (seqlen - 1)) == 0
    ends_seq = ((i * tm + tm) & (seqlen - 1)) == 0
    xm_s[0:HALO, :] = jnp.where(starts_seq, 0.0, xm_s[0:HALO, :])
    xm_s[HALO + tm:, :] = jnp.where(ends_seq, 0.0, xm_s[HALO + tm:, :])
    xm_all = xm_s[...]
    nrow = tm + 2 * HALO
    acc = cb_ref[...] + xm_all[HALO:HALO + tm] * cw_ref[CONV_K // 2]
    for j in range(CONV_K):
        off = j - CONV_K // 2
        if off != 0:
            shifted = pltpu.roll(xm_all, (nrow - off) % nrow, axis=0)
            acc = acc + shifted[HALO:HALO + tm] * cw_ref[j]
    xc = _silu(acc)
    xco_ref[...] = xc.astype(BF16)

    for t in range(INNER // MXU_TILE):
        lo = t * MXU_TILE
        qk = _dot(xc[:, lo:lo + MXU_TILE].astype(BF16), wqk_ref[t])
        q_ref[:, lo:lo + MXU_TILE] = qk[:, :MXU_TILE].astype(BF16)
        k_ref[:, lo:lo + MXU_TILE] = (qk[:, MXU_TILE:] / math.sqrt(DH)).astype(BF16)
        xm = xm_s[pl.ds(HALO, tm), lo:lo + MXU_TILE].astype(BF16)
        v_ref[:, lo:lo + MXU_TILE] = _dot(xm, wv_ref[t]).astype(BF16)

    g = (_dot(q_ref[...], wg_ref[0:INNER, :]) + _dot(k_ref[...], wg_ref[INNER:2 * INNER, :])
         + _dot(v_ref[...], wg_ref[2 * INNER:, :]) + bg_ref[...])
    lane = lax.broadcasted_iota(jnp.int32, g.shape, 1)
    logsig = jnp.minimum(g, 0.0) - jnp.log1p(jnp.exp(-jnp.abs(g)))
    gl = jnp.where(lane < 2 * NGATE, jnp.where((lane & HEADS) != 0, logsig, g), 0.0)
    prefix = _split_dot(jnp.where(_tri(True), 1.0, 0.0).astype(BF16), gl, 3)
    suffix = _split_dot(jnp.where(_tri(False), 1.0, 0.0).astype(BF16), gl, 3)
    cum = jnp.where((lane & (2 * HEADS)) == 0, prefix, suffix)
    gate_ref[...] = jnp.where(lane < NGATE, gl, cum)


def _mlstm_pre(x, mods, norm_g, wup, conv_w, conv_b, wqk, wv, wg, bg, *, layer, a):
    tm = PRE_TM
    nb8 = NTOK // HALO
    act = jax.ShapeDtypeStruct((NTOK, INNER), BF16)
    act_spec = pl.BlockSpec((tm, INNER), lambda i: (i, 0))
    return pl.pallas_call(
        functools.partial(_mlstm_pre_kernel, layer=layer),
        grid=(NTOK // tm,),
        in_specs=[
            pl.BlockSpec((HALO, D), lambda i: (jnp.maximum(i * (tm // HALO) - 1, 0), 0)),
            pl.BlockSpec((tm, D), lambda i: (i, 0)),
            pl.BlockSpec((HALO, D), lambda i: (jnp.minimum((i + 1) * (tm // HALO), nb8 - 1), 0)),
            pl.BlockSpec((None, None, 3 * N_SUB, D), lambda i: (layer, (i * tm) // GRP, 0, 0)),
            _resident((DEPTH, N_SUB, D), lambda i: (0, 0, 0)),
            _resident((None, D, 2 * INNER), lambda i: (a, 0, 0)),
            _resident((None, CONV_K, 1, INNER), lambda i: (a, 0, 0, 0)),
            _resident((None, 1, INNER), lambda i: (a, 0, 0)),
            _resident((None, INNER // MXU_TILE, MXU_TILE, 2 * MXU_TILE), lambda i: (a, 0, 0, 0)),
            _resident((None, INNER // MXU_TILE, MXU_TILE, MXU_TILE), lambda i: (a, 0, 0, 0)),
            _resident((None, 3 * INNER, GATE_LANES), lambda i: (a, 0, 0)),
            _resident((None, 1, GATE_LANES), lambda i: (a, 0, 0)),
        ],
        out_specs=[act_spec] * 5 + [pl.BlockSpec((tm, GATE_LANES), lambda i: (i, 0))],
        out_shape=[act] * 5 + [jax.ShapeDtypeStruct((NTOK, GATE_LANES), F32)],
        scratch_shapes=[pltpu.VMEM((tm + 2 * HALO, INNER), F32)],
        compiler_params=_params(1),
        name="mlstm_pre",
    )(x, x, x, mods, norm_g, wup, conv_w, conv_b, wqk, wv, wg, bg)


def _tri(lower):
    r = lax.broadcasted_iota(jnp.int32, (CHUNK, CHUNK), 0)
    c = lax.broadcasted_iota(jnp.int32, (CHUNK, CHUNK), 1)
    return (c <= r) if lower else (c >= r)


def _pick_col(x, col):
    lane = lax.broadcasted_iota(jnp.int32, x.shape, 1)
    return jnp.sum(jnp.where(lane == col, x, 0.0), axis=1, keepdims=True)


def _pick_row(x, row):
    sub = lax.broadcasted_iota(jnp.int32, x.shape, 0)
    return jnp.sum(jnp.where(sub == row, x, 0.0), axis=0, keepdims=True)


def _mlstm_unit(q, k, v, qk, gc, gt, head, lower, state):
    d = 0 if lower else 1
    col_i = d * 2 * HEADS + head
    col_f = NGATE + col_i + HEADS
    bc = _pick_col(gc, col_f)
    ic = _pick_col(gc, col_i)
    br = _pick_row(gt, col_f)
    ir = _pick_row(gt, col_i)
    if state is None:
        m0 = jnp.zeros((1, 1), F32)
    else:
        c0, n0, m0 = state
    log_d = jnp.where(_tri(lower), bc - br + ir, -jnp.inf)
    inter = bc + m0
    mt = jnp.maximum(inter, jnp.max(log_d, axis=-1, keepdims=True))
    s = qk * jnp.exp(log_d - mt)
    num = _dot(s.astype(BF16), v)
    den = jnp.sum(s, axis=-1, keepdims=True)
    if state is not None:
        ws = jnp.exp(inter - mt)
        num = num + ws * _dot_nt(q, c0.astype(BF16))
        qn = _dot_nt(q, jnp.broadcast_to(n0, (GATE_LANES, DH)).astype(BF16))
        den = den + ws * qn[:, 0:1]
    h = num / jnp.maximum(jnp.abs(den), jnp.exp(-mt))

    b_last = bc[CHUNK - 1:CHUNK, :] if lower else bc[0:1, :]
    g = b_last - bc + ic
    m_new = jnp.maximum(b_last + m0, jnp.max(g, axis=0, keepdims=True))
    w = jnp.exp(g - m_new)
    c_new = _dot_tn((v.astype(F32) * w).astype(BF16), k)
    n_new = _dot_tn(jnp.broadcast_to(w, (CHUNK, GATE_LANES)).astype(BF16), k)[0:1, :]
    if state is not None:
        dec = jnp.exp(b_last + m0 - m_new)
        c_new = dec * c0 + c_new
        n_new = dec * n0 + n_new
    return h, c_new, n_new, m_new


def _scan_prompt_kernel(q_ref, k_ref, v_ref, gc_ref, gt_ref, hf_ref, hb_ref, c_ref, n_ref, m_ref):
    gc, gt = gc_ref[...], gt_ref[...]
    for j in range(SCAN_HEADS):
        head = pl.program_id(1) * SCAN_HEADS + j
        cols = slice(j * DH, (j + 1) * DH)
        q, k, v = q_ref[:, cols], k_ref[:, cols], v_ref[:, cols]
        qk = _dot_nt(q, k)
        for d, h_ref in enumerate((hf_ref, hb_ref)):
            h, c_new, n_new, m_new = _mlstm_unit(q, k, v, qk, gc, gt, head, d == 0, None)
            h_ref[:, cols] = h.astype(BF16)
            c_ref[d, j] = c_new
            n_ref[j, d:d + 1, :] = n_new
            m_ref[j, d:d + 1, :] = jnp.broadcast_to(m_new, (1, GATE_LANES))


def _scan_prompt(q, k, v, gates, gates_t):
    hps = SCAN_HEADS
    act_spec = pl.BlockSpec((CHUNK, hps * DH), lambda b, h: (b, h))
    hshape = jax.ShapeDtypeStruct((NP_TOK, INNER), BF16)
    return pl.pallas_call(
        _scan_prompt_kernel,
        grid=(BATCH, HEADS // hps),
        in_specs=[act_spec, act_spec, act_spec,
                  pl.BlockSpec((CHUNK, GATE_LANES), lambda b, h: (b, 0)),
                  pl.BlockSpec((2 * NGATE, CHUNK), lambda b, h: (0, b))],
        out_specs=[act_spec, act_spec,
                   pl.BlockSpec((None, None, 2, hps, DH, DH), lambda b, h: (b, 0, 0, h, 0, 0)),
                   pl.BlockSpec((None, hps, 2, DH), lambda b, h: (b, h, 0, 0)),
                   pl.BlockSpec((None, hps, 2, GATE_LANES), lambda b, h: (b, h, 0, 0))],
        out_shape=[hshape, hshape,
                   jax.ShapeDtypeStruct((BATCH, 1, 2, HEADS, DH, DH), F32),
                   jax.ShapeDtypeStruct((BATCH, HEADS, 2, DH), F32),
                   jax.ShapeDtypeStruct((BATCH, HEADS, 2, GATE_LANES), F32)],
        compiler_params=_params(2),
        name="scan_prompt",
    )(q, k, v, gates, gates_t)


def _scan_sample_kernel(qf_ref, kf_ref, vf_ref, gcf_ref, gtf_ref, qb_ref, kb_ref, vb_ref, gcb_ref,
                        gtb_ref, c0_ref, n0_ref, m0_ref, hf_ref, hb_ref, c_s, n_s, m_s):
    @pl.when(pl.program_id(2) == 0)
    def _():
        for j in range(SCAN_HEADS):
            for d in range(2):
                c_s[j, d] = c0_ref[d, j]
        n_s[...] = n0_ref[...]
        m_s[...] = m0_ref[...]

    units = ((qf_ref, kf_ref, vf_ref, gcf_ref, gtf_ref, hf_ref), (qb_ref, kb_ref, vb_ref, gcb_ref, gtb_ref, hb_ref))
    for j in range(SCAN_HEADS):
        head = pl.program_id(1) * SCAN_HEADS + j
        cols = slice(j * DH, (j + 1) * DH)
        for d, (q_ref, k_ref, v_ref, gc_ref, gt_ref, h_ref) in enumerate(units):
            q, k, v = q_ref[:, cols], k_ref[:, cols], v_ref[:, cols]
            qk = _dot_nt(q, k)
            state = (c_s[j, d], n_s[j, d:d + 1, :], m_s[j, d:d + 1, 0:1])
            h, c_new, n_new, m_new = _mlstm_unit(q, k, v, qk, gc_ref[...], gt_ref[...], head, d == 0, state)
            h_ref[:, cols] = h.astype(BF16)
            c_s[j, d] = c_new
            n_s[j, d:d + 1, :] = n_new
            m_s[j, d:d + 1, :] = jnp.broadcast_to(m_new, (1, GATE_LANES))


def _scan_sample(q, k, v, gates, gates_t, c0, n0, m0, *, a):
    hps = SCAN_HEADS
    nc = DEC_SEQ // CHUNK
    base = NP_TOK // CHUNK

    def fwd(b, h, c):
        return b * nc + c

    def bwd(b, h, c):
        return b * nc + (nc - 1 - c)

    def specs(blk):
        act = pl.BlockSpec((CHUNK, hps * DH), lambda b, h, c: (base + blk(b, h, c), h))
        return [act, act, act,
                pl.BlockSpec((CHUNK, GATE_LANES), lambda b, h, c: (base + blk(b, h, c), 0)),
                pl.BlockSpec((2 * NGATE, CHUNK), lambda b, h, c: (0, base + blk(b, h, c)))]

    return pl.pallas_call(
        _scan_sample_kernel,
        grid=(DEC_BATCH, HEADS // hps, nc),
        in_specs=specs(fwd) + specs(bwd) + [
            pl.BlockSpec((None, None, 2, hps, DH, DH), lambda b, h, c: (b, a, 0, h, 0, 0)),
            pl.BlockSpec((None, hps, 2, DH), lambda b, h, c: (b, h, 0, 0)),
            pl.BlockSpec((None, hps, 2, GATE_LANES), lambda b, h, c: (b, h, 0, 0))],
        out_specs=[pl.BlockSpec((CHUNK, hps * DH), lambda b, h, c: (fwd(b, h, c), h)),
                   pl.BlockSpec((CHUNK, hps * DH), lambda b, h, c: (bwd(b, h, c), h))],
        out_shape=[jax.ShapeDtypeStruct((NS_TOK, INNER), BF16)] * 2,
        scratch_shapes=[pltpu.VMEM((hps, 2, DH, DH), F32), pltpu.VMEM((hps, 2, DH), F32),
                        pltpu.VMEM((hps, 2, GATE_LANES), F32)],
        compiler_params=_params(3),
        name="scan_sample",
    )(q, k, v, gates, gates_t, q, k, v, gates, gates_t, c0, n0, m0)


def _mlstm_out_kernel(x_ref, hfp_ref, hbp_ref, hfs_ref, hbs_ref, xc_ref, z_ref, mod_ref, ng_ref,
                      skip_ref, wd_ref, o_ref):
    is_prompt = pl.program_id(0) < NP_TOK // OUT_TM
    hf = jnp.where(is_prompt, hfp_ref[...], hfs_ref[...])
    hb = jnp.where(is_prompt, hbp_ref[...], hbs_ref[...])
    hs = hf.astype(F32) + hb.astype(F32)
    parts = []
    for h in range(HEADS):
        seg = hs[:, h * DH:(h + 1) * DH]
        mu = jnp.mean(seg, axis=-1, keepdims=True)
        cen = seg - mu
        var = jnp.mean(cen * cen, axis=-1, keepdims=True)
        parts.append(cen * lax.rsqrt(var + EPS))
    hn = jnp.concatenate(parts, axis=-1) * ng_ref[...]
    out = (hn + skip_ref[...] * xc_ref[...].astype(F32)) * _silu(z_ref[...].astype(F32))
    y = _dot(out.astype(BF16), wd_ref[...])
    o_ref[...] = x_ref[...] + mod_ref[5:6, :] * y


def _mlstm_out(x, hf_p, hb_p, hf_s, hb_s, xc, z, mods, norm_g, skip, w_down, *, layer, a):
    tm = OUT_TM
    npt = NP_TOK // tm
    act_spec = pl.BlockSpec((tm, INNER), lambda i: (i, 0))
    prompt_spec = pl.BlockSpec((tm, INNER), lambda i: (jnp.minimum(i, npt - 1), 0))
    sample_spec = pl.BlockSpec((tm, INNER), lambda i: (jnp.maximum(i - npt, 0), 0))
    return pl.pallas_call(
        _mlstm_out_kernel,
        grid=(NTOK // tm,),
        in_specs=[
            pl.BlockSpec((tm, D), lambda i: (i, 0)),
            prompt_spec, prompt_spec, sample_spec, sample_spec, act_spec, act_spec,
            pl.BlockSpec((None, None, 3 * N_SUB, D), lambda i: (layer, (i * tm) // GRP, 0, 0)),
            _resident((None, 1, INNER), lambda i: (a, 0, 0)),
            _resident((None, 1, INNER), lambda i: (a, 0, 0)),
            _resident((None, INNER, D), lambda i: (a, 0, 0)),
        ],
        out_specs=pl.BlockSpec((tm, D), lambda i: (i, 0)),
        out_shape=jax.ShapeDtypeStruct((NTOK, D), F32),
        compiler_params=_params(1),
        name="mlstm_out",
    )(x, hf_p, hb_p, hf_s, hb_s, xc, z, mods, norm_g, skip, w_down)


def _window_count(idx, win, n):
    lo = jnp.maximum(idx - win // 2, 0)
    hi = jnp.minimum(idx - win // 2 + win, n)
    return (hi - lo).astype(F32)


def _pool_finish(x_ref, o_ref, h_g, p, g, mod_ref, wp_ref, ps_ref, rows):
    cols = slice(g * POOL_GC, (g + 1) * POOL_GC)
    y = _dot((p - h_g).astype(BF16), wp_ref[g]) * ps_ref[:, cols]
    o_ref[rows, cols] = x_ref[rows, cols] + mod_ref[5:6, cols] * y


def _pool_prompt_kernel(x_ref, mod_ref, g_ref, a_ref, wp_ref, ps_ref, o_ref, *, layer):
    h = _modulate(x_ref[...], g_ref[layer, 1:2, :], mod_ref, 1)
    t = lax.broadcasted_iota(jnp.int32, (SEQ, 1), 0)
    for g, win in enumerate(POOL_WINDOWS):
        cnt = _window_count(t, win, SEQ)
        for s in range(POOL_TM // SEQ):
            rows = slice(s * SEQ, (s + 1) * SEQ)
            h_g = h[rows, g * POOL_GC:(g + 1) * POOL_GC]
            p = _split_dot(a_ref[g], h_g, 2) / cnt
            _pool_finish(x_ref, o_ref, h_g, p, g, mod_ref, wp_ref, ps_ref, rows)


def _pool_sample_kernel(xc_ref, xn_ref, mod_ref, g_ref, a_ref, wp_ref, ps_ref, o_ref, hext, *, layer):
    tm = POOL_TM
    rows_per_tile = tm // GRID_W
    tiles_per_seq = DEC_SEQ // tm
    j = pl.program_id(0) % tiles_per_seq
    gain = g_ref[layer, 1:2, :]

    @pl.when(j == 0)
    def _():
        hext[0:tm, :] = jnp.zeros((tm, D), F32)
        hext[tm:2 * tm, :] = _modulate(xc_ref[...], gain, mod_ref, 1)

    @pl.when(j > 0)
    def _():
        hext[0:tm, :] = hext[tm:2 * tm, :]
        hext[tm:2 * tm, :] = hext[2 * tm:3 * tm, :]

    @pl.when(j < tiles_per_seq - 1)
    def _():
        hext[2 * tm:3 * tm, :] = _modulate(xn_ref[...], gain, mod_ref, 1)

    @pl.when(j == tiles_per_seq - 1)
    def _():
        hext[2 * tm:3 * tm, :] = jnp.zeros((tm, D), F32)

    t = lax.broadcasted_iota(jnp.int32, (tm, 1), 0)
    row = j * rows_per_tile + (t >> int(math.log2(GRID_W)))
    col = t & (GRID_W - 1)
    all_rows = slice(0, tm)
    for g, win in enumerate(POOL_WINDOWS):
        cols = slice(g * POOL_GC, (g + 1) * POOL_GC)
        acc = None
        for dr in range(-(win // 2), win - win // 2):
            term = hext[pl.ds(tm + GRID_W * dr, tm), cols]
            acc = term if acc is None else acc + term
        p_rows = acc / _window_count(row, win, DEC_SEQ // GRID_W)
        p = _split_dot(a_ref[g], p_rows, 2) / _window_count(col, win, GRID_W)
        _pool_finish(xc_ref, o_ref, hext[tm:2 * tm, cols], p, g, mod_ref, wp_ref, ps_ref, all_rows)


def _band(n, win):
    t = np.arange(n)
    lo = np.clip(t - win // 2, 0, n)
    hi = np.clip(t - win // 2 + win, 0, n)
    u = np.arange(n)
    return ((u[None, :] >= lo[:, None]) & (u[None, :] < hi[:, None])).astype(np.float32)


def _pool(x, mods, norm_g, pool_w, pool_scale, *, layer, p):
    tm = POOL_TM
    a1 = jnp.asarray(np.stack([_band(SEQ, w) for w in POOL_WINDOWS]), BF16)
    a2 = jnp.asarray(np.stack([np.kron(np.eye(tm // GRID_W, dtype=np.float32), _band(GRID_W, w))
                               for w in POOL_WINDOWS]), BF16)
    ngroups = len(POOL_WINDOWS)
    common = [
        _resident((DEPTH, N_SUB, D), lambda i: (0, 0, 0)),
    ]
    wp_spec = _resident((None, ngroups, POOL_GC, POOL_GC), lambda i: (p, 0, 0, 0))
    ps_spec = _resident((None, 1, D), lambda i: (p, 0, 0))
    out_p = pl.pallas_call(
        functools.partial(_pool_prompt_kernel, layer=layer),
        grid=(NP_TOK // tm,),
        in_specs=[pl.BlockSpec((tm, D), lambda i: (i, 0)),
                  pl.BlockSpec((None, None, 3 * N_SUB, D), lambda i: (layer, 0, 0, 0))] + common + [
                  _resident((ngroups, SEQ, SEQ), lambda i: (0, 0, 0)), wp_spec, ps_spec],
        out_specs=pl.BlockSpec((tm, D), lambda i: (i, 0)),
        out_shape=jax.ShapeDtypeStruct((NP_TOK, D), F32),
        compiler_params=_params(1),
        name="pool_prompt",
    )(x, mods, norm_g, a1, pool_w, pool_scale)

    base = NP_TOK // tm
    per_seq = DEC_SEQ // tm
    last = NTOK // tm - 1
    out_s = pl.pallas_call(
        functools.partial(_pool_sample_kernel, layer=layer),
        grid=(NS_TOK // tm,),
        in_specs=[pl.BlockSpec((tm, D), lambda i: (base + i, 0)),
                  pl.BlockSpec((tm, D), lambda i: (jnp.minimum(base + i + 1, last), 0)),
                  pl.BlockSpec((None, None, 3 * N_SUB, D), lambda i: (layer, 1 + i // per_seq, 0, 0))]
                 + common + [_resident((ngroups, tm, tm), lambda i: (0, 0, 0)), wp_spec, ps_spec],
        out_specs=pl.BlockSpec((tm, D), lambda i: (i, 0)),
        out_shape=jax.ShapeDtypeStruct((NS_TOK, D), F32),
        scratch_shapes=[pltpu.VMEM((3 * tm, D), F32)],
        compiler_params=_params(1),
        name="pool_sample",
    )(x, x, mods, norm_g, a2, pool_w, pool_scale)
    return out_p, out_s


def _block_diag_tiles(w):
    na = w.shape[0]
    ntile = INNER // MXU_TILE
    rows = jnp.swapaxes(w, -1, -2).reshape(na, ntile, MXU_TILE, QKV_BLOCK)
    rows = jnp.tile(rows, (1, 1, 1, MXU_TILE // QKV_BLOCK))
    blk = np.arange(MXU_TILE) // QKV_BLOCK
    mask = jnp.asarray(blk[:, None] == blk[None, :])
    return jnp.where(mask, rows, 0.0).astype(BF16)


def kernel(x_prompt, x_sample, state_C, state_n, state_m, c, c_ctx, ada_w, ada_b, norm_g, ffn_w_in,
           ffn_w_out, mlstm_w_up, mlstm_conv_w, mlstm_conv_b, mlstm_w_q, mlstm_w_k, mlstm_w_v,
           mlstm_w_gate, mlstm_b_gate, mlstm_norm_g, mlstm_skip, mlstm_w_down, pool_w, pool_scale,
           final_g):
    na = mlstm_w_up.shape[0]
    xs = (x_prompt.reshape(NP_TOK, D), x_sample.reshape(NS_TOK, D))
    cond8 = jnp.concatenate([c_ctx[None, :], c, jnp.zeros((8 - 1 - DEC_BATCH, D), F32)], axis=0)
    mods = _adaln(cond8, ada_w, ada_b)
    mods = jnp.transpose(mods[:, :, :1 + DEC_BATCH], (0, 2, 1, 3))

    w_in = ffn_w_in.astype(BF16)
    w_out = ffn_w_out.astype(BF16)
    fg = final_g.reshape(1, D)
    wup = mlstm_w_up.astype(BF16)
    wqk = jnp.concatenate([_block_diag_tiles(mlstm_w_q), _block_diag_tiles(mlstm_w_k)], axis=-1)
    wv = _block_diag_tiles(mlstm_w_v)
    lane_pad = ((0, 0), (0, 0), (0, GATE_LANES - 2 * NGATE))
    wg = jnp.pad(jnp.tile(mlstm_w_gate.reshape(na, 3 * INNER, NGATE), (1, 1, 2)), lane_pad).astype(BF16)
    bg = jnp.pad(jnp.tile(mlstm_b_gate.reshape(na, 1, NGATE), (1, 1, 2)), lane_pad)
    conv_b = mlstm_conv_b.reshape(na, 1, INNER)
    mng = mlstm_norm_g.reshape(na, 1, INNER)
    skip = mlstm_skip.reshape(na, 1, INNER)
    wdown = mlstm_w_down.astype(BF16)
    pw = pool_w.astype(BF16)
    ps = pool_scale.reshape(-1, 1, D)

    new_c, new_n, new_m = [], [], []
    for l in range(DEPTH):
        x = _ffn(xs if l == 0 else (x,), mods, norm_g, w_in, w_out, fg, layer=l, sub=0, which=0)
        if l % 2 == 0:
            a = l // 2
            q, k, v, xc, z, gates = _mlstm_pre(x, mods, norm_g, wup, mlstm_conv_w, conv_b, wqk, wv, wg, bg,
                                               layer=l, a=a)
            gates_t = jnp.transpose(gates[:, :2 * NGATE])
            hf_p, hb_p, c_new, n_new, m_new = _scan_prompt(q, k, v, gates, gates_t)
            n0 = jnp.transpose(state_n[:, a], (0, 2, 1, 3))
            m0 = jnp.broadcast_to(jnp.transpose(state_m[:, a], (0, 2, 1))[..., None],
                                  (DEC_BATCH, HEADS, 2, GATE_LANES))
            hf_s, hb_s = _scan_sample(q, k, v, gates, gates_t, state_C, n0, m0, a=a)
            x = _mlstm_out(x, hf_p, hb_p, hf_s, hb_s, xc, z, mods, mng, skip, wdown, layer=l, a=a)
            new_c.append(c_new)
            new_n.append(jnp.transpose(n_new, (0, 2, 1, 3))[:, None])
            new_m.append(jnp.transpose(m_new[..., 0], (0, 2, 1))[:, None])
            x = (x,)
        else:
            x = _pool(x, mods, norm_g, pw, ps, layer=l, p=l // 2)
        x = _ffn(x, mods, norm_g, w_in, w_out, fg, layer=l, sub=2, which=1, final=(l == DEPTH - 1))
    y_prompt = x[0].reshape(BATCH, SEQ, D)
    y_sample = x[1].reshape(DEC_BATCH, DEC_SEQ, D)
    return (y_prompt, y_sample, jnp.concatenate(new_c, axis=1), jnp.concatenate(new_n, axis=1),
            jnp.concatenate(new_m, axis=1))
```

```python
import functools
import math

import numpy as np
import jax
import jax.numpy as jnp
from jax import lax
from jax.experimental import pallas as pl
from jax.experimental.pallas import tpu as pltpu

F32 = jnp.float32
BF16 = jnp.bfloat16

D = 1024
BATCH = 16
SEQ = 256
DEPTH = 2
DEC_BATCH = 2
DEC_SEQ = 4096
GRID_W = 64
N_SUB = 3
INNER = 2 * D
HEADS = 4
DH = INNER // HEADS
QKV_BLOCK = 4
CONV_K = 5
POOL_WINDOWS = (2, 4, 8, 16)
POOL_GC = D // len(POOL_WINDOWS)
D_FF = 2816
EPS = 1e-6

NP_TOK = BATCH * SEQ
NS_TOK = DEC_BATCH * DEC_SEQ
NTOK = NP_TOK + NS_TOK
GRP = 4096
assert NP_TOK == GRP and DEC_SEQ == GRP

MXU_TILE = 256
GATE_LANES = 128
CHUNK = 256
HALO = 16

NGATE = 2 * 2 * HEADS

SCAN_HEADS = 2

FFN_TM = 512
PRE_TM = CHUNK
OUT_TM = 512
POOL_TM = 512
VMEM_LIMIT = 56 * 1024 * 1024


def _silu(x):
    return x * jax.nn.sigmoid(x)


def _dot(a, b):
    return jnp.dot(a, b, preferred_element_type=F32)


def _dot_nt(a, b):
    return lax.dot_general(a, b, (((1,), (1,)), ((), ())), preferred_element_type=F32)


def _dot_tn(a, b):
    return lax.dot_general(a, b, (((0,), (0,)), ((), ())), preferred_element_type=F32)


def _resident(shape, index_map):
    return pl.BlockSpec(shape, index_map, pipeline_mode=pl.Buffered(1))


def _params(n_axes):
    return pltpu.CompilerParams(
        dimension_semantics=("arbitrary",) * n_axes, vmem_limit_bytes=VMEM_LIMIT)


def _modulate(x, g, mod_ref, sub):
    ms = jnp.mean(x * x, axis=-1, keepdims=True)
    y = x * lax.rsqrt(ms + EPS) * g
    return y * (1.0 + mod_ref[3 * sub + 1:3 * sub + 2, :]) + mod_ref[3 * sub:3 * sub + 1, :]


def _split_dot(a, x, terms, left=True):
    parts = []
    r = x
    for _ in range(terms):
        p = r.astype(BF16)
        parts.append(p)
        r = r - p.astype(F32)
    acc = None
    for p in reversed(parts):
        t = _dot(a, p) if left else _dot(p, a)
        acc = t if acc is None else acc + t
    return acc


def _adaln_kernel(cond_ref, w_ref, b_ref, o_ref):
    s = _silu(cond_ref[...]).astype(BF16)
    o_ref[...] = _dot(s, w_ref[...].astype(BF16)) + b_ref[...]


def _adaln(cond8, ada_w, ada_b):
    return pl.pallas_call(
        _adaln_kernel,
        grid=(DEPTH, 3 * N_SUB),
        in_specs=[
            pl.BlockSpec((8, D), lambda l, j: (0, 0)),
            pl.BlockSpec((None, D, D), lambda l, j: (l, 0, j)),
            pl.BlockSpec((None, None, 1, D), lambda l, j: (l, j, 0, 0)),
        ],
        out_specs=pl.BlockSpec((None, None, 8, D), lambda l, j: (l, j, 0, 0)),
        out_shape=jax.ShapeDtypeStruct((DEPTH, 3 * N_SUB, 8, D), F32),
        compiler_params=_params(2),
        name="adaln",
    )(cond8, ada_w, ada_b.reshape(DEPTH, 3 * N_SUB, 1, D))


def _ffn_kernel(*refs, layer, sub, split_in, emit_h, final):
    n_x = 2 if split_in else 1
    x_refs, (mod_ref, g_ref, win_ref, wout_ref, fg_ref), o_refs = refs[:n_x], refs[n_x:n_x + 5], refs[n_x + 5:]
    is_prompt = pl.program_id(0) < NP_TOK // FFN_TM
    x = jnp.where(is_prompt, x_refs[0][...], x_refs[1][...]) if split_in else x_refs[0][...]
    h = _modulate(x, g_ref[layer, sub:sub + 1, :], mod_ref, sub).astype(BF16)
    acc = jnp.zeros(x.shape, F32)
    for c in range(D_FF // MXU_TILE):
        lo = c * MXU_TILE
        a = _dot(h, win_ref[:, lo:lo + MXU_TILE].astype(BF16))
        b = _dot(h, win_ref[:, D_FF + lo:D_FF + lo + MXU_TILE].astype(BF16))
        u = (_silu(a) * b).astype(BF16)
        acc = acc + _dot(u, wout_ref[lo:lo + MXU_TILE, :].astype(BF16))
    out = x + (0.5 * mod_ref[3 * sub + 2:3 * sub + 3, :]) * acc
    if not final:
        o_refs[0][...] = out
        if emit_h:
            o_refs[1][...] = _modulate(out, g_ref[layer, 1:2, :], mod_ref, 1).astype(BF16)
        return
    ms = jnp.mean(out * out, axis=-1, keepdims=True)
    out = out * lax.rsqrt(ms + EPS) * fg_ref[...]

    @pl.when(is_prompt)
    def _():
        o_refs[0][...] = out

    @pl.when(jnp.logical_not(is_prompt))
    def _():
        o_refs[1][...] = out


def _ffn(xs, mods, norm_g, w_in, w_out, final_g, *, layer, sub, which, final=False):
    tm = FFN_TM
    npt = NP_TOK // tm
    prompt_spec = pl.BlockSpec((tm, D), lambda i: (jnp.minimum(i, npt - 1), 0))
    sample_spec = pl.BlockSpec((tm, D), lambda i: (jnp.maximum(i - npt, 0), 0))
    whole_spec = pl.BlockSpec((tm, D), lambda i: (i, 0))
    split_in = len(xs) == 2
    emit_h = sub == 0
    if final:
        out_specs = [prompt_spec, sample_spec]
        out_shape = [jax.ShapeDtypeStruct((NP_TOK, D), F32), jax.ShapeDtypeStruct((NS_TOK, D), F32)]
    elif emit_h:
        out_specs = [whole_spec, whole_spec]
        out_shape = [jax.ShapeDtypeStruct((NTOK, D), F32), jax.ShapeDtypeStruct((NTOK, D), BF16)]
    else:
        out_specs = whole_spec
        out_shape = jax.ShapeDtypeStruct((NTOK, D), F32)
    return pl.pallas_call(
        functools.partial(_ffn_kernel, layer=layer, sub=sub, split_in=split_in, emit_h=emit_h,
                          final=final),
        grid=(NTOK // tm,),
        in_specs=([prompt_spec, sample_spec] if split_in else [whole_spec]) + [
            pl.BlockSpec((None, None, 3 * N_SUB, D), lambda i: (layer, (i * tm) // GRP, 0, 0)),
            _resident((DEPTH, N_SUB, D), lambda i: (0, 0, 0)),
            _resident((None, None, D, 2 * D_FF), lambda i: (layer, which, 0, 0)),
            _resident((None, None, D_FF, D), lambda i: (layer, which, 0, 0)),
            _resident((1, D), lambda i: (0, 0)),
        ],
        out_specs=out_specs,
        out_shape=out_shape,
        compiler_params=_params(1),
        name=f"ffn_l{layer}_{which}",
    )(*xs, mods, norm_g, w_in, w_out, final_g)


def _mlstm_pre_kernel(hp_ref, hc_ref, hn_ref, wup_ref, cw_ref, cb_ref, wqk_ref, wv_ref, wg_ref,
                      bg_ref, q_ref, k_ref, v_ref, xco_ref, z_ref, gate_ref, xm_s):
    tm = PRE_TM
    i = pl.program_id(0)
    he = jnp.concatenate([hp_ref[...], hc_ref[...], hn_ref[...]], axis=0)
    xm_s[...] = _dot(he, wup_ref[:, :INNER])
    z_ref[...] = _dot(hc_ref[...], wup_ref[:, INNER:]).astype(BF16)

    seqlen = jnp.where(i < NP_TOK // tm, SEQ, DEC_SEQ)
    starts_seq = ((i * tm) & (seqlen - 1)) == 0
    ends_seq = ((i * tm + tm) & (seqlen - 1)) == 0
    xm_s[0:HALO, :] = jnp.where(starts_seq, 0.0, xm_s[0:HALO, :])
    xm_s[HALO + tm:, :] = jnp.where(ends_seq, 0.0, xm_s[HALO + tm:, :])
    xm_all = xm_s[...]
    nrow = tm + 2 * HALO
    acc = cb_ref[...] + xm_all[HALO:HALO + tm] * cw_ref[CONV_K // 2]
    for j in range(CONV_K):
        off = j - CONV_K // 2
        if off != 0:
            shifted = pltpu.roll(xm_all, (nrow - off) % nrow, axis=0)
            acc = acc + shifted[HALO:HALO + tm] * cw_ref[j]
    xc = _silu(acc)
    xco_ref[...] = xc.astype(BF16)

    for t in range(INNER // MXU_TILE):
        lo = t * MXU_TILE
        qk = _dot(xc[:, lo:lo + MXU_TILE].astype(BF16), wqk_ref[t])
        q_ref[:, lo:lo + MXU_TILE] = qk[:, :MXU_TILE].astype(BF16)
        k_ref[:, lo:lo + MXU_TILE] = (qk[:, MXU_TILE:] / math.sqrt(DH)).astype(BF16)
        xm = xm_s[pl.ds(HALO, tm), lo:lo + MXU_TILE].astype(BF16)
        v_ref[:, lo:lo + MXU_TILE] = _dot(xm, wv_ref[t]).astype(BF16)

    g = (_dot(q_ref[...], wg_ref[0:INNER, :]) + _dot(k_ref[...], wg_ref[INNER:2 * INNER, :])
         + _dot(v_ref[...], wg_ref[2 * INNER:, :]) + bg_ref[...])
    lane = lax.broadcasted_iota(jnp.int32, g.shape, 1)
    logsig = jnp.minimum(g, 0.0) - jnp.log1p(jnp.exp(-jnp.abs(g)))
    gl = jnp.where(lane < 2 * NGATE, jnp.where((lane & HEADS) != 0, logsig, g), 0.0)
    prefix = _split_dot(jnp.where(_tri(True), 1.0, 0.0).astype(BF16), gl, 3)
    suffix = _split_dot(jnp.where(_tri(False), 1.0, 0.0).astype(BF16), gl, 3)
    cum = jnp.where((lane & (2 * HEADS)) == 0, prefix, suffix)
    gate_ref[...] = jnp.where(lane < NGATE, gl, cum)


def _mlstm_pre(hm, wup, conv_w, conv_b, wqk, wv, wg, bg, *, a):
    tm = PRE_TM
    nhalo = NTOK // HALO
    act = jax.ShapeDtypeStruct((NTOK, INNER), BF16)
    act_spec = pl.BlockSpec((tm, INNER), lambda i: (i, 0))
    return pl.pallas_call(
        _mlstm_pre_kernel,
        grid=(NTOK // tm,),
        in_specs=[
            pl.BlockSpec((HALO, D), lambda i: (jnp.maximum(i * (tm // HALO) - 1, 0), 0)),
            pl.BlockSpec((tm, D), lambda i: (i, 0)),
            pl.BlockSpec((HALO, D), lambda i: (jnp.minimum((i + 1) * (tm // HALO), nhalo - 1), 0)),
            _resident((None, D, 2 * INNER), lambda i: (a, 0, 0)),
            _resident((None, CONV_K, 1, INNER), lambda i: (a, 0, 0, 0)),
            _resident((None, 1, INNER), lambda i: (a, 0, 0)),
            _resident((None, INNER // MXU_TILE, MXU_TILE, 2 * MXU_TILE), lambda i: (a, 0, 0, 0)),
            _resident((None, INNER // MXU_TILE, MXU_TILE, MXU_TILE), lambda i: (a, 0, 0, 0)),
            _resident((None, 3 * INNER, GATE_LANES), lambda i: (a, 0, 0)),
            _resident((None, 1, GATE_LANES), lambda i: (a, 0, 0)),
        ],
        out_specs=[act_spec] * 5 + [pl.BlockSpec((tm, GATE_LANES), lambda i: (i, 0))],
        out_shape=[act] * 5 + [jax.ShapeDtypeStruct((NTOK, GATE_LANES), F32)],
        scratch_shapes=[pltpu.VMEM((tm + 2 * HALO, INNER), F32)],
        compiler_params=_params(1),
        name="mlstm_pre",
    )(hm, hm, hm, wup, conv_w, conv_b, wqk, wv, wg, bg)


def _tri(lower):
    r = lax.broadcasted_iota(jnp.int32, (CHUNK, CHUNK), 0)
    c = lax.broadcasted_iota(jnp.int32, (CHUNK, CHUNK), 1)
    return (c <= r) if lower else (c >= r)


def _pick_col(x, col):
    lane = lax.broadcasted_iota(jnp.int32, x.shape, 1)
    return jnp.sum(jnp.where(lane == col, x, 0.0), axis=1, keepdims=True)


def _pick_row(x, row):
    sub = lax.broadcasted_iota(jnp.int32, x.shape, 0)
    return jnp.sum(jnp.where(sub == row, x, 0.0), axis=0, keepdims=True)


def _mlstm_unit(q, k, v, qk, gc, gt, head, lower, state):
    d = 0 if lower else 1
    col_i = d * 2 * HEADS + head
    col_f = NGATE + col_i + HEADS
    bc = _pick_col(gc, col_f)
    ic = _pick_col(gc, col_i)
    br = _pick_row(gt, col_f)
    ir = _pick_row(gt, col_i)
    if state is None:
        m0 = jnp.zeros((1, 1), F32)
    else:
        c0, n0, m0 = state
    log_d = jnp.where(_tri(lower), bc - br + ir, -jnp.inf)
    inter = bc + m0
    mt = jnp.maximum(inter, jnp.max(log_d, axis=-1, keepdims=True))
    s = qk * jnp.exp(log_d - mt)
    num = _dot(s.astype(BF16), v)
    den = jnp.sum(s, axis=-1, keepdims=True)
    if state is not None:
        ws = jnp.exp(inter - mt)
        num = num + ws * _dot_nt(q, c0.astype(BF16))
        qn = _dot_nt(q, jnp.broadcast_to(n0, (GATE_LANES, DH)).astype(BF16))
        den = den + ws * qn[:, 0:1]
    h = num / jnp.maximum(jnp.abs(den), jnp.exp(-mt))

    b_last = bc[CHUNK - 1:CHUNK, :] if lower else bc[0:1, :]
    g = b_last - bc + ic
    m_new = jnp.maximum(b_last + m0, jnp.max(g, axis=0, keepdims=True))
    w = jnp.exp(g - m_new)
    c_new = _dot_tn((v.astype(F32) * w).astype(BF16), k)
    n_new = _dot_tn(jnp.broadcast_to(w, (CHUNK, GATE_LANES)).astype(BF16), k)[0:1, :]
    if state is not None:
        dec = jnp.exp(b_last + m0 - m_new)
        c_new = dec * c0 + c_new
        n_new = dec * n0 + n_new
    return h, c_new, n_new, m_new


def _scan_prompt_kernel(q_ref, k_ref, v_ref, gc_ref, gt_ref, hf_ref, hb_ref, c_ref, n_ref, m_ref):
    gc, gt = gc_ref[...], gt_ref[...]
    for j in range(SCAN_HEADS):
        head = pl.program_id(1) * SCAN_HEADS + j
        cols = slice(j * DH, (j + 1) * DH)
        q, k, v = q_ref[:, cols], k_ref[:, cols], v_ref[:, cols]
        qk = _dot_nt(q, k)
        for d, h_ref in enumerate((hf_ref, hb_ref)):
            h, c_new, n_new, m_new = _mlstm_unit(q, k, v, qk, gc, gt, head, d == 0, None)
            h_ref[:, cols] = h.astype(BF16)
            c_ref[d, j] = c_new
            n_ref[j, d:d + 1, :] = n_new
            m_ref[j, d:d + 1, :] = jnp.broadcast_to(m_new, (1, GATE_LANES))


def _scan_prompt(q, k, v, gates, gates_t):
    hps = SCAN_HEADS
    act_spec = pl.BlockSpec((CHUNK, hps * DH), lambda b, h: (b, h))
    hshape = jax.ShapeDtypeStruct((NP_TOK, INNER), BF16)
    return pl.pallas_call(
        _scan_prompt_kernel,
        grid=(BATCH, HEADS // hps),
        in_specs=[act_spec, act_spec, act_spec,
                  pl.BlockSpec((CHUNK, GATE_LANES), lambda b, h: (b, 0)),
                  pl.BlockSpec((2 * NGATE, CHUNK), lambda b, h: (0, b))],
        out_specs=[act_spec, act_spec,
                   pl.BlockSpec((None, None, 2, hps, DH, DH), lambda b, h: (b, 0, 0, h, 0, 0)),
                   pl.BlockSpec((None, hps, 2, DH), lambda b, h: (b, h, 0, 0)),
                   pl.BlockSpec((None, hps, 2, GATE_LANES), lambda b, h: (b, h, 0, 0))],
        out_shape=[hshape, hshape,
                   jax.ShapeDtypeStruct((BATCH, 1, 2, HEADS, DH, DH), F32),
                   jax.ShapeDtypeStruct((BATCH, HEADS, 2, DH), F32),
                   jax.ShapeDtypeStruct((BATCH, HEADS, 2, GATE_LANES), F32)],
        compiler_params=_params(2),
        name="scan_prompt",
    )(q, k, v, gates, gates_t)


def _scan_sample_kernel(qf_ref, kf_ref, vf_ref, gcf_ref, gtf_ref, qb_ref, kb_ref, vb_ref, gcb_ref,
                        gtb_ref, c0_ref, n0_ref, m0_ref, hf_ref, hb_ref, c_s, n_s, m_s):
    @pl.when(pl.program_id(2) == 0)
    def _():
        for j in range(SCAN_HEADS):
            for d in range(2):
                c_s[j, d] = c0_ref[d, j]
        n_s[...] = n0_ref[...]
        m_s[...] = m0_ref[...]

    units = ((qf_ref, kf_ref, vf_ref, gcf_ref, gtf_ref, hf_ref), (qb_ref, kb_ref, vb_ref, gcb_ref, gtb_ref, hb_ref))
    for j in range(SCAN_HEADS):
        head = pl.program_id(1) * SCAN_HEADS + j
        cols = slice(j * DH, (j + 1) * DH)
        for d, (q_ref, k_ref, v_ref, gc_ref, gt_ref, h_ref) in enumerate(units):
            q, k, v = q_ref[:, cols], k_ref[:, cols], v_ref[:, cols]
            qk = _dot_nt(q, k)
            state = (c_s[j, d], n_s[j, d:d + 1, :], m_s[j, d:d + 1, 0:1])
            h, c_new, n_new, m_new = _mlstm_unit(q, k, v, qk, gc_ref[...], gt_ref[...], head, d == 0, state)
            h_ref[:, cols] = h.astype(BF16)
            c_s[j, d] = c_new
            n_s[j, d:d + 1, :] = n_new
            m_s[j, d:d + 1, :] = jnp.broadcast_to(m_new, (1, GATE_LANES))


def _scan_sample(q, k, v, gates, gates_t, c0, n0, m0, *, a):
    hps = SCAN_HEADS
    nc = DEC_SEQ // CHUNK
    base = NP_TOK // CHUNK

    def fwd(b, h, c):
        return b * nc + c

    def bwd(b, h, c):
        return b * nc + (nc - 1 - c)

    def specs(blk):
        act = pl.BlockSpec((CHUNK, hps * DH), lambda b, h, c: (base + blk(b, h, c), h))
        return [act, act, act,
                pl.BlockSpec((CHUNK, GATE_LANES), lambda b, h, c: (base + blk(b, h, c), 0)),
                pl.BlockSpec((2 * NGATE, CHUNK), lambda b, h, c: (0, base + blk(b, h, c)))]

    return pl.pallas_call(
        _scan_sample_kernel,
        grid=(DEC_BATCH, HEADS // hps, nc),
        in_specs=specs(fwd) + specs(bwd) + [
            pl.BlockSpec((None, None, 2, hps, DH, DH), lambda b, h, c: (b, a, 0, h, 0, 0)),
            pl.BlockSpec((None, hps, 2, DH), lambda b, h, c: (b, h, 0, 0)),
            pl.BlockSpec((None, hps, 2, GATE_LANES), lambda b, h, c: (b, h, 0, 0))],
        out_specs=[pl.BlockSpec((CHUNK, hps * DH), lambda b, h, c: (fwd(b, h, c), h)),
                   pl.BlockSpec((CHUNK, hps * DH), lambda b, h, c: (bwd(b, h, c), h))],
        out_shape=[jax.ShapeDtypeStruct((NS_TOK, INNER), BF16)] * 2,
        scratch_shapes=[pltpu.VMEM((hps, 2, DH, DH), F32), pltpu.VMEM((hps, 2, DH), F32),
                        pltpu.VMEM((hps, 2, GATE_LANES), F32)],
        compiler_params=_params(3),
        name="scan_sample",
    )(q, k, v, gates, gates_t, q, k, v, gates, gates_t, c0, n0, m0)


def _mlstm_out_kernel(x_ref, hfp_ref, hbp_ref, hfs_ref, hbs_ref, xc_ref, z_ref, mod_ref, ng_ref,
                      skip_ref, wd_ref, o_ref):
    is_prompt = pl.program_id(0) < NP_TOK // OUT_TM
    hf = jnp.where(is_prompt, hfp_ref[...], hfs_ref[...])
    hb = jnp.where(is_prompt, hbp_ref[...], hbs_ref[...])
    hs = hf.astype(F32) + hb.astype(F32)
    parts = []
    for h in range(HEADS):
        seg = hs[:, h * DH:(h + 1) * DH]
        mu = jnp.mean(seg, axis=-1, keepdims=True)
        cen = seg - mu
        var = jnp.mean(cen * cen, axis=-1, keepdims=True)
        parts.append(cen * lax.rsqrt(var + EPS))
    hn = jnp.concatenate(parts, axis=-1) * ng_ref[...]
    out = (hn + skip_ref[...] * xc_ref[...].astype(F32)) * _silu(z_ref[...].astype(F32))
    y = _dot(out.astype(BF16), wd_ref[...])
    o_ref[...] = x_ref[...] + mod_ref[5:6, :] * y


def _mlstm_out(x, hf_p, hb_p, hf_s, hb_s, xc, z, mods, norm_g, skip, w_down, *, layer, a):
    tm = OUT_TM
    npt = NP_TOK // tm
    act_spec = pl.BlockSpec((tm, INNER), lambda i: (i, 0))
    prompt_spec = pl.BlockSpec((tm, INNER), lambda i: (jnp.minimum(i, npt - 1), 0))
    sample_spec = pl.BlockSpec((tm, INNER), lambda i: (jnp.maximum(i - npt, 0), 0))
    return pl.pallas_call(
        _mlstm_out_kernel,
        grid=(NTOK // tm,),
        in_specs=[
            pl.BlockSpec((tm, D), lambda i: (i, 0)),
            prompt_spec, prompt_spec, sample_spec, sample_spec, act_spec, act_spec,
            pl.BlockSpec((None, None, 3 * N_SUB, D), lambda i: (layer, (i * tm) // GRP, 0, 0)),
            _resident((None, 1, INNER), lambda i: (a, 0, 0)),
            _resident((None, 1, INNER), lambda i: (a, 0, 0)),
            _resident((None, INNER, D), lambda i: (a, 0, 0)),
        ],
        out_specs=pl.BlockSpec((tm, D), lambda i: (i, 0)),
        out_shape=jax.ShapeDtypeStruct((NTOK, D), F32),
        compiler_params=_params(1),
        name="mlstm_out",
    )(x, hf_p, hb_p, hf_s, hb_s, xc, z, mods, norm_g, skip, w_down)


def _window_count(idx, win, n):
    lo = np.clip(idx - win // 2, 0, n)
    hi = np.clip(idx - win // 2 + win, 0, n)
    return (hi - lo).astype(np.float32)


def _pool_finish(x_ref, o_ref, h_g, p, g, mod_ref, wp_ref, ps_ref, rows):
    cols = slice(g * POOL_GC, (g + 1) * POOL_GC)
    y = _dot((p - h_g).astype(BF16), wp_ref[g]) * ps_ref[:, cols]
    o_ref[rows, cols] = x_ref[rows, cols] + mod_ref[5:6, cols] * y


def _pool_prompt_kernel(x_ref, h_ref, mod_ref, a_ref, cnt_ref, wp_ref, ps_ref, o_ref):
    for g in range(len(POOL_WINDOWS)):
        for s in range(POOL_TM // SEQ):
            rows = slice(s * SEQ, (s + 1) * SEQ)
            h_g = h_ref[rows, g * POOL_GC:(g + 1) * POOL_GC]
            p = _dot(a_ref[g], h_g) / cnt_ref[g]
            _pool_finish(x_ref, o_ref, h_g.astype(F32), p, g, mod_ref, wp_ref, ps_ref, rows)


def _pool_sample_kernel(xc_ref, hp_ref, hc_ref, hn_ref, mod_ref, a_ref, crow_ref, ccol_ref, wp_ref,
                        ps_ref, o_ref, hext):
    tm = POOL_TM
    tiles_per_seq = DEC_SEQ // tm
    j = pl.program_id(0) % tiles_per_seq
    hext[0:tm, :] = jnp.where(j > 0, hp_ref[...].astype(F32), 0.0)
    hext[tm:2 * tm, :] = hc_ref[...].astype(F32)
    hext[2 * tm:3 * tm, :] = jnp.where(j < tiles_per_seq - 1, hn_ref[...].astype(F32), 0.0)
    all_rows = slice(0, tm)
    for g, win in enumerate(POOL_WINDOWS):
        cols = slice(g * POOL_GC, (g + 1) * POOL_GC)
        acc = None
        for dr in range(-(win // 2), win - win // 2):
            term = hext[pl.ds(tm + GRID_W * dr, tm), cols]
            acc = term if acc is None else acc + term
        p_rows = acc / crow_ref[g]
        p = _dot(a_ref[g], p_rows.astype(BF16)) / ccol_ref[g]
        _pool_finish(xc_ref, o_ref, hext[tm:2 * tm, cols], p, g, mod_ref, wp_ref, ps_ref, all_rows)


def _band(n, win):
    t = np.arange(n)
    lo = np.clip(t - win // 2, 0, n)
    hi = np.clip(t - win // 2 + win, 0, n)
    u = np.arange(n)
    return ((u[None, :] >= lo[:, None]) & (u[None, :] < hi[:, None])).astype(np.float32)


def _pool(x, hm, mods, pool_w, pool_scale, *, layer, p):
    tm = POOL_TM
    a1 = jnp.asarray(np.stack([_band(SEQ, w) for w in POOL_WINDOWS]), BF16)
    a2 = jnp.asarray(np.stack([np.kron(np.eye(tm // GRID_W, dtype=np.float32), _band(GRID_W, w))
                               for w in POOL_WINDOWS]), BF16)
    ngroups = len(POOL_WINDOWS)
    base = NP_TOK // tm
    per_seq = DEC_SEQ // tm
    last = NTOK // tm - 1
    tok = np.arange(DEC_SEQ)
    cnt1 = jnp.asarray(np.stack([_window_count(np.arange(SEQ), w, SEQ) for w in POOL_WINDOWS])[..., None])
    crow = np.stack([_window_count(tok // GRID_W, w, DEC_SEQ // GRID_W) for w in POOL_WINDOWS])
    crow = jnp.asarray(crow.reshape(ngroups, per_seq, tm, 1).transpose(1, 0, 2, 3))
    ccol = jnp.asarray(np.stack([_window_count(tok[:tm] % GRID_W, w, GRID_W) for w in POOL_WINDOWS])[..., None])

    wp_spec = _resident((None, ngroups, POOL_GC, POOL_GC), lambda i: (p, 0, 0, 0))
    ps_spec = _resident((None, 1, D), lambda i: (p, 0, 0))
    tile = pl.BlockSpec((tm, D), lambda i: (i, 0))
    out_p = pl.pallas_call(
        _pool_prompt_kernel,
        grid=(NP_TOK // tm,),
        in_specs=[tile, tile,
                  pl.BlockSpec((None, None, 3 * N_SUB, D), lambda i: (layer, 0, 0, 0)),
                  _resident((ngroups, SEQ, SEQ), lambda i: (0, 0, 0)),
                  _resident((ngroups, SEQ, 1), lambda i: (0, 0, 0)), wp_spec, ps_spec],
        out_specs=tile,
        out_shape=jax.ShapeDtypeStruct((NP_TOK, D), F32),
        compiler_params=_params(1),
        name="pool_prompt",
    )(x, hm, mods, a1, cnt1, pool_w, pool_scale)

    out_s = pl.pallas_call(
        _pool_sample_kernel,
        grid=(NS_TOK // tm,),
        in_specs=[pl.BlockSpec((tm, D), lambda i: (base + i, 0)),
                  pl.BlockSpec((tm, D), lambda i: (jnp.maximum(base + i - 1, base), 0)),
                  pl.BlockSpec((tm, D), lambda i: (base + i, 0)),
                  pl.BlockSpec((tm, D), lambda i: (jnp.minimum(base + i + 1, last), 0)),
                  pl.BlockSpec((None, None, 3 * N_SUB, D), lambda i: (layer, 1 + i // per_seq, 0, 0)),
                  _resident((ngroups, tm, tm), lambda i: (0, 0, 0)),
                  pl.BlockSpec((None, ngroups, tm, 1), lambda i: (i % per_seq, 0, 0, 0)),
                  _resident((ngroups, tm, 1), lambda i: (0, 0, 0)), wp_spec, ps_spec],
        out_specs=tile,
        out_shape=jax.ShapeDtypeStruct((NS_TOK, D), F32),
        scratch_shapes=[pltpu.VMEM((3 * tm, D), F32)],
        compiler_params=_params(1),
        name="pool_sample",
    )(x, hm, hm, hm, mods, a2, crow, ccol, pool_w, pool_scale)
    return out_p, out_s


def _block_diag_tiles(w):
    na = w.shape[0]
    ntile = INNER // MXU_TILE
    rows = jnp.swapaxes(w, -1, -2).reshape(na, ntile, MXU_TILE, QKV_BLOCK)
    rows = jnp.tile(rows, (1, 1, 1, MXU_TILE // QKV_BLOCK))
    blk = np.arange(MXU_TILE) // QKV_BLOCK
    mask = jnp.asarray(blk[:, None] == blk[None, :])
    return jnp.where(mask, rows, 0.0).astype(BF16)


def kernel(x_prompt, x_sample, state_C, state_n, state_m, c, c_ctx, ada_w, ada_b, norm_g, ffn_w_in,
           ffn_w_out, mlstm_w_up, mlstm_conv_w, mlstm_conv_b, mlstm_w_q, mlstm_w_k, mlstm_w_v,
           mlstm_w_gate, mlstm_b_gate, mlstm_norm_g, mlstm_skip, mlstm_w_down, pool_w, pool_scale,
           final_g):
    na = mlstm_w_up.shape[0]
    xs = (x_prompt.reshape(NP_TOK, D), x_sample.reshape(NS_TOK, D))
    cond8 = jnp.concatenate([c_ctx[None, :], c, jnp.zeros((8 - 1 - DEC_BATCH, D), F32)], axis=0)
    mods = _adaln(cond8, ada_w, ada_b)
    mods = jnp.transpose(mods[:, :, :1 + DEC_BATCH], (0, 2, 1, 3))

    w_in = ffn_w_in
    w_out = ffn_w_out
    fg = final_g.reshape(1, D)
    wup = mlstm_w_up.astype(BF16)
    wqk = jnp.concatenate([_block_diag_tiles(mlstm_w_q), _block_diag_tiles(mlstm_w_k)], axis=-1)
    wv = _block_diag_tiles(mlstm_w_v)
    lane_pad = ((0, 0), (0, 0), (0, GATE_LANES - 2 * NGATE))
    wg = jnp.pad(jnp.tile(mlstm_w_gate.reshape(na, 3 * INNER, NGATE), (1, 1, 2)), lane_pad).astype(BF16)
    bg = jnp.pad(jnp.tile(mlstm_b_gate.reshape(na, 1, NGATE), (1, 1, 2)), lane_pad)
    conv_b = mlstm_conv_b.reshape(na, 1, INNER)
    mng = mlstm_norm_g.reshape(na, 1, INNER)
    skip = mlstm_skip.reshape(na, 1, INNER)
    wdown = mlstm_w_down.astype(BF16)
    pw = pool_w.astype(BF16)
    ps = pool_scale.reshape(-1, 1, D)

    new_c, new_n, new_m = [], [], []
    for l in range(DEPTH):
        x, hm = _ffn(xs if l == 0 else (x,), mods, norm_g, w_in, w_out, fg, layer=l, sub=0, which=0)
        if l % 2 == 0:
            a = l // 2
            q, k, v, xc, z, gates = _mlstm_pre(hm, wup, mlstm_conv_w, conv_b, wqk, wv, wg, bg, a=a)
            gates_t = jnp.transpose(gates[:, :2 * NGATE])
            hf_p, hb_p, c_new, n_new, m_new = _scan_prompt(q, k, v, gates, gates_t)
            n0 = jnp.transpose(state_n[:, a], (0, 2, 1, 3))
            m0 = jnp.broadcast_to(jnp.transpose(state_m[:, a], (0, 2, 1))[..., None],
                                  (DEC_BATCH, HEADS, 2, GATE_LANES))
            hf_s, hb_s = _scan_sample(q, k, v, gates, gates_t, state_C, n0, m0, a=a)
            x = _mlstm_out(x, hf_p, hb_p, hf_s, hb_s, xc, z, mods, mng, skip, wdown, layer=l, a=a)
            new_c.append(c_new)
            new_n.append(jnp.transpose(n_new, (0, 2, 1, 3))[:, None])
            new_m.append(jnp.transpose(m_new[..., 0], (0, 2, 1))[:, None])
            x = (x,)
        else:
            x = _pool(x, hm, mods, pw, ps, layer=l, p=l // 2)
        x = _ffn(x, mods, norm_g, w_in, w_out, fg, layer=l, sub=2, which=1, final=(l == DEPTH - 1))
    y_prompt = x[0].reshape(BATCH, SEQ, D)
    y_sample = x[1].reshape(DEC_BATCH, DEC_SEQ, D)
    return (y_prompt, y_sample, jnp.concatenate(new_c, axis=1), jnp.concatenate(new_n, axis=1),
            jnp.concatenate(new_m, axis=1))
```

```python
import functools
import math

import numpy as np
import jax
import jax.numpy as jnp
from jax import lax
from jax.experimental import pallas as pl
from jax.experimental.pallas import tpu as pltpu

F32 = jnp.float32
BF16 = jnp.bfloat16

D = 1024
BATCH = 16
SEQ = 256
DEPTH = 2
DEC_BATCH = 2
DEC_SEQ = 4096
GRID_W = 64
N_SUB = 3
INNER = 2 * D
HEADS = 4
DH = INNER // HEADS
QKV_BLOCK = 4
CONV_K = 5
POOL_WINDOWS = (2, 4, 8, 16)
POOL_GC = D // len(POOL_WINDOWS)
D_FF = 2816
EPS = 1e-6

NP_TOK = BATCH * SEQ
NS_TOK = DEC_BATCH * DEC_SEQ
NTOK = NP_TOK + NS_TOK
GRP = 4096
assert NP_TOK == GRP and DEC_SEQ == GRP

MXU_TILE = 256
GATE_LANES = 128
CHUNK = 256
HALO = 16

NGATE = 2 * 2 * HEADS

SCAN_HEADS = HEADS

FFN_TM = 512
PRE_TM = 2 * CHUNK
OUT_TM = 512
POOL_TM = 512
VMEM_LIMIT = 56 * 1024 * 1024


def _silu(x):
    return x * jax.nn.sigmoid(x)


def _dot(a, b):
    return jnp.dot(a, b, preferred_element_type=F32)


def _dot_nt(a, b):
    return lax.dot_general(a, b, (((1,), (1,)), ((), ())), preferred_element_type=F32)


def _dot_tn(a, b):
    return lax.dot_general(a, b, (((0,), (0,)), ((), ())), preferred_element_type=F32)


def _resident(shape, index_map):
    return pl.BlockSpec(shape, index_map, pipeline_mode=pl.Buffered(1))


def _params(n_axes):
    return pltpu.CompilerParams(
        dimension_semantics=("arbitrary",) * n_axes, vmem_limit_bytes=VMEM_LIMIT)


def _modulate(x, g, mod_ref, sub):
    ms = jnp.mean(x * x, axis=-1, keepdims=True)
    y = x * lax.rsqrt(ms + EPS) * g
    return y * (1.0 + mod_ref[3 * sub + 1:3 * sub + 2, :]) + mod_ref[3 * sub:3 * sub + 1, :]


def _split_dot(a, x, terms, left=True):
    parts = []
    r = x
    for _ in range(terms):
        p = r.astype(BF16)
        parts.append(p)
        r = r - p.astype(F32)
    acc = None
    for p in reversed(parts):
        t = _dot(a, p) if left else _dot(p, a)
        acc = t if acc is None else acc + t
    return acc


def _adaln_kernel(cond_ref, w_ref, b_ref, o_ref):
    s = _silu(cond_ref[...]).astype(BF16)
    o_ref[...] = _dot(s, w_ref[...].astype(BF16)) + b_ref[...]


def _adaln(cond8, ada_w, ada_b):
    return pl.pallas_call(
        _adaln_kernel,
        grid=(DEPTH, 3 * N_SUB),
        in_specs=[
            pl.BlockSpec((8, D), lambda l, j: (0, 0)),
            pl.BlockSpec((None, D, D), lambda l, j: (l, 0, j)),
            pl.BlockSpec((None, None, 1, D), lambda l, j: (l, j, 0, 0)),
        ],
        out_specs=pl.BlockSpec((None, None, 8, D), lambda l, j: (l, j, 0, 0)),
        out_shape=jax.ShapeDtypeStruct((DEPTH, 3 * N_SUB, 8, D), F32),
        compiler_params=_params(2),
        name="adaln",
    )(cond8, ada_w, ada_b.reshape(DEPTH, 3 * N_SUB, 1, D))


def _ffn_kernel(*refs, layer, sub, split_in, emit_h, final):
    n_x = 2 if split_in else 1
    x_refs, (mod_ref, g_ref, win_ref, wout_ref, fg_ref), o_refs = refs[:n_x], refs[n_x:n_x + 5], refs[n_x + 5:]
    is_prompt = pl.program_id(0) < NP_TOK // FFN_TM
    x = jnp.where(is_prompt, x_refs[0][...], x_refs[1][...]) if split_in else x_refs[0][...]
    h = _modulate(x, g_ref[layer, sub:sub + 1, :], mod_ref, sub).astype(BF16)
    acc = jnp.zeros(x.shape, F32)
    for c in range(D_FF // MXU_TILE):
        lo = c * MXU_TILE
        a = _dot(h, win_ref[:, lo:lo + MXU_TILE].astype(BF16))
        b = _dot(h, win_ref[:, D_FF + lo:D_FF + lo + MXU_TILE].astype(BF16))
        u = (_silu(a) * b).astype(BF16)
        acc = acc + _dot(u, wout_ref[lo:lo + MXU_TILE, :].astype(BF16))
    out = x + (0.5 * mod_ref[3 * sub + 2:3 * sub + 3, :]) * acc
    if not final:
        o_refs[0][...] = out
        if emit_h:
            o_refs[1][...] = _modulate(out, g_ref[layer, 1:2, :], mod_ref, 1).astype(BF16)
        return
    ms = jnp.mean(out * out, axis=-1, keepdims=True)
    out = out * lax.rsqrt(ms + EPS) * fg_ref[...]

    @pl.when(is_prompt)
    def _():
        o_refs[0][...] = out

    @pl.when(jnp.logical_not(is_prompt))
    def _():
        o_refs[1][...] = out


def _ffn(xs, mods, norm_g, w_in, w_out, final_g, *, layer, sub, which, final=False):
    tm = FFN_TM
    npt = NP_TOK // tm
    prompt_spec = pl.BlockSpec((tm, D), lambda i: (jnp.minimum(i, npt - 1), 0))
    sample_spec = pl.BlockSpec((tm, D), lambda i: (jnp.maximum(i - npt, 0), 0))
    whole_spec = pl.BlockSpec((tm, D), lambda i: (i, 0))
    split_in = len(xs) == 2
    emit_h = sub == 0
    if final:
        out_specs = [prompt_spec, sample_spec]
        out_shape = [jax.ShapeDtypeStruct((NP_TOK, D), F32), jax.ShapeDtypeStruct((NS_TOK, D), F32)]
    elif emit_h:
        out_specs = [whole_spec, whole_spec]
        out_shape = [jax.ShapeDtypeStruct((NTOK, D), F32), jax.ShapeDtypeStruct((NTOK, D), BF16)]
    else:
        out_specs = whole_spec
        out_shape = jax.ShapeDtypeStruct((NTOK, D), F32)
    return pl.pallas_call(
        functools.partial(_ffn_kernel, layer=layer, sub=sub, split_in=split_in, emit_h=emit_h,
                          final=final),
        grid=(NTOK // tm,),
        in_specs=([prompt_spec, sample_spec] if split_in else [whole_spec]) + [
            pl.BlockSpec((None, None, 3 * N_SUB, D), lambda i: (layer, (i * tm) // GRP, 0, 0)),
            _resident((DEPTH, N_SUB, D), lambda i: (0, 0, 0)),
            _resident((None, None, D, 2 * D_FF), lambda i: (layer, which, 0, 0)),
            _resident((None, None, D_FF, D), lambda i: (layer, which, 0, 0)),
            _resident((1, D), lambda i: (0, 0)),
        ],
        out_specs=out_specs,
        out_shape=out_shape,
        compiler_params=_params(1),
        name=f"ffn_l{layer}_{which}",
    )(*xs, mods, norm_g, w_in, w_out, final_g)


def _mlstm_pre_kernel(hp_ref, hc_ref, hn_ref, wup_ref, cw_ref, cb_ref, wqk_ref, wv_ref, wg_ref,
                      bg_ref, q_ref, k_ref, v_ref, xco_ref, z_ref, gate_ref, xm_s, acc_s):
    tm = PRE_TM
    nrow = tm + 2 * HALO
    i = pl.program_id(0)
    he = jnp.concatenate([hp_ref[...], hc_ref[...], hn_ref[...]], axis=0)
    xm_s[...] = _dot(he, wup_ref[:, :INNER])
    z_ref[...] = _dot(hc_ref[...], wup_ref[:, INNER:]).astype(BF16)

    is_prompt = i < NP_TOK // tm
    seqlen = jnp.where(is_prompt, SEQ, DEC_SEQ)
    starts_seq = ((i * tm) & (seqlen - 1)) == 0
    ends_seq = ((i * tm + tm) & (seqlen - 1)) == 0
    xm_s[0:HALO, :] = jnp.where(starts_seq, 0.0, xm_s[0:HALO, :])
    xm_s[HALO + tm:, :] = jnp.where(ends_seq, 0.0, xm_s[HALO + tm:, :])
    xm_all = xm_s[...]
    taps = []
    for j in range(CONV_K):
        off = j - CONV_K // 2
        taps.append(xm_all if off == 0 else pltpu.roll(xm_all, (nrow - off) % nrow, axis=0))

    def conv_rows(lo, hi, at_seq_end):
        acc = cb_ref[...]
        for j in range(CONV_K):
            off = j - CONV_K // 2
            term = taps[j][HALO + lo:HALO + hi]
            if at_seq_end and off != 0:
                pos = (i * tm + lo + lax.broadcasted_iota(jnp.int32, (hi - lo, 1), 0)) & (seqlen - 1)
                inside = (pos + off >= 0) if off < 0 else (pos + off < seqlen)
                term = jnp.where(inside, term, 0.0)
            acc = acc + term * cw_ref[j]
        return acc

    edge = 8
    lo = 0
    for b in range(SEQ, tm, SEQ):
        acc_s[lo:b - edge, :] = conv_rows(lo, b - edge, False)
        acc_s[b - edge:b + edge, :] = conv_rows(b - edge, b + edge, True)
        lo = b + edge
    acc_s[lo:tm, :] = conv_rows(lo, tm, False)
    xc = _silu(acc_s[...])
    xco_ref[...] = xc.astype(BF16)

    for t in range(INNER // MXU_TILE):
        cols = slice(t * MXU_TILE, (t + 1) * MXU_TILE)
        qk = _dot(xc[:, cols].astype(BF16), wqk_ref[t])
        q_ref[:, cols] = qk[:, :MXU_TILE].astype(BF16)
        k_ref[:, cols] = (qk[:, MXU_TILE:] / math.sqrt(DH)).astype(BF16)
        xm = xm_s[pl.ds(HALO, tm), cols].astype(BF16)
        v_ref[:, cols] = _dot(xm, wv_ref[t]).astype(BF16)

    g = (_dot(q_ref[...], wg_ref[0:INNER, :]) + _dot(k_ref[...], wg_ref[INNER:2 * INNER, :])
         + _dot(v_ref[...], wg_ref[2 * INNER:, :]) + bg_ref[...])
    lane = lax.broadcasted_iota(jnp.int32, g.shape, 1)
    logsig = jnp.minimum(g, 0.0) - jnp.log1p(jnp.exp(-jnp.abs(g)))
    gl = jnp.where(lane < 2 * NGATE, jnp.where((lane & HEADS) != 0, logsig, g), 0.0)
    tri_prefix = jnp.where(_tri(True), 1.0, 0.0).astype(BF16)
    tri_suffix = jnp.where(_tri(False), 1.0, 0.0).astype(BF16)
    lane_c = lax.broadcasted_iota(jnp.int32, (CHUNK, GATE_LANES), 1)
    for c in range(tm // CHUNK):
        rows = slice(c * CHUNK, (c + 1) * CHUNK)
        gl_c = gl[rows]
        cum = jnp.where((lane_c & (2 * HEADS)) == 0, _split_dot(tri_prefix, gl_c, 3), _split_dot(tri_suffix, gl_c, 3))
        gate_ref[rows, :] = jnp.where(lane_c < NGATE, gl_c, cum)


def _mlstm_pre(hm, wup, conv_w, conv_b, wqk, wv, wg, bg, *, a):
    tm = PRE_TM
    nhalo = NTOK // HALO
    act = jax.ShapeDtypeStruct((NTOK, INNER), BF16)
    act_spec = pl.BlockSpec((tm, INNER), lambda i: (i, 0))
    return pl.pallas_call(
        _mlstm_pre_kernel,
        grid=(NTOK // tm,),
        in_specs=[
            pl.BlockSpec((HALO, D), lambda i: (jnp.maximum(i * (tm // HALO) - 1, 0), 0)),
            pl.BlockSpec((tm, D), lambda i: (i, 0)),
            pl.BlockSpec((HALO, D), lambda i: (jnp.minimum((i + 1) * (tm // HALO), nhalo - 1), 0)),
            _resident((None, D, 2 * INNER), lambda i: (a, 0, 0)),
            _resident((None, CONV_K, 1, INNER), lambda i: (a, 0, 0, 0)),
            _resident((None, 1, INNER), lambda i: (a, 0, 0)),
            _resident((None, INNER // MXU_TILE, MXU_TILE, 2 * MXU_TILE), lambda i: (a, 0, 0, 0)),
            _resident((None, INNER // MXU_TILE, MXU_TILE, MXU_TILE), lambda i: (a, 0, 0, 0)),
            _resident((None, 3 * INNER, GATE_LANES), lambda i: (a, 0, 0)),
            _resident((None, 1, GATE_LANES), lambda i: (a, 0, 0)),
        ],
        out_specs=[act_spec] * 5 + [pl.BlockSpec((tm, GATE_LANES), lambda i: (i, 0))],
        out_shape=[act] * 5 + [jax.ShapeDtypeStruct((NTOK, GATE_LANES), F32)],
        scratch_shapes=[pltpu.VMEM((tm + 2 * HALO, INNER), F32), pltpu.VMEM((tm, INNER), F32)],
        compiler_params=_params(1),
        name="mlstm_pre",
    )(hm, hm, hm, wup, conv_w, conv_b, wqk, wv, wg, bg)


def _tri(lower):
    r = lax.broadcasted_iota(jnp.int32, (CHUNK, CHUNK), 0)
    c = lax.broadcasted_iota(jnp.int32, (CHUNK, CHUNK), 1)
    return (c <= r) if lower else (c >= r)


def _pick_col(x, col):
    lane = lax.broadcasted_iota(jnp.int32, x.shape, 1)
    return jnp.sum(jnp.where(lane == col, x, 0.0), axis=1, keepdims=True)


def _pick_row(x, row):
    return x[row:row + 1, :]


def _mlstm_unit(q, k, v, qk, gc, gt, head, lower, state):
    d = 0 if lower else 1
    col_i = d * 2 * HEADS + head
    col_f = NGATE + col_i + HEADS
    bc = _pick_col(gc, col_f)
    ic = _pick_col(gc, col_i)
    br = _pick_row(gt, col_f)
    ir = _pick_row(gt, col_i)
    if state is None:
        m0 = jnp.zeros((1, 1), F32)
    else:
        c0, n0, m0 = state
    log_d = jnp.where(_tri(lower), bc - br + ir, -jnp.inf)
    inter = bc + m0
    mt = jnp.maximum(inter, jnp.max(log_d, axis=-1, keepdims=True))
    s = qk * jnp.exp(log_d - mt)
    num = _dot(s.astype(BF16), v)
    den = jnp.sum(s, axis=-1, keepdims=True)
    if state is not None:
        ws = jnp.exp(inter - mt)
        num = num + ws * _dot_nt(q, c0.astype(BF16))
        qn = _dot_nt(q, jnp.broadcast_to(n0, (GATE_LANES, DH)).astype(BF16))
        den = den + ws * qn[:, 0:1]
    h = num / jnp.maximum(jnp.abs(den), jnp.exp(-mt))

    b_last = bc[CHUNK - 1:CHUNK, :] if lower else bc[0:1, :]
    g = b_last - bc + ic
    m_new = jnp.maximum(b_last + m0, jnp.max(g, axis=0, keepdims=True))
    w = jnp.exp(g - m_new)
    c_new = _dot_tn((v.astype(F32) * w).astype(BF16), k)
    n_new = _dot_tn(jnp.broadcast_to(w, (CHUNK, GATE_LANES)).astype(BF16), k)[0:1, :]
    if state is not None:
        dec = jnp.exp(b_last + m0 - m_new)
        c_new = dec * c0 + c_new
        n_new = dec * n0 + n_new
    return h, c_new, n_new, m_new


def _scan_prompt_kernel(q_ref, k_ref, v_ref, gc_ref, gt_ref, hf_ref, hb_ref, c_ref, n_ref, m_ref):
    gc, gt = gc_ref[...], gt_ref[...]
    for j in range(SCAN_HEADS):
        head = j
        cols = slice(j * DH, (j + 1) * DH)
        q, k, v = q_ref[:, cols], k_ref[:, cols], v_ref[:, cols]
        qk = _dot_nt(q, k)
        for d, h_ref in enumerate((hf_ref, hb_ref)):
            h, c_new, n_new, m_new = _mlstm_unit(q, k, v, qk, gc, gt, head, d == 0, None)
            h_ref[:, cols] = h.astype(BF16)
            c_ref[d, j] = c_new
            n_ref[j, d:d + 1, :] = n_new
            m_ref[j, d:d + 1, :] = jnp.broadcast_to(m_new, (1, GATE_LANES))


def _scan_prompt(q, k, v, gates, gates_t):
    hps = SCAN_HEADS
    act_spec = pl.BlockSpec((CHUNK, hps * DH), lambda b, h: (b, h))
    hshape = jax.ShapeDtypeStruct((NP_TOK, INNER), BF16)
    return pl.pallas_call(
        _scan_prompt_kernel,
        grid=(BATCH, HEADS // hps),
        in_specs=[act_spec, act_spec, act_spec,
                  pl.BlockSpec((CHUNK, GATE_LANES), lambda b, h: (b, 0)),
                  pl.BlockSpec((2 * NGATE, CHUNK), lambda b, h: (0, b))],
        out_specs=[act_spec, act_spec,
                   pl.BlockSpec((None, None, 2, hps, DH, DH), lambda b, h: (b, 0, 0, h, 0, 0)),
                   pl.BlockSpec((None, hps, 2, DH), lambda b, h: (b, h, 0, 0)),
                   pl.BlockSpec((None, hps, 2, GATE_LANES), lambda b, h: (b, h, 0, 0))],
        out_shape=[hshape, hshape,
                   jax.ShapeDtypeStruct((BATCH, 1, 2, HEADS, DH, DH), F32),
                   jax.ShapeDtypeStruct((BATCH, HEADS, 2, DH), F32),
                   jax.ShapeDtypeStruct((BATCH, HEADS, 2, GATE_LANES), F32)],
        compiler_params=_params(2),
        name="scan_prompt",
    )(q, k, v, gates, gates_t)


def _scan_sample_kernel(qf_ref, kf_ref, vf_ref, gcf_ref, gtf_ref, qb_ref, kb_ref, vb_ref, gcb_ref,
                        gtb_ref, c0_ref, n0_ref, m0_ref, hf_ref, hb_ref, c_s, n_s, m_s):
    @pl.when(pl.program_id(2) == 0)
    def _():
        for j in range(SCAN_HEADS):
            for d in range(2):
                c_s[j, d] = c0_ref[d, j]
        n_s[...] = n0_ref[...]
        m_s[...] = m0_ref[...]

    units = ((qf_ref, kf_ref, vf_ref, gcf_ref, gtf_ref, hf_ref), (qb_ref, kb_ref, vb_ref, gcb_ref, gtb_ref, hb_ref))
    for j in range(SCAN_HEADS):
        head = j
        cols = slice(j * DH, (j + 1) * DH)
        for d, (q_ref, k_ref, v_ref, gc_ref, gt_ref, h_ref) in enumerate(units):
            q, k, v = q_ref[:, cols], k_ref[:, cols], v_ref[:, cols]
            qk = _dot_nt(q, k)
            state = (c_s[j, d], n_s[j, d:d + 1, :], m_s[j, d:d + 1, 0:1])
            h, c_new, n_new, m_new = _mlstm_unit(q, k, v, qk, gc_ref[...], gt_ref[...], head, d == 0, state)
            h_ref[:, cols] = h.astype(BF16)
            c_s[j, d] = c_new
            n_s[j, d:d + 1, :] = n_new
            m_s[j, d:d + 1, :] = jnp.broadcast_to(m_new, (1, GATE_LANES))


def _scan_sample(q, k, v, gates, gates_t, c0, n0, m0, *, a):
    hps = SCAN_HEADS
    nc = DEC_SEQ // CHUNK
    base = NP_TOK // CHUNK

    def fwd(b, h, c):
        return b * nc + c

    def bwd(b, h, c):
        return b * nc + (nc - 1 - c)

    def specs(blk):
        act = pl.BlockSpec((CHUNK, hps * DH), lambda b, h, c: (base + blk(b, h, c), h))
        return [act, act, act,
                pl.BlockSpec((CHUNK, GATE_LANES), lambda b, h, c: (base + blk(b, h, c), 0)),
                pl.BlockSpec((2 * NGATE, CHUNK), lambda b, h, c: (0, base + blk(b, h, c)))]

    return pl.pallas_call(
        _scan_sample_kernel,
        grid=(DEC_BATCH, HEADS // hps, nc),
        in_specs=specs(fwd) + specs(bwd) + [
            pl.BlockSpec((None, None, 2, hps, DH, DH), lambda b, h, c: (b, a, 0, h, 0, 0)),
            pl.BlockSpec((None, hps, 2, DH), lambda b, h, c: (b, h, 0, 0)),
            pl.BlockSpec((None, hps, 2, GATE_LANES), lambda b, h, c: (b, h, 0, 0))],
        out_specs=[pl.BlockSpec((CHUNK, hps * DH), lambda b, h, c: (fwd(b, h, c), h)),
                   pl.BlockSpec((CHUNK, hps * DH), lambda b, h, c: (bwd(b, h, c), h))],
        out_shape=[jax.ShapeDtypeStruct((NS_TOK, INNER), BF16)] * 2,
        scratch_shapes=[pltpu.VMEM((hps, 2, DH, DH), F32), pltpu.VMEM((hps, 2, DH), F32),
                        pltpu.VMEM((hps, 2, GATE_LANES), F32)],
        compiler_params=_params(3),
        name="scan_sample",
    )(q, k, v, gates, gates_t, q, k, v, gates, gates_t, c0, n0, m0)


def _mlstm_out_kernel(x_ref, hfp_ref, hbp_ref, hfs_ref, hbs_ref, xc_ref, z_ref, mod_ref, ng_ref,
                      skip_ref, wd_ref, o_ref):
    is_prompt = pl.program_id(0) < NP_TOK // OUT_TM
    y = None
    for h in range(HEADS):
        cols = slice(h * DH, (h + 1) * DH)
        hf = jnp.where(is_prompt, hfp_ref[:, cols], hfs_ref[:, cols])
        hb = jnp.where(is_prompt, hbp_ref[:, cols], hbs_ref[:, cols])
        seg = hf.astype(F32) + hb.astype(F32)
        mu = jnp.mean(seg, axis=-1, keepdims=True)
        cen = seg - mu
        var = jnp.mean(cen * cen, axis=-1, keepdims=True)
        hn = cen * lax.rsqrt(var + EPS) * ng_ref[:, cols]
        out = (hn + skip_ref[:, cols] * xc_ref[:, cols].astype(F32)) * _silu(z_ref[:, cols].astype(F32))
        part = _dot(out.astype(BF16), wd_ref[cols, :])
        y = part if y is None else y + part
    o_ref[...] = x_ref[...] + mod_ref[5:6, :] * y


def _mlstm_out(x, hf_p, hb_p, hf_s, hb_s, xc, z, mods, norm_g, skip, w_down, *, layer, a):
    tm = OUT_TM
    npt = NP_TOK // tm
    act_spec = pl.BlockSpec((tm, INNER), lambda i: (i, 0))
    prompt_spec = pl.BlockSpec((tm, INNER), lambda i: (jnp.minimum(i, npt - 1), 0))
    sample_spec = pl.BlockSpec((tm, INNER), lambda i: (jnp.maximum(i - npt, 0), 0))
    return pl.pallas_call(
        _mlstm_out_kernel,
        grid=(NTOK // tm,),
        in_specs=[
            pl.BlockSpec((tm, D), lambda i: (i, 0)),
            prompt_spec, prompt_spec, sample_spec, sample_spec, act_spec, act_spec,
            pl.BlockSpec((None, None, 3 * N_SUB, D), lambda i: (layer, (i * tm) // GRP, 0, 0)),
            _resident((None, 1, INNER), lambda i: (a, 0, 0)),
            _resident((None, 1, INNER), lambda i: (a, 0, 0)),
            _resident((None, INNER, D), lambda i: (a, 0, 0)),
        ],
        out_specs=pl.BlockSpec((tm, D), lambda i: (i, 0)),
        out_shape=jax.ShapeDtypeStruct((NTOK, D), F32),
        compiler_params=_params(1),
        name="mlstm_out",
    )(x, hf_p, hb_p, hf_s, hb_s, xc, z, mods, norm_g, skip, w_down)


def _window_count(idx, win, n):
    lo = np.clip(idx - win // 2, 0, n)
    hi = np.clip(idx - win // 2 + win, 0, n)
    return (hi - lo).astype(np.float32)


def _pool_finish(x_ref, o_ref, h_g, p, g, mod_ref, wp_ref, ps_ref, rows):
    cols = slice(g * POOL_GC, (g + 1) * POOL_GC)
    y = _dot((p - h_g).astype(BF16), wp_ref[g]) * ps_ref[:, cols]
    o_ref[rows, cols] = x_ref[rows, cols] + mod_ref[5:6, cols] * y


def _pool_prompt_kernel(x_ref, h_ref, mod_ref, a_ref, cnt_ref, wp_ref, ps_ref, o_ref):
    for g in range(len(POOL_WINDOWS)):
        for s in range(POOL_TM // SEQ):
            rows = slice(s * SEQ, (s + 1) * SEQ)
            h_g = h_ref[rows, g * POOL_GC:(g + 1) * POOL_GC]
            p = _dot(a_ref[g], h_g) / cnt_ref[g]
            _pool_finish(x_ref, o_ref, h_g.astype(F32), p, g, mod_ref, wp_ref, ps_ref, rows)


def _pool_sample_kernel(xc_ref, hp_ref, hc_ref, hn_ref, mod_ref, a_ref, crow_ref, ccol_ref, wp_ref,
                        ps_ref, o_ref, hext):
    tm = POOL_TM
    tiles_per_seq = DEC_SEQ // tm
    j = pl.program_id(0) % tiles_per_seq
    hext[0:tm, :] = jnp.where(j > 0, hp_ref[...].astype(F32), 0.0)
    hext[tm:2 * tm, :] = hc_ref[...].astype(F32)
    hext[2 * tm:3 * tm, :] = jnp.where(j < tiles_per_seq - 1, hn_ref[...].astype(F32), 0.0)
    all_rows = slice(0, tm)
    for g, win in enumerate(POOL_WINDOWS):
        cols = slice(g * POOL_GC, (g + 1) * POOL_GC)
        acc = None
        for dr in range(-(win // 2), win - win // 2):
            term = hext[pl.ds(tm + GRID_W * dr, tm), cols]
            acc = term if acc is None else acc + term
        p_rows = acc / crow_ref[g]
        p = _dot(a_ref[g], p_rows.astype(BF16)) / ccol_ref[g]
        _pool_finish(xc_ref, o_ref, hext[tm:2 * tm, cols], p, g, mod_ref, wp_ref, ps_ref, all_rows)


def _band(n, win):
    t = np.arange(n)
    lo = np.clip(t - win // 2, 0, n)
    hi = np.clip(t - win // 2 + win, 0, n)
    u = np.arange(n)
    return ((u[None, :] >= lo[:, None]) & (u[None, :] < hi[:, None])).astype(np.float32)


def _pool(x, hm, mods, pool_w, pool_scale, *, layer, p):
    tm = POOL_TM
    a1 = jnp.asarray(np.stack([_band(SEQ, w) for w in POOL_WINDOWS]), BF16)
    a2 = jnp.asarray(np.stack([np.kron(np.eye(tm // GRID_W, dtype=np.float32), _band(GRID_W, w))
                               for w in POOL_WINDOWS]), BF16)
    ngroups = len(POOL_WINDOWS)
    base = NP_TOK // tm
    per_seq = DEC_SEQ // tm
    last = NTOK // tm - 1
    tok = np.arange(DEC_SEQ)
    cnt1 = jnp.asarray(np.stack([_window_count(np.arange(SEQ), w, SEQ) for w in POOL_WINDOWS])[..., None])
    crow = np.stack([_window_count(tok // GRID_W, w, DEC_SEQ // GRID_W) for w in POOL_WINDOWS])
    crow = jnp.asarray(crow.reshape(ngroups, per_seq, tm, 1).transpose(1, 0, 2, 3))
    ccol = jnp.asarray(np.stack([_window_count(tok[:tm] % GRID_W, w, GRID_W) for w in POOL_WINDOWS])[..., None])

    wp_spec = _resident((None, ngroups, POOL_GC, POOL_GC), lambda i: (p, 0, 0, 0))
    ps_spec = _resident((None, 1, D), lambda i: (p, 0, 0))
    tile = pl.BlockSpec((tm, D), lambda i: (i, 0))
    out_p = pl.pallas_call(
        _pool_prompt_kernel,
        grid=(NP_TOK // tm,),
        in_specs=[tile, tile,
                  pl.BlockSpec((None, None, 3 * N_SUB, D), lambda i: (layer, 0, 0, 0)),
                  _resident((ngroups, SEQ, SEQ), lambda i: (0, 0, 0)),
                  _resident((ngroups, SEQ, 1), lambda i: (0, 0, 0)), wp_spec, ps_spec],
        out_specs=tile,
        out_shape=jax.ShapeDtypeStruct((NP_TOK, D), F32),
        compiler_params=_params(1),
        name="pool_prompt",
    )(x, hm, mods, a1, cnt1, pool_w, pool_scale)

    out_s = pl.pallas_call(
        _pool_sample_kernel,
        grid=(NS_TOK // tm,),
        in_specs=[pl.BlockSpec((tm, D), lambda i: (base + i, 0)),
                  pl.BlockSpec((tm, D), lambda i: (jnp.maximum(base + i - 1, base), 0)),
                  pl.BlockSpec((tm, D), lambda i: (base + i, 0)),
                  pl.BlockSpec((tm, D), lambda i: (jnp.minimum(base + i + 1, last), 0)),
                  pl.BlockSpec((None, None, 3 * N_SUB, D), lambda i: (layer, 1 + i // per_seq, 0, 0)),
                  _resident((ngroups, tm, tm), lambda i: (0, 0, 0)),
                  pl.BlockSpec((None, ngroups, tm, 1), lambda i: (i % per_seq, 0, 0, 0)),
                  _resident((ngroups, tm, 1), lambda i: (0, 0, 0)), wp_spec, ps_spec],
        out_specs=tile,
        out_shape=jax.ShapeDtypeStruct((NS_TOK, D), F32),
        scratch_shapes=[pltpu.VMEM((3 * tm, D), F32)],
        compiler_params=_params(1),
        name="pool_sample",
    )(x, hm, hm, hm, mods, a2, crow, ccol, pool_w, pool_scale)
    return out_p, out_s


def _block_diag_tiles(w):
    na = w.shape[0]
    ntile = INNER // MXU_TILE
    rows = jnp.swapaxes(w, -1, -2).reshape(na, ntile, MXU_TILE, QKV_BLOCK)
    rows = jnp.tile(rows, (1, 1, 1, MXU_TILE // QKV_BLOCK))
    blk = np.arange(MXU_TILE) // QKV_BLOCK
    mask = jnp.asarray(blk[:, None] == blk[None, :])
    return jnp.where(mask, rows, 0.0).astype(BF16)


def kernel(x_prompt, x_sample, state_C, state_n, state_m, c, c_ctx, ada_w, ada_b, norm_g, ffn_w_in,
           ffn_w_out, mlstm_w_up, mlstm_conv_w, mlstm_conv_b, mlstm_w_q, mlstm_w_k, mlstm_w_v,
           mlstm_w_gate, mlstm_b_gate, mlstm_norm_g, mlstm_skip, mlstm_w_down, pool_w, pool_scale,
           final_g):
    na = mlstm_w_up.shape[0]
    xs = (x_prompt.reshape(NP_TOK, D), x_sample.reshape(NS_TOK, D))
    cond8 = jnp.concatenate([c_ctx[None, :], c, jnp.zeros((8 - 1 - DEC_BATCH, D), F32)], axis=0)
    mods = _adaln(cond8, ada_w, ada_b)
    mods = jnp.transpose(mods[:, :, :1 + DEC_BATCH], (0, 2, 1, 3))

    w_in = ffn_w_in
    w_out = ffn_w_out
    fg = final_g.reshape(1, D)
    wup = mlstm_w_up.astype(BF16)
    wqk = jnp.concatenate([_block_diag_tiles(mlstm_w_q), _block_diag_tiles(mlstm_w_k)], axis=-1)
    wv = _block_diag_tiles(mlstm_w_v)
    lane_pad = ((0, 0), (0, 0), (0, GATE_LANES - 2 * NGATE))
    wg = jnp.pad(jnp.tile(mlstm_w_gate.reshape(na, 3 * INNER, NGATE), (1, 1, 2)), lane_pad).astype(BF16)
    bg = jnp.pad(jnp.tile(mlstm_b_gate.reshape(na, 1, NGATE), (1, 1, 2)), lane_pad)
    conv_b = mlstm_conv_b.reshape(na, 1, INNER)
    mng = mlstm_norm_g.reshape(na, 1, INNER)
    skip = mlstm_skip.reshape(na, 1, INNER)
    wdown = mlstm_w_down.astype(BF16)
    pw = pool_w.astype(BF16)
    ps = pool_scale.reshape(-1, 1, D)

    new_c, new_n, new_m = [], [], []
    for l in range(DEPTH):
        x, hm = _ffn(xs if l == 0 else (x,), mods, norm_g, w_in, w_out, fg, layer=l, sub=0, which=0)
        if l % 2 == 0:
            a = l // 2
            q, k, v, xc, z, gates = _mlstm_pre(hm, wup, mlstm_conv_w, conv_b, wqk, wv, wg, bg, a=a)
            gates_t = jnp.transpose(gates[:, :2 * NGATE])
            hf_p, hb_p, c_new, n_new, m_new = _scan_prompt(q, k, v, gates, gates_t)
            n0 = jnp.transpose(state_n[:, a], (0, 2, 1, 3))
            m0 = jnp.broadcast_to(jnp.transpose(state_m[:, a], (0, 2, 1))[..., None],
                                  (DEC_BATCH, HEADS, 2, GATE_LANES))
            hf_s, hb_s = _scan_sample(q, k, v, gates, gates_t, state_C, n0, m0, a=a)
            x = _mlstm_out(x, hf_p, hb_p, hf_s, hb_s, xc, z, mods, mng, skip, wdown, layer=l, a=a)
            new_c.append(c_new)
            new_n.append(jnp.transpose(n_new, (0, 2, 1, 3))[:, None])
            new_m.append(jnp.transpose(m_new[..., 0], (0, 2, 1))[:, None])
            x = (x,)
        else:
            x = _pool(x, hm, mods, pw, ps, layer=l, p=l // 2)
        x = _ffn(x, mods, norm_g, w_in, w_out, fg, layer=l, sub=2, which=1, final=(l == DEPTH - 1))
    y_prompt = x[0].reshape(BATCH, SEQ, D)
    y_sample = x[1].reshape(DEC_BATCH, DEC_SEQ, D)
    return (y_prompt, y_sample, jnp.concatenate(new_c, axis=1), jnp.concatenate(new_n, axis=1),
            jnp.concatenate(new_m, axis=1))
```

```python
import functools
import math

import numpy as np
import jax
import jax.numpy as jnp
from jax import lax
from jax.experimental import pallas as pl
from jax.experimental.pallas import tpu as pltpu

F32 = jnp.float32
BF16 = jnp.bfloat16

D = 1024
BATCH = 16
SEQ = 256
DEPTH = 2
DEC_BATCH = 2
DEC_SEQ = 4096
GRID_W = 64
N_SUB = 3
INNER = 2 * D
HEADS = 4
DH = INNER // HEADS
QKV_BLOCK = 4
CONV_K = 5
POOL_WINDOWS = (2, 4, 8, 16)
POOL_GC = D // len(POOL_WINDOWS)
D_FF = 2816
EPS = 1e-6

NP_TOK = BATCH * SEQ
NS_TOK = DEC_BATCH * DEC_SEQ
NTOK = NP_TOK + NS_TOK
GRP = 4096
assert NP_TOK == GRP and DEC_SEQ == GRP

MXU_TILE = 256
GATE_LANES = 128
CHUNK = 256
HALO = 16

NGATE = 2 * 2 * HEADS

SCAN_HEADS = HEADS

FFN_TM = 512
PRE_TM = 2 * CHUNK
OUT_TM = 512
POOL_TM = 512
VMEM_LIMIT = 56 * 1024 * 1024


def _silu(x):
    return x * jax.nn.sigmoid(x)


def _dot(a, b):
    return jnp.dot(a, b, preferred_element_type=F32)


def _dot_nt(a, b):
    return lax.dot_general(a, b, (((1,), (1,)), ((), ())), preferred_element_type=F32)


def _dot_tn(a, b):
    return lax.dot_general(a, b, (((0,), (0,)), ((), ())), preferred_element_type=F32)


def _resident(shape, index_map):
    return pl.BlockSpec(shape, index_map, pipeline_mode=pl.Buffered(1))


def _params(n_axes):
    return pltpu.CompilerParams(
        dimension_semantics=("arbitrary",) * n_axes, vmem_limit_bytes=VMEM_LIMIT)


def _modulate(x, g, mod_ref, sub):
    ms = jnp.mean(x * x, axis=-1, keepdims=True)
    y = x * lax.rsqrt(ms + EPS) * g
    return y * (1.0 + mod_ref[3 * sub + 1:3 * sub + 2, :]) + mod_ref[3 * sub:3 * sub + 1, :]


def _split_dot(a, x, terms, left=True):
    parts = []
    r = x
    for _ in range(terms):
        p = r.astype(BF16)
        parts.append(p)
        r = r - p.astype(F32)
    acc = None
    for p in reversed(parts):
        t = _dot(a, p) if left else _dot(p, a)
        acc = t if acc is None else acc + t
    return acc


def _adaln_kernel(cond_ref, w_ref, b_ref, o_ref):
    s = _silu(cond_ref[...]).astype(BF16)
    o_ref[...] = _dot(s, w_ref[...].astype(BF16)) + b_ref[...]


def _adaln(cond8, ada_w, ada_b):
    return pl.pallas_call(
        _adaln_kernel,
        grid=(DEPTH, 3 * N_SUB),
        in_specs=[
            pl.BlockSpec((8, D), lambda l, j: (0, 0)),
            pl.BlockSpec((None, D, D), lambda l, j: (l, 0, j)),
            pl.BlockSpec((None, None, 1, D), lambda l, j: (l, j, 0, 0)),
        ],
        out_specs=pl.BlockSpec((None, None, 8, D), lambda l, j: (l, j, 0, 0)),
        out_shape=jax.ShapeDtypeStruct((DEPTH, 3 * N_SUB, 8, D), F32),
        compiler_params=_params(2),
        name="adaln",
    )(cond8, ada_w, ada_b.reshape(DEPTH, 3 * N_SUB, 1, D))


def _ffn_kernel(*refs, layer, sub, split_in, emit_h, final):
    n_x = 2 if split_in else 1
    x_refs, (mod_ref, g_ref, win_ref, wout_ref, fg_ref), o_refs = refs[:n_x], refs[n_x:n_x + 5], refs[n_x + 5:]
    is_prompt = pl.program_id(0) < NP_TOK // FFN_TM
    x = jnp.where(is_prompt, x_refs[0][...], x_refs[1][...]) if split_in else x_refs[0][...]
    h = _modulate(x, g_ref[layer, sub:sub + 1, :], mod_ref, sub).astype(BF16)
    acc = jnp.zeros(x.shape, F32)
    for c in range(D_FF // MXU_TILE):
        lo = c * MXU_TILE
        a = _dot(h, win_ref[:, lo:lo + MXU_TILE].astype(BF16))
        b = _dot(h, win_ref[:, D_FF + lo:D_FF + lo + MXU_TILE].astype(BF16))
        u = (_silu(a) * b).astype(BF16)
        acc = acc + _dot(u, wout_ref[lo:lo + MXU_TILE, :].astype(BF16))
    out = x + (0.5 * mod_ref[3 * sub + 2:3 * sub + 3, :]) * acc
    if not final:
        o_refs[0][...] = out
        if emit_h:
            o_refs[1][...] = _modulate(out, g_ref[layer, 1:2, :], mod_ref, 1).astype(BF16)
        return
    ms = jnp.mean(out * out, axis=-1, keepdims=True)
    out = out * lax.rsqrt(ms + EPS) * fg_ref[...]

    @pl.when(is_prompt)
    def _():
        o_refs[0][...] = out

    @pl.when(jnp.logical_not(is_prompt))
    def _():
        o_refs[1][...] = out


def _ffn(xs, mods, norm_g, w_in, w_out, final_g, *, layer, sub, which, final=False):
    tm = FFN_TM
    npt = NP_TOK // tm
    prompt_spec = pl.BlockSpec((tm, D), lambda i: (jnp.minimum(i, npt - 1), 0))
    sample_spec = pl.BlockSpec((tm, D), lambda i: (jnp.maximum(i - npt, 0), 0))
    whole_spec = pl.BlockSpec((tm, D), lambda i: (i, 0))
    split_in = len(xs) == 2
    emit_h = sub == 0
    if final:
        out_specs = [prompt_spec, sample_spec]
        out_shape = [jax.ShapeDtypeStruct((NP_TOK, D), F32), jax.ShapeDtypeStruct((NS_TOK, D), F32)]
    elif emit_h:
        out_specs = [whole_spec, whole_spec]
        out_shape = [jax.ShapeDtypeStruct((NTOK, D), F32), jax.ShapeDtypeStruct((NTOK, D), BF16)]
    else:
        out_specs = whole_spec
        out_shape = jax.ShapeDtypeStruct((NTOK, D), F32)
    return pl.pallas_call(
        functools.partial(_ffn_kernel, layer=layer, sub=sub, split_in=split_in, emit_h=emit_h,
                          final=final),
        grid=(NTOK // tm,),
        in_specs=([prompt_spec, sample_spec] if split_in else [whole_spec]) + [
            pl.BlockSpec((None, None, 3 * N_SUB, D), lambda i: (layer, (i * tm) // GRP, 0, 0)),
            _resident((DEPTH, N_SUB, D), lambda i: (0, 0, 0)),
            _resident((None, None, D, 2 * D_FF), lambda i: (layer, which, 0, 0)),
            _resident((None, None, D_FF, D), lambda i: (layer, which, 0, 0)),
            _resident((1, D), lambda i: (0, 0)),
        ],
        out_specs=out_specs,
        out_shape=out_shape,
        compiler_params=_params(1),
        name=f"ffn_l{layer}_{which}",
    )(*xs, mods, norm_g, w_in, w_out, final_g)


def _mlstm_pre_kernel(hp_ref, hc_ref, hn_ref, wup_ref, cw_ref, cb_ref, wqk_ref, wv_ref, wg_ref,
                      bg_ref, q_ref, k_ref, v_ref, xco_ref, z_ref, gate_ref, xm_s, acc_s):
    tm = PRE_TM
    nrow = tm + 2 * HALO
    i = pl.program_id(0)
    he = jnp.concatenate([hp_ref[...], hc_ref[...], hn_ref[...]], axis=0)
    xm_s[...] = _dot(he, wup_ref[:, :INNER].astype(BF16))
    z_ref[...] = _dot(hc_ref[...], wup_ref[:, INNER:].astype(BF16)).astype(BF16)

    is_prompt = i < NP_TOK // tm
    seqlen = jnp.where(is_prompt, SEQ, DEC_SEQ)
    starts_seq = ((i * tm) & (seqlen - 1)) == 0
    ends_seq = ((i * tm + tm) & (seqlen - 1)) == 0
    xm_s[0:HALO, :] = jnp.where(starts_seq, 0.0, xm_s[0:HALO, :])
    xm_s[HALO + tm:, :] = jnp.where(ends_seq, 0.0, xm_s[HALO + tm:, :])
    xm_all = xm_s[...]
    taps = []
    for j in range(CONV_K):
        off = j - CONV_K // 2
        taps.append(xm_all if off == 0 else pltpu.roll(xm_all, (nrow - off) % nrow, axis=0))

    def conv_rows(lo, hi, at_seq_end):
        acc = cb_ref[...]
        for j in range(CONV_K):
            off = j - CONV_K // 2
            term = taps[j][HALO + lo:HALO + hi]
            if at_seq_end and off != 0:
                pos = (i * tm + lo + lax.broadcasted_iota(jnp.int32, (hi - lo, 1), 0)) & (seqlen - 1)
                inside = (pos + off >= 0) if off < 0 else (pos + off < seqlen)
                term = jnp.where(inside, term, 0.0)
            acc = acc + term * cw_ref[j]
        return acc

    edge = 8
    lo = 0
    for b in range(SEQ, tm, SEQ):
        acc_s[lo:b - edge, :] = conv_rows(lo, b - edge, False)
        acc_s[b - edge:b + edge, :] = conv_rows(b - edge, b + edge, True)
        lo = b + edge
    acc_s[lo:tm, :] = conv_rows(lo, tm, False)
    xc = _silu(acc_s[...])
    xco_ref[...] = xc.astype(BF16)

    for t in range(INNER // MXU_TILE):
        cols = slice(t * MXU_TILE, (t + 1) * MXU_TILE)
        qk = _dot(xc[:, cols].astype(BF16), wqk_ref[t])
        q_ref[:, cols] = qk[:, :MXU_TILE].astype(BF16)
        k_ref[:, cols] = (qk[:, MXU_TILE:] / math.sqrt(DH)).astype(BF16)
        xm = xm_s[pl.ds(HALO, tm), cols].astype(BF16)
        v_ref[:, cols] = _dot(xm, wv_ref[t]).astype(BF16)

    g = (_dot(q_ref[...], wg_ref[0:INNER, :]) + _dot(k_ref[...], wg_ref[INNER:2 * INNER, :])
         + _dot(v_ref[...], wg_ref[2 * INNER:, :]) + bg_ref[...])
    lane = lax.broadcasted_iota(jnp.int32, g.shape, 1)
    logsig = jnp.minimum(g, 0.0) - jnp.log1p(jnp.exp(-jnp.abs(g)))
    gl = jnp.where(lane < 2 * NGATE, jnp.where((lane & HEADS) != 0, logsig, g), 0.0)
    tri_prefix = jnp.where(_tri(True), 1.0, 0.0).astype(BF16)
    lane_c = lax.broadcasted_iota(jnp.int32, (CHUNK, GATE_LANES), 1)
    for c in range(tm // CHUNK):
        rows = slice(c * CHUNK, (c + 1) * CHUNK)
        gl_c = gl[rows]
        prefix = _split_dot(tri_prefix, gl_c, 3)
        suffix = prefix[CHUNK - 1:CHUNK, :] - prefix + gl_c
        cum = jnp.where((lane_c & (2 * HEADS)) == 0, prefix, suffix)
        gate_ref[rows, :] = jnp.where(lane_c < NGATE, gl_c, cum)


def _mlstm_pre(hm, wup, conv_w, conv_b, wqk, wv, wg, bg, *, a):
    tm = PRE_TM
    nhalo = NTOK // HALO
    act = jax.ShapeDtypeStruct((NTOK, INNER), BF16)
    act_spec = pl.BlockSpec((tm, INNER), lambda i: (i, 0))
    return pl.pallas_call(
        _mlstm_pre_kernel,
        grid=(NTOK // tm,),
        in_specs=[
            pl.BlockSpec((HALO, D), lambda i: (jnp.maximum(i * (tm // HALO) - 1, 0), 0)),
            pl.BlockSpec((tm, D), lambda i: (i, 0)),
            pl.BlockSpec((HALO, D), lambda i: (jnp.minimum((i + 1) * (tm // HALO), nhalo - 1), 0)),
            _resident((None, D, 2 * INNER), lambda i: (a, 0, 0)),
            _resident((None, CONV_K, 1, INNER), lambda i: (a, 0, 0, 0)),
            _resident((None, 1, INNER), lambda i: (a, 0, 0)),
            _resident((None, INNER // MXU_TILE, MXU_TILE, 2 * MXU_TILE), lambda i: (a, 0, 0, 0)),
            _resident((None, INNER // MXU_TILE, MXU_TILE, MXU_TILE), lambda i: (a, 0, 0, 0)),
            _resident((None, 3 * INNER, GATE_LANES), lambda i: (a, 0, 0)),
            _resident((None, 1, GATE_LANES), lambda i: (a, 0, 0)),
        ],
        out_specs=[act_spec] * 5 + [pl.BlockSpec((tm, GATE_LANES), lambda i: (i, 0))],
        out_shape=[act] * 5 + [jax.ShapeDtypeStruct((NTOK, GATE_LANES), F32)],
        scratch_shapes=[pltpu.VMEM((tm + 2 * HALO, INNER), F32), pltpu.VMEM((tm, INNER), F32)],
        compiler_params=_params(1),
        name="mlstm_pre",
    )(hm, hm, hm, wup, conv_w, conv_b, wqk, wv, wg, bg)


def _tri(lower):
    r = lax.broadcasted_iota(jnp.int32, (CHUNK, CHUNK), 0)
    c = lax.broadcasted_iota(jnp.int32, (CHUNK, CHUNK), 1)
    return (c <= r) if lower else (c >= r)


def _pick_col(x, col):
    lane = lax.broadcasted_iota(jnp.int32, x.shape, 1)
    return jnp.sum(jnp.where(lane == col, x, 0.0), axis=1, keepdims=True)


def _pick_row(x, row):
    if isinstance(row, int):
        return x[row:row + 1, :]
    sub = lax.broadcasted_iota(jnp.int32, x.shape, 0)
    return jnp.sum(jnp.where(sub == row, x, 0.0), axis=0, keepdims=True)


def _head_index(j):
    return j if SCAN_HEADS == HEADS else pl.program_id(1) * SCAN_HEADS + j


def _unit_gates(gc, gt, head, lower, m0):
    d = 0 if lower else 1
    col_i = d * 2 * HEADS + head
    col_f = NGATE + col_i + HEADS
    bc = _pick_col(gc, col_f)
    ic = _pick_col(gc, col_i)
    br = _pick_row(gt, col_f)
    ir = _pick_row(gt, col_i)
    log_d = jnp.where(_tri(lower), bc - br + ir, -jnp.inf)
    inter = bc + m0
    mt = jnp.maximum(inter, jnp.max(log_d, axis=-1, keepdims=True))
    decay = jnp.exp(log_d - mt)
    ws = jnp.exp(inter - mt)
    b_last = bc[CHUNK - 1:CHUNK, :] if lower else bc[0:1, :]
    g = b_last - bc + ic
    m_new = jnp.maximum(b_last + m0, jnp.max(g, axis=0, keepdims=True))
    w = jnp.exp(g - m_new)
    w_row = jnp.exp(b_last - br + ir - m_new)
    dec = jnp.exp(b_last + m0 - m_new)
    return decay, mt, ws, w, m_new, dec, w_row


def _unit_apply(q, k, v, qk, gates, state):
    decay, mt, ws, w, _, dec, w_row = gates
    s = qk * decay
    num = _dot(s.astype(BF16), v)
    den = jnp.sum(s, axis=-1, keepdims=True)
    if state is not None:
        c0, n0 = state
        num = num + ws * _dot_nt(q, c0.astype(BF16))
        qn = _dot_nt(q, jnp.broadcast_to(n0, (GATE_LANES, DH)).astype(BF16))
        den = den + ws * qn[:, 0:1]
    h = num / jnp.maximum(jnp.abs(den), jnp.exp(-mt))
    c_new = _dot_tn((v.astype(F32) * w).astype(BF16), k)
    n_new = _dot(jnp.broadcast_to(w_row, (16, CHUNK)).astype(BF16), k)[0:1, :]
    if state is not None:
        c_new = dec * c0 + c_new
        n_new = dec * n0 + n_new
    return h, c_new, n_new


def _scan_prompt_kernel(q_ref, k_ref, v_ref, gc_ref, gt_ref, hf_ref, hb_ref, c_ref, n_ref, m_ref):
    gc, gt = gc_ref[...], gt_ref[...]
    m0 = jnp.zeros((1, 1), F32)
    gates = [[_unit_gates(gc, gt, _head_index(j), d == 0, m0) for d in range(2)] for j in range(SCAN_HEADS)]
    for j in range(SCAN_HEADS):
        cols = slice(j * DH, (j + 1) * DH)
        q, k, v = q_ref[:, cols], k_ref[:, cols], v_ref[:, cols]
        qk = _dot_nt(q, k)
        for d, h_ref in enumerate((hf_ref, hb_ref)):
            h, c_new, n_new = _unit_apply(q, k, v, qk, gates[j][d], None)
            h_ref[:, cols] = h.astype(BF16)
            c_ref[d, j] = c_new
            n_ref[j, d:d + 1, :] = n_new
            m_ref[j, d:d + 1, :] = jnp.broadcast_to(gates[j][d][4], (1, GATE_LANES))


def _scan_prompt(q, k, v, gates, gates_t):
    hps = SCAN_HEADS
    act_spec = pl.BlockSpec((CHUNK, hps * DH), lambda b, h: (b, h))
    hshape = jax.ShapeDtypeStruct((NP_TOK, INNER), BF16)
    return pl.pallas_call(
        _scan_prompt_kernel,
        grid=(BATCH, HEADS // hps),
        in_specs=[act_spec, act_spec, act_spec,
                  pl.BlockSpec((CHUNK, GATE_LANES), lambda b, h: (b, 0)),
                  pl.BlockSpec((2 * NGATE, CHUNK), lambda b, h: (0, b))],
        out_specs=[act_spec, act_spec,
                   pl.BlockSpec((None, None, 2, hps, DH, DH), lambda b, h: (b, 0, 0, h, 0, 0)),
                   pl.BlockSpec((None, hps, 2, DH), lambda b, h: (b, h, 0, 0)),
                   pl.BlockSpec((None, hps, 2, GATE_LANES), lambda b, h: (b, h, 0, 0))],
        out_shape=[hshape, hshape,
                   jax.ShapeDtypeStruct((BATCH, 1, 2, HEADS, DH, DH), F32),
                   jax.ShapeDtypeStruct((BATCH, HEADS, 2, DH), F32),
                   jax.ShapeDtypeStruct((BATCH, HEADS, 2, GATE_LANES), F32)],
        compiler_params=_params(2),
        name="scan_prompt",
    )(q, k, v, gates, gates_t)


def _scan_sample_kernel(qf_ref, kf_ref, vf_ref, gcf_ref, gtf_ref, qb_ref, kb_ref, vb_ref, gcb_ref,
                        gtb_ref, c0_ref, n0_ref, m0_ref, hf_ref, hb_ref, c_s, n_s, m_s):
    @pl.when(pl.program_id(2) == 0)
    def _():
        for j in range(SCAN_HEADS):
            for d in range(2):
                c_s[j, d] = c0_ref[d, j]
        n_s[...] = n0_ref[...]
        m_s[...] = m0_ref[...]

    units = ((qf_ref, kf_ref, vf_ref, gcf_ref, gtf_ref, hf_ref), (qb_ref, kb_ref, vb_ref, gcb_ref, gtb_ref, hb_ref))
    gates = [[_unit_gates(u[3][...], u[4][...], _head_index(j), d == 0, m_s[j, d:d + 1, 0:1])
              for d, u in enumerate(units)]
             for j in range(SCAN_HEADS)]
    for j in range(SCAN_HEADS):
        cols = slice(j * DH, (j + 1) * DH)
        for d, (q_ref, k_ref, v_ref, _, _, h_ref) in enumerate(units):
            q, k, v = q_ref[:, cols], k_ref[:, cols], v_ref[:, cols]
            qk = _dot_nt(q, k)
            state = (c_s[j, d], n_s[j, d:d + 1, :])
            h, c_new, n_new = _unit_apply(q, k, v, qk, gates[j][d], state)
            h_ref[:, cols] = h.astype(BF16)
            c_s[j, d] = c_new
            n_s[j, d:d + 1, :] = n_new
            m_s[j, d:d + 1, :] = jnp.broadcast_to(gates[j][d][4], (1, GATE_LANES))


def _scan_sample(q, k, v, gates, gates_t, c0, n0, m0, *, a):
    hps = SCAN_HEADS
    nc = DEC_SEQ // CHUNK
    base = NP_TOK // CHUNK

    def fwd(b, h, c):
        return b * nc + c

    def bwd(b, h, c):
        return b * nc + (nc - 1 - c)

    def specs(blk):
        act = pl.BlockSpec((CHUNK, hps * DH), lambda b, h, c: (base + blk(b, h, c), h))
        return [act, act, act,
                pl.BlockSpec((CHUNK, GATE_LANES), lambda b, h, c: (base + blk(b, h, c), 0)),
                pl.BlockSpec((2 * NGATE, CHUNK), lambda b, h, c: (0, base + blk(b, h, c)))]

    return pl.pallas_call(
        _scan_sample_kernel,
        grid=(DEC_BATCH, HEADS // hps, nc),
        in_specs=specs(fwd) + specs(bwd) + [
            pl.BlockSpec((None, None, 2, hps, DH, DH), lambda b, h, c: (b, a, 0, h, 0, 0)),
            pl.BlockSpec((None, hps, 2, DH), lambda b, h, c: (b, h, 0, 0)),
            pl.BlockSpec((None, hps, 2, GATE_LANES), lambda b, h, c: (b, h, 0, 0))],
        out_specs=[pl.BlockSpec((CHUNK, hps * DH), lambda b, h, c: (fwd(b, h, c), h)),
                   pl.BlockSpec((CHUNK, hps * DH), lambda b, h, c: (bwd(b, h, c), h))],
        out_shape=[jax.ShapeDtypeStruct((NS_TOK, INNER), BF16)] * 2,
        scratch_shapes=[pltpu.VMEM((hps, 2, DH, DH), F32), pltpu.VMEM((hps, 2, DH), F32),
                        pltpu.VMEM((hps, 2, GATE_LANES), F32)],
        compiler_params=_params(3),
        name="scan_sample",
    )(q, k, v, gates, gates_t, q, k, v, gates, gates_t, c0, n0, m0)


def _mlstm_out_kernel(x_ref, hfp_ref, hbp_ref, hfs_ref, hbs_ref, xc_ref, z_ref, mod_ref, ng_ref,
                      skip_ref, wd_ref, o_ref):
    is_prompt = pl.program_id(0) < NP_TOK // OUT_TM
    y = None
    for h in range(HEADS):
        cols = slice(h * DH, (h + 1) * DH)
        hf = jnp.where(is_prompt, hfp_ref[:, cols], hfs_ref[:, cols])
        hb = jnp.where(is_prompt, hbp_ref[:, cols], hbs_ref[:, cols])
        seg = hf.astype(F32) + hb.astype(F32)
        mu = jnp.mean(seg, axis=-1, keepdims=True)
        cen = seg - mu
        var = jnp.mean(cen * cen, axis=-1, keepdims=True)
        hn = cen * lax.rsqrt(var + EPS) * ng_ref[:, cols]
        out = (hn + skip_ref[:, cols] * xc_ref[:, cols].astype(F32)) * _silu(z_ref[:, cols].astype(F32))
        part = _dot(out.astype(BF16), wd_ref[cols, :])
        y = part if y is None else y + part
    o_ref[...] = x_ref[...] + mod_ref[5:6, :] * y


def _mlstm_out(x, hf_p, hb_p, hf_s, hb_s, xc, z, mods, norm_g, skip, w_down, *, layer, a):
    tm = OUT_TM
    npt = NP_TOK // tm
    act_spec = pl.BlockSpec((tm, INNER), lambda i: (i, 0))
    prompt_spec = pl.BlockSpec((tm, INNER), lambda i: (jnp.minimum(i, npt - 1), 0))
    sample_spec = pl.BlockSpec((tm, INNER), lambda i: (jnp.maximum(i - npt, 0), 0))
    return pl.pallas_call(
        _mlstm_out_kernel,
        grid=(NTOK // tm,),
        in_specs=[
            pl.BlockSpec((tm, D), lambda i: (i, 0)),
            prompt_spec, prompt_spec, sample_spec, sample_spec, act_spec, act_spec,
            pl.BlockSpec((None, None, 3 * N_SUB, D), lambda i: (layer, (i * tm) // GRP, 0, 0)),
            _resident((None, 1, INNER), lambda i: (a, 0, 0)),
            _resident((None, 1, INNER), lambda i: (a, 0, 0)),
            _resident((None, INNER, D), lambda i: (a, 0, 0)),
        ],
        out_specs=pl.BlockSpec((tm, D), lambda i: (i, 0)),
        out_shape=jax.ShapeDtypeStruct((NTOK, D), F32),
        compiler_params=_params(1),
        name="mlstm_out",
    )(x, hf_p, hb_p, hf_s, hb_s, xc, z, mods, norm_g, skip, w_down)


def _window_count(idx, win, n):
    lo = np.clip(idx - win // 2, 0, n)
    hi = np.clip(idx - win // 2 + win, 0, n)
    return (hi - lo).astype(np.float32)


def _pool_finish(x_ref, o_ref, h_g, p, g, mod_ref, wp_ref, ps_ref, rows):
    cols = slice(g * POOL_GC, (g + 1) * POOL_GC)
    y = _dot((p - h_g).astype(BF16), wp_ref[g]) * ps_ref[:, cols]
    o_ref[rows, cols] = x_ref[rows, cols] + mod_ref[5:6, cols] * y


def _pool_prompt_kernel(x_ref, h_ref, mod_ref, a_ref, cnt_ref, wp_ref, ps_ref, o_ref):
    for g in range(len(POOL_WINDOWS)):
        for s in range(POOL_TM // SEQ):
            rows = slice(s * SEQ, (s + 1) * SEQ)
            h_g = h_ref[rows, g * POOL_GC:(g + 1) * POOL_GC]
            p = _dot(a_ref[g], h_g) / cnt_ref[g]
            _pool_finish(x_ref, o_ref, h_g.astype(F32), p, g, mod_ref, wp_ref, ps_ref, rows)


def _pool_sample_kernel(xc_ref, hp_ref, hc_ref, hn_ref, mod_ref, a_ref, crow_ref, ccol_ref, wp_ref,
                        ps_ref, o_ref, hext):
    tm = POOL_TM
    tiles_per_seq = DEC_SEQ // tm
    j = pl.program_id(0) % tiles_per_seq
    hext[0:tm, :] = jnp.where(j > 0, hp_ref[...].astype(F32), 0.0)
    hext[tm:2 * tm, :] = hc_ref[...].astype(F32)
    hext[2 * tm:3 * tm, :] = jnp.where(j < tiles_per_seq - 1, hn_ref[...].astype(F32), 0.0)
    all_rows = slice(0, tm)
    for g, win in enumerate(POOL_WINDOWS):
        cols = slice(g * POOL_GC, (g + 1) * POOL_GC)
        acc = None
        for dr in range(-(win // 2), win - win // 2):
            term = hext[pl.ds(tm + GRID_W * dr, tm), cols]
            acc = term if acc is None else acc + term
        p_rows = acc / crow_ref[g]
        p = _dot(a_ref[g], p_rows.astype(BF16)) / ccol_ref[g]
        _pool_finish(xc_ref, o_ref, hext[tm:2 * tm, cols], p, g, mod_ref, wp_ref, ps_ref, all_rows)


def _band(n, win):
    t = np.arange(n)
    lo = np.clip(t - win // 2, 0, n)
    hi = np.clip(t - win // 2 + win, 0, n)
    u = np.arange(n)
    return ((u[None, :] >= lo[:, None]) & (u[None, :] < hi[:, None])).astype(np.float32)


def _pool(x, hm, mods, pool_w, pool_scale, *, layer, p):
    tm = POOL_TM
    a1 = jnp.asarray(np.stack([_band(SEQ, w) for w in POOL_WINDOWS]), BF16)
    a2 = jnp.asarray(np.stack([np.kron(np.eye(tm // GRID_W, dtype=np.float32), _band(GRID_W, w))
                               for w in POOL_WINDOWS]), BF16)
    ngroups = len(POOL_WINDOWS)
    base = NP_TOK // tm
    per_seq = DEC_SEQ // tm
    last = NTOK // tm - 1
    tok = np.arange(DEC_SEQ)
    cnt1 = jnp.asarray(np.stack([_window_count(np.arange(SEQ), w, SEQ) for w in POOL_WINDOWS])[..., None])
    crow = np.stack([_window_count(tok // GRID_W, w, DEC_SEQ // GRID_W) for w in POOL_WINDOWS])
    crow = jnp.asarray(crow.reshape(ngroups, per_seq, tm, 1).transpose(1, 0, 2, 3))
    ccol = jnp.asarray(np.stack([_window_count(tok[:tm] % GRID_W, w, GRID_W) for w in POOL_WINDOWS])[..., None])

    wp_spec = _resident((None, ngroups, POOL_GC, POOL_GC), lambda i: (p, 0, 0, 0))
    ps_spec = _resident((None, 1, D), lambda i: (p, 0, 0))
    tile = pl.BlockSpec((tm, D), lambda i: (i, 0))
    out_p = pl.pallas_call(
        _pool_prompt_kernel,
        grid=(NP_TOK // tm,),
        in_specs=[tile, tile,
                  pl.BlockSpec((None, None, 3 * N_SUB, D), lambda i: (layer, 0, 0, 0)),
                  _resident((ngroups, SEQ, SEQ), lambda i: (0, 0, 0)),
                  _resident((ngroups, SEQ, 1), lambda i: (0, 0, 0)), wp_spec, ps_spec],
        out_specs=tile,
        out_shape=jax.ShapeDtypeStruct((NP_TOK, D), F32),
        compiler_params=_params(1),
        name="pool_prompt",
    )(x, hm, mods, a1, cnt1, pool_w, pool_scale)

    out_s = pl.pallas_call(
        _pool_sample_kernel,
        grid=(NS_TOK // tm,),
        in_specs=[pl.BlockSpec((tm, D), lambda i: (base + i, 0)),
                  pl.BlockSpec((tm, D), lambda i: (jnp.maximum(base + i - 1, base), 0)),
                  pl.BlockSpec((tm, D), lambda i: (base + i, 0)),
                  pl.BlockSpec((tm, D), lambda i: (jnp.minimum(base + i + 1, last), 0)),
                  pl.BlockSpec((None, None, 3 * N_SUB, D), lambda i: (layer, 1 + i // per_seq, 0, 0)),
                  _resident((ngroups, tm, tm), lambda i: (0, 0, 0)),
                  pl.BlockSpec((None, ngroups, tm, 1), lambda i: (i % per_seq, 0, 0, 0)),
                  _resident((ngroups, tm, 1), lambda i: (0, 0, 0)), wp_spec, ps_spec],
        out_specs=tile,
        out_shape=jax.ShapeDtypeStruct((NS_TOK, D), F32),
        scratch_shapes=[pltpu.VMEM((3 * tm, D), F32)],
        compiler_params=_params(1),
        name="pool_sample",
    )(x, hm, hm, hm, mods, a2, crow, ccol, pool_w, pool_scale)
    return out_p, out_s


def _block_diag_tiles(w):
    na = w.shape[0]
    ntile = INNER // MXU_TILE
    rows = jnp.swapaxes(w, -1, -2).reshape(na, ntile, MXU_TILE, QKV_BLOCK)
    rows = jnp.tile(rows, (1, 1, 1, MXU_TILE // QKV_BLOCK))
    blk = np.arange(MXU_TILE) // QKV_BLOCK
    mask = jnp.asarray(blk[:, None] == blk[None, :])
    return jnp.where(mask, rows, 0.0).astype(BF16)


def kernel(x_prompt, x_sample, state_C, state_n, state_m, c, c_ctx, ada_w, ada_b, norm_g, ffn_w_in,
           ffn_w_out, mlstm_w_up, mlstm_conv_w, mlstm_conv_b, mlstm_w_q, mlstm_w_k, mlstm_w_v,
           mlstm_w_gate, mlstm_b_gate, mlstm_norm_g, mlstm_skip, mlstm_w_down, pool_w, pool_scale,
           final_g):
    na = mlstm_w_up.shape[0]
    xs = (x_prompt.reshape(NP_TOK, D), x_sample.reshape(NS_TOK, D))
    cond8 = jnp.concatenate([c_ctx[None, :], c, jnp.zeros((8 - 1 - DEC_BATCH, D), F32)], axis=0)
    mods = _adaln(cond8, ada_w, ada_b)
    mods = jnp.transpose(mods[:, :, :1 + DEC_BATCH], (0, 2, 1, 3))

    w_in = ffn_w_in
    w_out = ffn_w_out
    fg = final_g.reshape(1, D)
    wup = mlstm_w_up
    wqk = jnp.concatenate([_block_diag_tiles(mlstm_w_q), _block_diag_tiles(mlstm_w_k)], axis=-1)
    wv = _block_diag_tiles(mlstm_w_v)
    lane_pad = ((0, 0), (0, 0), (0, GATE_LANES - 2 * NGATE))
    wg = jnp.pad(jnp.tile(mlstm_w_gate.reshape(na, 3 * INNER, NGATE), (1, 1, 2)), lane_pad).astype(BF16)
    bg = jnp.pad(jnp.tile(mlstm_b_gate.reshape(na, 1, NGATE), (1, 1, 2)), lane_pad)
    conv_b = mlstm_conv_b.reshape(na, 1, INNER)
    mng = mlstm_norm_g.reshape(na, 1, INNER)
    skip = mlstm_skip.reshape(na, 1, INNER)
    wdown = mlstm_w_down.astype(BF16)
    pw = pool_w.astype(BF16)
    ps = pool_scale.reshape(-1, 1, D)

    new_c, new_n, new_m = [], [], []
    for l in range(DEPTH):
        x, hm = _ffn(xs if l == 0 else (x,), mods, norm_g, w_in, w_out, fg, layer=l, sub=0, which=0)
        if l % 2 == 0:
            a = l // 2
            q, k, v, xc, z, gates = _mlstm_pre(hm, wup, mlstm_conv_w, conv_b, wqk, wv, wg, bg, a=a)
            gates_t = jnp.transpose(gates[:, :2 * NGATE])
            hf_p, hb_p, c_new, n_new, m_new = _scan_prompt(q, k, v, gates, gates_t)
            n0 = jnp.transpose(state_n[:, a], (0, 2, 1, 3))
            m0 = jnp.broadcast_to(jnp.transpose(state_m[:, a], (0, 2, 1))[..., None],
                                  (DEC_BATCH, HEADS, 2, GATE_LANES))
            hf_s, hb_s = _scan_sample(q, k, v, gates, gates_t, state_C, n0, m0, a=a)
            x = _mlstm_out(x, hf_p, hb_p, hf_s, hb_s, xc, z, mods, mng, skip, wdown, layer=l, a=a)
            new_c.append(c_new)
            new_n.append(jnp.transpose(n_new, (0, 2, 1, 3))[:, None])
            new_m.append(jnp.transpose(m_new[..., 0], (0, 2, 1))[:, None])
            x = (x,)
        else:
            x = _pool(x, hm, mods, pw, ps, layer=l, p=l // 2)
        x = _ffn(x, mods, norm_g, w_in, w_out, fg, layer=l, sub=2, which=1, final=(l == DEPTH - 1))
    y_prompt = x[0].reshape(BATCH, SEQ, D)
    y_sample = x[1].reshape(DEC_BATCH, DEC_SEQ, D)
    return (y_prompt, y_sample, jnp.concatenate(new_c, axis=1), jnp.concatenate(new_n, axis=1),
            jnp.concatenate(new_m, axis=1))
```

```python
import functools
import math

import numpy as np
import jax
import jax.numpy as jnp
from jax import lax
from jax.experimental import pallas as pl
from jax.experimental.pallas import tpu as pltpu

F32 = jnp.float32
BF16 = jnp.bfloat16

D = 1024
BATCH = 16
SEQ = 256
DEPTH = 2
DEC_BATCH = 2
DEC_SEQ = 4096
GRID_W = 64
N_SUB = 3
INNER = 2 * D
HEADS = 4
DH = INNER // HEADS
QKV_BLOCK = 4
CONV_K = 5
POOL_WINDOWS = (2, 4, 8, 16)
POOL_GC = D // len(POOL_WINDOWS)
D_FF = 2816
EPS = 1e-6

NP_TOK = BATCH * SEQ
NS_TOK = DEC_BATCH * DEC_SEQ
NTOK = NP_TOK + NS_TOK
GRP = 4096
assert NP_TOK == GRP and DEC_SEQ == GRP

MXU_TILE = 256
GATE_LANES = 128
CHUNK = 256
HALO = 16

NGATE = 2 * 2 * HEADS

FFN_TM = 512
PRE_TM = 2 * CHUNK
OUT_TM = 512
POOL_TM = 512
VMEM_LIMIT = 56 * 1024 * 1024


def _silu(x):
    return x * jax.nn.sigmoid(x)


def _dot(a, b):
    return jnp.dot(a, b, preferred_element_type=F32)


def _dot_nt(a, b):
    return lax.dot_general(a, b, (((1,), (1,)), ((), ())), preferred_element_type=F32)


def _dot_tn(a, b):
    return lax.dot_general(a, b, (((0,), (0,)), ((), ())), preferred_element_type=F32)


def _resident(shape, index_map):
    return pl.BlockSpec(shape, index_map, pipeline_mode=pl.Buffered(1))


def _params(n_axes):
    return pltpu.CompilerParams(
        dimension_semantics=("arbitrary",) * n_axes, vmem_limit_bytes=VMEM_LIMIT)


def _modulate(x, g, mod_ref, sub):
    ms = jnp.mean(x * x, axis=-1, keepdims=True)
    gain = g * (1.0 + mod_ref[3 * sub + 1:3 * sub + 2, :])
    return x * lax.rsqrt(ms + EPS) * gain + mod_ref[3 * sub:3 * sub + 1, :]


def _split_dot(a, x, terms, left=True):
    parts = []
    r = x
    for _ in range(terms):
        p = r.astype(BF16)
        parts.append(p)
        r = r - p.astype(F32)
    acc = None
    for p in reversed(parts):
        t = _dot(a, p) if left else _dot(p, a)
        acc = t if acc is None else acc + t
    return acc


def _adaln_kernel(cond_ref, w_ref, b_ref, o_ref):
    s = _silu(cond_ref[...]).astype(BF16)
    o_ref[...] = _dot(s, w_ref[...].astype(BF16)) + b_ref[...]


def _adaln(cond8, ada_w, ada_b):
    return pl.pallas_call(
        _adaln_kernel,
        grid=(DEPTH, 3 * N_SUB),
        in_specs=[
            pl.BlockSpec((8, D), lambda l, j: (0, 0)),
            pl.BlockSpec((None, D, D), lambda l, j: (l, 0, j)),
            pl.BlockSpec((None, None, 1, D), lambda l, j: (l, j, 0, 0)),
        ],
        out_specs=pl.BlockSpec((None, None, 8, D), lambda l, j: (l, j, 0, 0)),
        out_shape=jax.ShapeDtypeStruct((DEPTH, 3 * N_SUB, 8, D), F32),
        compiler_params=_params(2),
        name="adaln",
    )(cond8, ada_w, ada_b.reshape(DEPTH, 3 * N_SUB, 1, D))


def _ffn_kernel(*refs, layer, sub, split_in, emit_h, final):
    n_x = 2 if split_in else 1
    x_refs, (mod_ref, g_ref, win_ref, wout_ref, fg_ref), o_refs = refs[:n_x], refs[n_x:n_x + 5], refs[n_x + 5:]
    is_prompt = pl.program_id(0) < NP_TOK // FFN_TM
    x = jnp.where(is_prompt, x_refs[0][...], x_refs[1][...]) if split_in else x_refs[0][...]
    h = _modulate(x, g_ref[layer, sub:sub + 1, :], mod_ref, sub).astype(BF16)
    acc = jnp.zeros(x.shape, F32)
    for c in range(D_FF // MXU_TILE):
        lo = c * MXU_TILE
        a = _dot(h, win_ref[:, lo:lo + MXU_TILE].astype(BF16))
        b = _dot(h, win_ref[:, D_FF + lo:D_FF + lo + MXU_TILE].astype(BF16))
        u = (_silu(a) * b).astype(BF16)
        acc = acc + _dot(u, wout_ref[lo:lo + MXU_TILE, :].astype(BF16))
    out = x + (0.5 * mod_ref[3 * sub + 2:3 * sub + 3, :]) * acc
    if not final:
        o_refs[0][...] = out
        if emit_h:
            o_refs[1][...] = _modulate(out, g_ref[layer, 1:2, :], mod_ref, 1).astype(BF16)
        return
    ms = jnp.mean(out * out, axis=-1, keepdims=True)
    out = out * lax.rsqrt(ms + EPS) * fg_ref[...]

    @pl.when(is_prompt)
    def _():
        o_refs[0][...] = out

    @pl.when(jnp.logical_not(is_prompt))
    def _():
        o_refs[1][...] = out


def _ffn(xs, mods, norm_g, w_in, w_out, final_g, *, layer, sub, which, final=False):
    tm = FFN_TM
    npt = NP_TOK // tm
    prompt_spec = pl.BlockSpec((tm, D), lambda i: (jnp.minimum(i, npt - 1), 0))
    sample_spec = pl.BlockSpec((tm, D), lambda i: (jnp.maximum(i - npt, 0), 0))
    whole_spec = pl.BlockSpec((tm, D), lambda i: (i, 0))
    split_in = len(xs) == 2
    emit_h = sub == 0
    if final:
        out_specs = [prompt_spec, sample_spec]
        out_shape = [jax.ShapeDtypeStruct((NP_TOK, D), F32), jax.ShapeDtypeStruct((NS_TOK, D), F32)]
    elif emit_h:
        out_specs = [whole_spec, whole_spec]
        out_shape = [jax.ShapeDtypeStruct((NTOK, D), F32), jax.ShapeDtypeStruct((NTOK, D), BF16)]
    else:
        out_specs = whole_spec
        out_shape = jax.ShapeDtypeStruct((NTOK, D), F32)
    return pl.pallas_call(
        functools.partial(_ffn_kernel, layer=layer, sub=sub, split_in=split_in, emit_h=emit_h,
                          final=final),
        grid=(NTOK // tm,),
        in_specs=([prompt_spec, sample_spec] if split_in else [whole_spec]) + [
            pl.BlockSpec((None, None, 3 * N_SUB, D), lambda i: (layer, (i * tm) // GRP, 0, 0)),
            _resident((DEPTH, N_SUB, D), lambda i: (0, 0, 0)),
            _resident((None, None, D, 2 * D_FF), lambda i: (layer, which, 0, 0)),
            _resident((None, None, D_FF, D), lambda i: (layer, which, 0, 0)),
            _resident((1, D), lambda i: (0, 0)),
        ],
        out_specs=out_specs,
        out_shape=out_shape,
        compiler_params=_params(1),
        name=f"ffn_l{layer}_{which}",
    )(*xs, mods, norm_g, w_in, w_out, final_g)


def _mlstm_pre_kernel(hp_ref, hc_ref, hn_ref, wup_ref, cw_ref, cb_ref, wqk_ref, wv_ref, wg_ref,
                      bg_ref, q_ref, k_ref, v_ref, xco_ref, z_ref, gate_ref, xm_s, acc_s):
    tm = PRE_TM
    nrow = tm + 2 * HALO
    i = pl.program_id(0)
    he = jnp.concatenate([hp_ref[...], hc_ref[...], hn_ref[...]], axis=0)
    xm_s[...] = _dot(he, wup_ref[:, :INNER].astype(BF16))
    z_ref[...] = _dot(hc_ref[...], wup_ref[:, INNER:].astype(BF16)).astype(BF16)

    is_prompt = i < NP_TOK // tm
    seqlen = jnp.where(is_prompt, SEQ, DEC_SEQ)
    starts_seq = ((i * tm) & (seqlen - 1)) == 0
    ends_seq = ((i * tm + tm) & (seqlen - 1)) == 0
    xm_s[0:HALO, :] = jnp.where(starts_seq, 0.0, xm_s[0:HALO, :])
    xm_s[HALO + tm:, :] = jnp.where(ends_seq, 0.0, xm_s[HALO + tm:, :])
    xm_all = xm_s[...]
    taps = []
    for j in range(CONV_K):
        off = j - CONV_K // 2
        taps.append(xm_all if off == 0 else pltpu.roll(xm_all, (nrow - off) % nrow, axis=0))

    def conv_rows(lo, hi, at_seq_end):
        acc = cb_ref[...]
        for j in range(CONV_K):
            off = j - CONV_K // 2
            term = taps[j][HALO + lo:HALO + hi]
            if at_seq_end and off != 0:
                pos = (i * tm + lo + lax.broadcasted_iota(jnp.int32, (hi - lo, 1), 0)) & (seqlen - 1)
                inside = (pos + off >= 0) if off < 0 else (pos + off < seqlen)
                term = jnp.where(inside, term, 0.0)
            acc = acc + term * cw_ref[j]
        return acc

    edge = 8
    lo = 0
    for b in range(SEQ, tm, SEQ):
        acc_s[lo:b - edge, :] = conv_rows(lo, b - edge, False)
        acc_s[b - edge:b + edge, :] = conv_rows(b - edge, b + edge, True)
        lo = b + edge
    acc_s[lo:tm, :] = conv_rows(lo, tm, False)
    xc = _silu(acc_s[...])
    xco_ref[...] = xc.astype(BF16)

    for t in range(INNER // MXU_TILE):
        cols = slice(t * MXU_TILE, (t + 1) * MXU_TILE)
        xc_t = xc[:, cols].astype(BF16)
        qk = _dot(xc_t, wqk_ref[t])
        q_ref[:, cols] = qk[:, :MXU_TILE].astype(BF16)
        k_ref[:, cols] = (qk[:, MXU_TILE:] / math.sqrt(DH)).astype(BF16)
        xm_t = xm_s[pl.ds(HALO, tm), cols].astype(BF16)
        v_ref[:, cols] = _dot(xm_t, wv_ref[t]).astype(BF16)
        part = _dot(xc_t, wg_ref[cols, :]) + _dot(xm_t, wg_ref[INNER + t * MXU_TILE:INNER + (t + 1) * MXU_TILE, :])
        g = part + (bg_ref[...] if t == 0 else g)
    lane = lax.broadcasted_iota(jnp.int32, g.shape, 1)
    logsig = jnp.minimum(g, 0.0) - jnp.log1p(jnp.exp(-jnp.abs(g)))
    gl = jnp.where(lane < 2 * NGATE, jnp.where((lane & HEADS) != 0, logsig, g), 0.0)
    tri_prefix = jnp.where(_tri(True), 1.0, 0.0).astype(BF16)
    lane_c = lax.broadcasted_iota(jnp.int32, (CHUNK, GATE_LANES), 1)
    for c in range(tm // CHUNK):
        rows = slice(c * CHUNK, (c + 1) * CHUNK)
        gl_c = gl[rows]
        prefix = _split_dot(tri_prefix, gl_c, 3)
        suffix = prefix[CHUNK - 1:CHUNK, :] - prefix + gl_c
        cum = jnp.where((lane_c & (2 * HEADS)) == 0, prefix, suffix)
        gate_ref[rows, :] = jnp.where(lane_c < NGATE, gl_c, cum)


def _mlstm_pre(hm, wup, conv_w, conv_b, wqk, wv, wg, bg, *, a):
    tm = PRE_TM
    nhalo = NTOK // HALO
    act = jax.ShapeDtypeStruct((NTOK, INNER), BF16)
    act_spec = pl.BlockSpec((tm, INNER), lambda i: (i, 0))
    return pl.pallas_call(
        _mlstm_pre_kernel,
        grid=(NTOK // tm,),
        in_specs=[
            pl.BlockSpec((HALO, D), lambda i: (jnp.maximum(i * (tm // HALO) - 1, 0), 0)),
            pl.BlockSpec((tm, D), lambda i: (i, 0)),
            pl.BlockSpec((HALO, D), lambda i: (jnp.minimum((i + 1) * (tm // HALO), nhalo - 1), 0)),
            _resident((None, D, 2 * INNER), lambda i: (a, 0, 0)),
            _resident((None, CONV_K, 1, INNER), lambda i: (a, 0, 0, 0)),
            _resident((None, 1, INNER), lambda i: (a, 0, 0)),
            _resident((None, INNER // MXU_TILE, MXU_TILE, 2 * MXU_TILE), lambda i: (a, 0, 0, 0)),
            _resident((None, INNER // MXU_TILE, MXU_TILE, MXU_TILE), lambda i: (a, 0, 0, 0)),
            _resident((None, 2 * INNER, GATE_LANES), lambda i: (a, 0, 0)),
            _resident((None, 1, GATE_LANES), lambda i: (a, 0, 0)),
        ],
        out_specs=[act_spec] * 5 + [pl.BlockSpec((tm, GATE_LANES), lambda i: (i, 0))],
        out_shape=[act] * 5 + [jax.ShapeDtypeStruct((NTOK, GATE_LANES), F32)],
        scratch_shapes=[pltpu.VMEM((tm + 2 * HALO, INNER), F32), pltpu.VMEM((tm, INNER), F32)],
        compiler_params=_params(1),
        name="mlstm_pre",
    )(hm, hm, hm, wup, conv_w, conv_b, wqk, wv, wg, bg)


def _tri(lower):
    r = lax.broadcasted_iota(jnp.int32, (CHUNK, CHUNK), 0)
    c = lax.broadcasted_iota(jnp.int32, (CHUNK, CHUNK), 1)
    return (c <= r) if lower else (c >= r)


def _pick_col(x, col):
    lane = lax.broadcasted_iota(jnp.int32, x.shape, 1)
    return jnp.sum(jnp.where(lane == col, x, 0.0), axis=1, keepdims=True)


def _pick_row(x, row):
    if isinstance(row, int):
        return x[row:row + 1, :]
    sub = lax.broadcasted_iota(jnp.int32, x.shape, 0)
    return jnp.sum(jnp.where(sub == row, x, 0.0), axis=0, keepdims=True)


def _unit_gates(gc, gt, head, lower, m0):
    d = 0 if lower else 1
    col_i = d * 2 * HEADS + head
    col_f = NGATE + col_i + HEADS
    bc = _pick_col(gc, col_f)
    ic = _pick_col(gc, col_i)
    br = _pick_row(gt, col_f)
    ir = _pick_row(gt, col_i)
    log_d = jnp.where(_tri(lower), bc - br + ir, -jnp.inf)
    inter = bc + m0
    mt = jnp.maximum(inter, jnp.max(log_d, axis=-1, keepdims=True))
    decay = jnp.exp(log_d - mt)
    ws = jnp.exp(inter - mt)
    b_last = bc[CHUNK - 1:CHUNK, :] if lower else bc[0:1, :]
    g = b_last - bc + ic
    m_new = jnp.maximum(b_last + m0, jnp.max(g, axis=0, keepdims=True))
    w = jnp.exp(g - m_new)
    w_row = jnp.exp(b_last - br + ir - m_new)
    dec = jnp.exp(b_last + m0 - m_new)
    return decay, mt, ws, w, m_new, dec, w_row


def _unit_apply(q, k, v, qk, gates, state):
    decay, mt, ws, w, _, dec, w_row = gates
    s = qk * decay
    num = _dot(s.astype(BF16), v)
    den = jnp.sum(s, axis=-1, keepdims=True)
    if state is not None:
        c0, n0 = state
        num = num + ws * _dot_nt(q, c0.astype(BF16))
        qn = _dot_nt(q, jnp.broadcast_to(n0, (GATE_LANES, DH)).astype(BF16))
        den = den + ws * qn[:, 0:1]
    h = num / jnp.maximum(jnp.abs(den), jnp.exp(-mt))
    c_new = _dot_tn((v.astype(F32) * w).astype(BF16), k)
    n_new = _dot(jnp.broadcast_to(w_row, (16, CHUNK)).astype(BF16), k)[0:1, :]
    if state is not None:
        c_new = dec * c0 + c_new
        n_new = dec * n0 + n_new
    return h, c_new, n_new


def _scan_prompt_kernel(q_ref, k_ref, v_ref, gc_ref, gt_ref, h_ref, c_ref, n_ref, m_ref):
    gc, gt = gc_ref[...], gt_ref[...]
    m0 = jnp.zeros((1, 1), F32)
    gates = [[_unit_gates(gc, gt, j, d == 0, m0) for d in range(2)] for j in range(HEADS)]
    for j in range(HEADS):
        cols = slice(j * DH, (j + 1) * DH)
        q, k, v = q_ref[:, cols], k_ref[:, cols], v_ref[:, cols]
        qk = _dot_nt(q, k)
        h_sum = None
        for d in range(2):
            h, c_new, n_new = _unit_apply(q, k, v, qk, gates[j][d], None)
            h_sum = h if h_sum is None else h_sum + h
            c_ref[d, j] = c_new
            n_ref[d, j:j + 1, :] = n_new
            m_ref[d, j:j + 1, :] = jnp.broadcast_to(gates[j][d][4], (1, GATE_LANES))
        h_ref[:, cols] = h_sum.astype(BF16)


def _scan_prompt(q, k, v, gates, gates_t):
    act_spec = pl.BlockSpec((CHUNK, INNER), lambda b: (b, 0))
    return pl.pallas_call(
        _scan_prompt_kernel,
        grid=(BATCH,),
        in_specs=[act_spec, act_spec, act_spec,
                  pl.BlockSpec((CHUNK, GATE_LANES), lambda b: (b, 0)),
                  pl.BlockSpec((2 * NGATE, CHUNK), lambda b: (0, b))],
        out_specs=[act_spec,
                   pl.BlockSpec((None, None, 2, HEADS, DH, DH), lambda b: (b, 0, 0, 0, 0, 0)),
                   pl.BlockSpec((None, None, 2, HEADS, DH), lambda b: (b, 0, 0, 0, 0)),
                   pl.BlockSpec((None, 2, HEADS, GATE_LANES), lambda b: (b, 0, 0, 0))],
        out_shape=[jax.ShapeDtypeStruct((NP_TOK, INNER), BF16),
                   jax.ShapeDtypeStruct((BATCH, 1, 2, HEADS, DH, DH), F32),
                   jax.ShapeDtypeStruct((BATCH, 1, 2, HEADS, DH), F32),
                   jax.ShapeDtypeStruct((BATCH, 2, HEADS, GATE_LANES), F32)],
        compiler_params=_params(1),
        name="scan_prompt",
    )(q, k, v, gates, gates_t)


def _scan_sample_kernel(qf_ref, kf_ref, vf_ref, gcf_ref, gtf_ref, qb_ref, kb_ref, vb_ref, gcb_ref,
                        gtb_ref, c0_ref, n0_ref, m0_ref, hf_ref, hb_ref, c_s, n_s, m_s):
    @pl.when(pl.program_id(1) == 0)
    def _():
        c_s[...] = c0_ref[...]
        n_s[...] = n0_ref[...]
        m_s[...] = m0_ref[...]

    units = ((qf_ref, kf_ref, vf_ref, gcf_ref, gtf_ref, hf_ref), (qb_ref, kb_ref, vb_ref, gcb_ref, gtb_ref, hb_ref))
    gates = [[_unit_gates(u[3][...], u[4][...], j, d == 0, m_s[d, j:j + 1, 0:1]) for d, u in enumerate(units)]
             for j in range(HEADS)]
    for j in range(HEADS):
        cols = slice(j * DH, (j + 1) * DH)
        for d, (q_ref, k_ref, v_ref, _, _, h_ref) in enumerate(units):
            q, k, v = q_ref[:, cols], k_ref[:, cols], v_ref[:, cols]
            qk = _dot_nt(q, k)
            state = (c_s[d, j], n_s[d, j:j + 1, :])
            h, c_new, n_new = _unit_apply(q, k, v, qk, gates[j][d], state)
            h_ref[:, cols] = h.astype(BF16)
            c_s[d, j] = c_new
            n_s[d, j:j + 1, :] = n_new
            m_s[d, j:j + 1, :] = jnp.broadcast_to(gates[j][d][4], (1, GATE_LANES))


def _scan_sample(q, k, v, gates, gates_t, c0, n0, m0, *, a):
    nc = DEC_SEQ // CHUNK
    base = NP_TOK // CHUNK

    def fwd(b, c):
        return b * nc + c

    def bwd(b, c):
        return b * nc + (nc - 1 - c)

    def specs(blk):
        act = pl.BlockSpec((CHUNK, INNER), lambda b, c: (base + blk(b, c), 0))
        return [act, act, act,
                pl.BlockSpec((CHUNK, GATE_LANES), lambda b, c: (base + blk(b, c), 0)),
                pl.BlockSpec((2 * NGATE, CHUNK), lambda b, c: (0, base + blk(b, c)))]

    return pl.pallas_call(
        _scan_sample_kernel,
        grid=(DEC_BATCH, nc),
        in_specs=specs(fwd) + specs(bwd) + [
            pl.BlockSpec((None, None, 2, HEADS, DH, DH), lambda b, c: (b, a, 0, 0, 0, 0)),
            pl.BlockSpec((None, None, 2, HEADS, DH), lambda b, c: (b, a, 0, 0, 0)),
            pl.BlockSpec((None, 2, HEADS, GATE_LANES), lambda b, c: (b, 0, 0, 0))],
        out_specs=[pl.BlockSpec((CHUNK, INNER), lambda b, c: (fwd(b, c), 0)),
                   pl.BlockSpec((CHUNK, INNER), lambda b, c: (bwd(b, c), 0))],
        out_shape=[jax.ShapeDtypeStruct((NS_TOK, INNER), BF16)] * 2,
        scratch_shapes=[pltpu.VMEM((2, HEADS, DH, DH), F32), pltpu.VMEM((2, HEADS, DH), F32),
                        pltpu.VMEM((2, HEADS, GATE_LANES), F32)],
        compiler_params=_params(2),
        name="scan_sample",
    )(q, k, v, gates, gates_t, q, k, v, gates, gates_t, c0, n0, m0)


def _mlstm_out_kernel(x_ref, hp_ref, hfs_ref, hbs_ref, xc_ref, z_ref, mod_ref, ng_ref, skip_ref, wd_ref,
                      o_ref):
    def tile(h_sum):
        y = None
        for h in range(HEADS):
            cols = slice(h * DH, (h + 1) * DH)
            seg = h_sum(cols)
            mu = jnp.mean(seg, axis=-1, keepdims=True)
            cen = seg - mu
            var = jnp.mean(cen * cen, axis=-1, keepdims=True)
            hn = cen * lax.rsqrt(var + EPS) * ng_ref[:, cols]
            out = (hn + skip_ref[:, cols] * xc_ref[:, cols].astype(F32)) * _silu(z_ref[:, cols].astype(F32))
            part = _dot(out.astype(BF16), wd_ref[cols, :])
            y = part if y is None else y + part
        o_ref[...] = x_ref[...] + mod_ref[5:6, :] * y

    is_prompt = pl.program_id(0) < NP_TOK // OUT_TM

    @pl.when(is_prompt)
    def _():
        tile(lambda cols: hp_ref[:, cols].astype(F32))

    @pl.when(jnp.logical_not(is_prompt))
    def _():
        tile(lambda cols: hfs_ref[:, cols].astype(F32) + hbs_ref[:, cols].astype(F32))


def _mlstm_out(x, h_p, hf_s, hb_s, xc, z, mods, norm_g, skip, w_down, *, layer, a):
    tm = OUT_TM
    npt = NP_TOK // tm
    act_spec = pl.BlockSpec((tm, INNER), lambda i: (i, 0))
    prompt_spec = pl.BlockSpec((tm, INNER), lambda i: (jnp.minimum(i, npt - 1), 0))
    sample_spec = pl.BlockSpec((tm, INNER), lambda i: (jnp.maximum(i - npt, 0), 0))
    return pl.pallas_call(
        _mlstm_out_kernel,
        grid=(NTOK // tm,),
        in_specs=[
            pl.BlockSpec((tm, D), lambda i: (i, 0)),
            prompt_spec, sample_spec, sample_spec, act_spec, act_spec,
            pl.BlockSpec((None, None, 3 * N_SUB, D), lambda i: (layer, (i * tm) // GRP, 0, 0)),
            _resident((None, 1, INNER), lambda i: (a, 0, 0)),
            _resident((None, 1, INNER), lambda i: (a, 0, 0)),
            _resident((None, INNER, D), lambda i: (a, 0, 0)),
        ],
        out_specs=pl.BlockSpec((tm, D), lambda i: (i, 0)),
        out_shape=jax.ShapeDtypeStruct((NTOK, D), F32),
        compiler_params=_params(1),
        name="mlstm_out",
    )(x, h_p, hf_s, hb_s, xc, z, mods, norm_g, skip, w_down)


def _window_count(idx, win, n):
    lo = np.clip(idx - win // 2, 0, n)
    hi = np.clip(idx - win // 2 + win, 0, n)
    return (hi - lo).astype(np.float32)


def _pool_finish(x_ref, o_ref, h_g, p, g, mod_ref, wp_ref, ps_ref, rows):
    cols = slice(g * POOL_GC, (g + 1) * POOL_GC)
    y = _dot((p - h_g).astype(BF16), wp_ref[g]) * ps_ref[:, cols]
    o_ref[rows, cols] = x_ref[rows, cols] + mod_ref[5:6, cols] * y


def _pool_prompt_kernel(x_ref, h_ref, mod_ref, a_ref, cnt_ref, wp_ref, ps_ref, o_ref):
    for g in range(len(POOL_WINDOWS)):
        for s in range(POOL_TM // SEQ):
            rows = slice(s * SEQ, (s + 1) * SEQ)
            h_g = h_ref[rows, g * POOL_GC:(g + 1) * POOL_GC]
            p = _dot(a_ref[g], h_g) / cnt_ref[g]
            _pool_finish(x_ref, o_ref, h_g.astype(F32), p, g, mod_ref, wp_ref, ps_ref, rows)


def _pool_sample_kernel(xc_ref, hp_ref, hc_ref, hn_ref, mod_ref, a_ref, crow_ref, ccol_ref, wp_ref,
                        ps_ref, o_ref, hext):
    tm = POOL_TM
    tiles_per_seq = DEC_SEQ // tm
    j = pl.program_id(0) % tiles_per_seq
    hext[0:tm, :] = jnp.where(j > 0, hp_ref[...].astype(F32), 0.0)
    hext[tm:2 * tm, :] = hc_ref[...].astype(F32)
    hext[2 * tm:3 * tm, :] = jnp.where(j < tiles_per_seq - 1, hn_ref[...].astype(F32), 0.0)
    all_rows = slice(0, tm)
    for g, win in enumerate(POOL_WINDOWS):
        cols = slice(g * POOL_GC, (g + 1) * POOL_GC)
        acc = None
        for dr in range(-(win // 2), win - win // 2):
            term = hext[pl.ds(tm + GRID_W * dr, tm), cols]
            acc = term if acc is None else acc + term
        p_rows = acc / crow_ref[g]
        p = _dot(a_ref[g], p_rows.astype(BF16)) / ccol_ref[g]
        _pool_finish(xc_ref, o_ref, hext[tm:2 * tm, cols], p, g, mod_ref, wp_ref, ps_ref, all_rows)


def _band(n, win):
    t = np.arange(n)
    lo = np.clip(t - win // 2, 0, n)
    hi = np.clip(t - win // 2 + win, 0, n)
    u = np.arange(n)
    return ((u[None, :] >= lo[:, None]) & (u[None, :] < hi[:, None])).astype(np.float32)


def _pool(x, hm, mods, pool_w, pool_scale, *, layer, p):
    tm = POOL_TM
    a1 = jnp.asarray(np.stack([_band(SEQ, w) for w in POOL_WINDOWS]), BF16)
    a2 = jnp.asarray(np.stack([np.kron(np.eye(tm // GRID_W, dtype=np.float32), _band(GRID_W, w))
                               for w in POOL_WINDOWS]), BF16)
    ngroups = len(POOL_WINDOWS)
    base = NP_TOK // tm
    per_seq = DEC_SEQ // tm
    last = NTOK // tm - 1
    tok = np.arange(DEC_SEQ)
    cnt1 = jnp.asarray(np.stack([_window_count(np.arange(SEQ), w, SEQ) for w in POOL_WINDOWS])[..., None])
    crow = np.stack([_window_count(tok // GRID_W, w, DEC_SEQ // GRID_W) for w in POOL_WINDOWS])
    crow = jnp.asarray(crow.reshape(ngroups, per_seq, tm, 1).transpose(1, 0, 2, 3))
    ccol = jnp.asarray(np.stack([_window_count(tok[:tm] % GRID_W, w, GRID_W) for w in POOL_WINDOWS])[..., None])

    wp_spec = _resident((None, ngroups, POOL_GC, POOL_GC), lambda i: (p, 0, 0, 0))
    ps_spec = _resident((None, 1, D), lambda i: (p, 0, 0))
    tile = pl.BlockSpec((tm, D), lambda i: (i, 0))
    out_p = pl.pallas_call(
        _pool_prompt_kernel,
        grid=(NP_TOK // tm,),
        in_specs=[tile, tile,
                  pl.BlockSpec((None, None, 3 * N_SUB, D), lambda i: (layer, 0, 0, 0)),
                  _resident((ngroups, SEQ, SEQ), lambda i: (0, 0, 0)),
                  _resident((ngroups, SEQ, 1), lambda i: (0, 0, 0)), wp_spec, ps_spec],
        out_specs=tile,
        out_shape=jax.ShapeDtypeStruct((NP_TOK, D), F32),
        compiler_params=_params(1),
        name="pool_prompt",
    )(x, hm, mods, a1, cnt1, pool_w, pool_scale)

    out_s = pl.pallas_call(
        _pool_sample_kernel,
        grid=(NS_TOK // tm,),
        in_specs=[pl.BlockSpec((tm, D), lambda i: (base + i, 0)),
                  pl.BlockSpec((tm, D), lambda i: (jnp.maximum(base + i - 1, base), 0)),
                  pl.BlockSpec((tm, D), lambda i: (base + i, 0)),
                  pl.BlockSpec((tm, D), lambda i: (jnp.minimum(base + i + 1, last), 0)),
                  pl.BlockSpec((None, None, 3 * N_SUB, D), lambda i: (layer, 1 + i // per_seq, 0, 0)),
                  _resident((ngroups, tm, tm), lambda i: (0, 0, 0)),
                  pl.BlockSpec((None, ngroups, tm, 1), lambda i: (i % per_seq, 0, 0, 0)),
                  _resident((ngroups, tm, 1), lambda i: (0, 0, 0)), wp_spec, ps_spec],
        out_specs=tile,
        out_shape=jax.ShapeDtypeStruct((NS_TOK, D), F32),
        scratch_shapes=[pltpu.VMEM((3 * tm, D), F32)],
        compiler_params=_params(1),
        name="pool_sample",
    )(x, hm, hm, hm, mods, a2, crow, ccol, pool_w, pool_scale)
    return out_p, out_s


def _block_diag_tiles(w):
    na = w.shape[0]
    ntile = INNER // MXU_TILE
    rows = jnp.swapaxes(w, -1, -2).reshape(na, ntile, MXU_TILE, QKV_BLOCK)
    rows = jnp.tile(rows, (1, 1, 1, MXU_TILE // QKV_BLOCK))
    blk = np.arange(MXU_TILE) // QKV_BLOCK
    mask = jnp.asarray(blk[:, None] == blk[None, :])
    return jnp.where(mask, rows, 0.0).astype(BF16)


def _fold_gate_weights(w_q, w_k, w_v, w_gate):
    na = w_q.shape[0]
    wg = w_gate.reshape(na, 3, INNER // QKV_BLOCK, QKV_BLOCK, NGATE)

    def fold(w, part):
        return jnp.einsum('anij,anig->anjg', w, wg[:, part], precision=lax.Precision.HIGHEST)

    fold_c = fold(w_q, 0) + fold(w_k, 1) / math.sqrt(DH)
    return jnp.concatenate([fold_c, fold(w_v, 2)], axis=1).reshape(na, 2 * INNER, NGATE)


def kernel(x_prompt, x_sample, state_C, state_n, state_m, c, c_ctx, ada_w, ada_b, norm_g, ffn_w_in,
           ffn_w_out, mlstm_w_up, mlstm_conv_w, mlstm_conv_b, mlstm_w_q, mlstm_w_k, mlstm_w_v,
           mlstm_w_gate, mlstm_b_gate, mlstm_norm_g, mlstm_skip, mlstm_w_down, pool_w, pool_scale,
           final_g):
    na = mlstm_w_up.shape[0]
    xs = (x_prompt.reshape(NP_TOK, D), x_sample.reshape(NS_TOK, D))
    cond8 = jnp.concatenate([c_ctx[None, :], c, jnp.zeros((8 - 1 - DEC_BATCH, D), F32)], axis=0)
    mods = _adaln(cond8, ada_w, ada_b)
    mods = jnp.transpose(mods[:, :, :1 + DEC_BATCH], (0, 2, 1, 3))

    w_in = ffn_w_in
    w_out = ffn_w_out
    fg = final_g.reshape(1, D)
    wup = mlstm_w_up
    wqk = jnp.concatenate([_block_diag_tiles(mlstm_w_q), _block_diag_tiles(mlstm_w_k)], axis=-1)
    wv = _block_diag_tiles(mlstm_w_v)
    lane_pad = ((0, 0), (0, 0), (0, GATE_LANES - 2 * NGATE))
    wg = jnp.pad(jnp.tile(_fold_gate_weights(mlstm_w_q, mlstm_w_k, mlstm_w_v, mlstm_w_gate), (1, 1, 2)),
                 lane_pad).astype(BF16)
    bg = jnp.pad(jnp.tile(mlstm_b_gate.reshape(na, 1, NGATE), (1, 1, 2)), lane_pad)
    conv_b = mlstm_conv_b.reshape(na, 1, INNER)
    mng = mlstm_norm_g.reshape(na, 1, INNER)
    skip = mlstm_skip.reshape(na, 1, INNER)
    wdown = mlstm_w_down.astype(BF16)
    pw = pool_w.astype(BF16)
    ps = pool_scale.reshape(-1, 1, D)

    new_c, new_n, new_m = [], [], []
    for l in range(DEPTH):
        x, hm = _ffn(xs if l == 0 else (x,), mods, norm_g, w_in, w_out, fg, layer=l, sub=0, which=0)
        if l % 2 == 0:
            a = l // 2
            q, k, v, xc, z, gates = _mlstm_pre(hm, wup, mlstm_conv_w, conv_b, wqk, wv, wg, bg, a=a)
            gates_t = jnp.transpose(gates[:, :2 * NGATE])
            h_p, c_new, n_new, m_new = _scan_prompt(q, k, v, gates, gates_t)
            m0 = jnp.broadcast_to(state_m[:, a, :, :, None], (DEC_BATCH, 2, HEADS, GATE_LANES))
            hf_s, hb_s = _scan_sample(q, k, v, gates, gates_t, state_C, state_n, m0, a=a)
            x = _mlstm_out(x, h_p, hf_s, hb_s, xc, z, mods, mng, skip, wdown, layer=l, a=a)
            new_c.append(c_new)
            new_n.append(n_new)
            new_m.append(m_new[:, None, :, :, 0])
            x = (x,)
        else:
            x = _pool(x, hm, mods, pw, ps, layer=l, p=l // 2)
        x = _ffn(x, mods, norm_g, w_in, w_out, fg, layer=l, sub=2, which=1, final=(l == DEPTH - 1))
    y_prompt = x[0].reshape(BATCH, SEQ, D)
    y_sample = x[1].reshape(DEC_BATCH, DEC_SEQ, D)
    return (y_prompt, y_sample, jnp.concatenate(new_c, axis=1), jnp.concatenate(new_n, axis=1),
            jnp.concatenate(new_m, axis=1))
```

```python
import functools
import math

import numpy as np
import jax
import jax.numpy as jnp
from jax import lax
from jax.experimental import pallas as pl
from jax.experimental.pallas import tpu as pltpu

F32 = jnp.float32
BF16 = jnp.bfloat16

D = 1024
BATCH = 16
SEQ = 256
DEPTH = 2
DEC_BATCH = 2
DEC_SEQ = 4096
GRID_W = 64
N_SUB = 3
INNER = 2 * D
HEADS = 4
DH = INNER // HEADS
QKV_BLOCK = 4
CONV_K = 5
POOL_WINDOWS = (2, 4, 8, 16)
POOL_GC = D // len(POOL_WINDOWS)
D_FF = 2816
EPS = 1e-6

NP_TOK = BATCH * SEQ
NS_TOK = DEC_BATCH * DEC_SEQ
NTOK = NP_TOK + NS_TOK
GRP = 4096
assert NP_TOK == GRP and DEC_SEQ == GRP

MXU_TILE = 256
GATE_LANES = 128
CHUNK = 256
HALO = 16

NGATE = 2 * 2 * HEADS

FFN_TM = 512
PRE_TM = 2 * CHUNK
OUT_TM = 512
POOL_TM = 512
VMEM_LIMIT = 56 * 1024 * 1024


def _silu(x):
    return x * jax.nn.sigmoid(x)


def _dot(a, b):
    return jnp.dot(a, b, preferred_element_type=F32)


def _dot_nt(a, b):
    return lax.dot_general(a, b, (((1,), (1,)), ((), ())), preferred_element_type=F32)


def _dot_tn(a, b):
    return lax.dot_general(a, b, (((0,), (0,)), ((), ())), preferred_element_type=F32)


def _resident(shape, index_map):
    return pl.BlockSpec(shape, index_map, pipeline_mode=pl.Buffered(1))


def _params(n_axes):
    return pltpu.CompilerParams(
        dimension_semantics=("arbitrary",) * n_axes, vmem_limit_bytes=VMEM_LIMIT)


def _modulate(x, g, mod_ref, sub):
    ms = jnp.mean(x * x, axis=-1, keepdims=True)
    gain = g * (1.0 + mod_ref[3 * sub + 1:3 * sub + 2, :])
    return x * lax.rsqrt(ms + EPS) * gain + mod_ref[3 * sub:3 * sub + 1, :]


def _split_dot(a, x, terms, left=True):
    parts = []
    r = x
    for _ in range(terms):
        p = r.astype(BF16)
        parts.append(p)
        r = r - p.astype(F32)
    acc = None
    for p in reversed(parts):
        t = _dot(a, p) if left else _dot(p, a)
        acc = t if acc is None else acc + t
    return acc


def _adaln_kernel(cond_ref, w_ref, b_ref, o_ref):
    s = _silu(cond_ref[...]).astype(BF16)
    o_ref[...] = _dot(s, w_ref[...].astype(BF16)) + b_ref[...]


def _adaln(cond8, ada_w, ada_b):
    return pl.pallas_call(
        _adaln_kernel,
        grid=(DEPTH, 3 * N_SUB),
        in_specs=[
            pl.BlockSpec((8, D), lambda l, j: (0, 0)),
            pl.BlockSpec((None, D, D), lambda l, j: (l, 0, j)),
            pl.BlockSpec((None, None, 1, D), lambda l, j: (l, j, 0, 0)),
        ],
        out_specs=pl.BlockSpec((None, None, 8, D), lambda l, j: (l, j, 0, 0)),
        out_shape=jax.ShapeDtypeStruct((DEPTH, 3 * N_SUB, 8, D), F32),
        compiler_params=_params(2),
        name="adaln",
    )(cond8, ada_w, ada_b.reshape(DEPTH, 3 * N_SUB, 1, D))


def _ffn_kernel(*refs, layer, sub, split_in, emit_h, final):
    n_x = 2 if split_in else 1
    x_refs, (mod_ref, g_ref, win_ref, wout_ref, fg_ref), o_refs = refs[:n_x], refs[n_x:n_x + 5], refs[n_x + 5:]
    is_prompt = pl.program_id(0) < NP_TOK // FFN_TM
    x = jnp.where(is_prompt, x_refs[0][...], x_refs[1][...]) if split_in else x_refs[0][...]
    h = _modulate(x, g_ref[layer, sub:sub + 1, :], mod_ref, sub).astype(BF16)
    acc = jnp.zeros(x.shape, F32)
    for c in range(D_FF // MXU_TILE):
        lo = c * MXU_TILE
        a = _dot(h, win_ref[:, lo:lo + MXU_TILE].astype(BF16))
        b = _dot(h, win_ref[:, D_FF + lo:D_FF + lo + MXU_TILE].astype(BF16))
        u = (_silu(a) * b).astype(BF16)
        acc = acc + _dot(u, wout_ref[lo:lo + MXU_TILE, :].astype(BF16))
    out = x + (0.5 * mod_ref[3 * sub + 2:3 * sub + 3, :]) * acc
    if not final:
        o_refs[0][...] = out
        if emit_h:
            o_refs[1][...] = _modulate(out, g_ref[layer, 1:2, :], mod_ref, 1).astype(BF16)
        return
    ms = jnp.mean(out * out, axis=-1, keepdims=True)
    out = out * lax.rsqrt(ms + EPS) * fg_ref[...]

    @pl.when(is_prompt)
    def _():
        o_refs[0][...] = out

    @pl.when(jnp.logical_not(is_prompt))
    def _():
        o_refs[1][...] = out


def _ffn(xs, mods, norm_g, w_in, w_out, final_g, *, layer, sub, which, final=False):
    tm = FFN_TM
    npt = NP_TOK // tm
    prompt_spec = pl.BlockSpec((tm, D), lambda i: (jnp.minimum(i, npt - 1), 0))
    sample_spec = pl.BlockSpec((tm, D), lambda i: (jnp.maximum(i - npt, 0), 0))
    whole_spec = pl.BlockSpec((tm, D), lambda i: (i, 0))
    split_in = len(xs) == 2
    emit_h = sub == 0
    if final:
        out_specs = [prompt_spec, sample_spec]
        out_shape = [jax.ShapeDtypeStruct((NP_TOK, D), F32), jax.ShapeDtypeStruct((NS_TOK, D), F32)]
    elif emit_h:
        out_specs = [whole_spec, whole_spec]
        out_shape = [jax.ShapeDtypeStruct((NTOK, D), F32), jax.ShapeDtypeStruct((NTOK, D), BF16)]
    else:
        out_specs = whole_spec
        out_shape = jax.ShapeDtypeStruct((NTOK, D), F32)
    return pl.pallas_call(
        functools.partial(_ffn_kernel, layer=layer, sub=sub, split_in=split_in, emit_h=emit_h,
                          final=final),
        grid=(NTOK // tm,),
        in_specs=([prompt_spec, sample_spec] if split_in else [whole_spec]) + [
            pl.BlockSpec((None, None, 3 * N_SUB, D), lambda i: (layer, (i * tm) // GRP, 0, 0)),
            _resident((DEPTH, N_SUB, D), lambda i: (0, 0, 0)),
            _resident((None, None, D, 2 * D_FF), lambda i: (layer, which, 0, 0)),
            _resident((None, None, D_FF, D), lambda i: (layer, which, 0, 0)),
            _resident((1, D), lambda i: (0, 0)),
        ],
        out_specs=out_specs,
        out_shape=out_shape,
        compiler_params=_params(1),
        name=f"ffn_l{layer}_{which}",
    )(*xs, mods, norm_g, w_in, w_out, final_g)


def _mlstm_pre_kernel(hp_ref, hc_ref, hn_ref, wup_ref, cw_ref, cb_ref, wqk_ref, wv_ref, wg_ref,
                      bg_ref, q_ref, k_ref, v_ref, xco_ref, z_ref, gate_ref, xm_s, acc_s):
    tm = PRE_TM
    nrow = tm + 2 * HALO
    i = pl.program_id(0)
    he = jnp.concatenate([hp_ref[...], hc_ref[...], hn_ref[...]], axis=0)
    xm_s[...] = _dot(he, wup_ref[:, :INNER].astype(BF16))
    z_ref[...] = _dot(hc_ref[...], wup_ref[:, INNER:].astype(BF16)).astype(BF16)

    is_prompt = i < NP_TOK // tm
    seqlen = jnp.where(is_prompt, SEQ, DEC_SEQ)
    starts_seq = ((i * tm) & (seqlen - 1)) == 0
    ends_seq = ((i * tm + tm) & (seqlen - 1)) == 0
    xm_s[0:HALO, :] = jnp.where(starts_seq, 0.0, xm_s[0:HALO, :])
    xm_s[HALO + tm:, :] = jnp.where(ends_seq, 0.0, xm_s[HALO + tm:, :])
    xm_all = xm_s[...]
    taps = []
    for j in range(CONV_K):
        off = j - CONV_K // 2
        taps.append(xm_all if off == 0 else pltpu.roll(xm_all, (nrow - off) % nrow, axis=0))

    def conv_rows(lo, hi, at_seq_end):
        acc = cb_ref[...]
        for j in range(CONV_K):
            off = j - CONV_K // 2
            term = taps[j][HALO + lo:HALO + hi]
            if at_seq_end and off != 0:
                pos = (i * tm + lo + lax.broadcasted_iota(jnp.int32, (hi - lo, 1), 0)) & (seqlen - 1)
                inside = (pos + off >= 0) if off < 0 else (pos + off < seqlen)
                term = jnp.where(inside, term, 0.0)
            acc = acc + term * cw_ref[j]
        return acc

    edge = 8
    lo = 0
    for b in range(SEQ, tm, SEQ):
        acc_s[lo:b - edge, :] = conv_rows(lo, b - edge, False)
        acc_s[b - edge:b + edge, :] = conv_rows(b - edge, b + edge, True)
        lo = b + edge
    acc_s[lo:tm, :] = conv_rows(lo, tm, False)
    xc = _silu(acc_s[...])
    xco_ref[...] = xc.astype(BF16)

    for t in range(INNER // MXU_TILE):
        cols = slice(t * MXU_TILE, (t + 1) * MXU_TILE)
        qk = _dot(xc[:, cols].astype(BF16), wqk_ref[t])
        q_ref[:, cols] = qk[:, :MXU_TILE].astype(BF16)
        k_ref[:, cols] = (qk[:, MXU_TILE:] / math.sqrt(DH)).astype(BF16)
        xm = xm_s[pl.ds(HALO, tm), cols].astype(BF16)
        v_ref[:, cols] = _dot(xm, wv_ref[t]).astype(BF16)

    g = (_dot(q_ref[...], wg_ref[0:INNER, :]) + _dot(k_ref[...], wg_ref[INNER:2 * INNER, :])
         + _dot(v_ref[...], wg_ref[2 * INNER:, :]) + bg_ref[...])
    lane = lax.broadcasted_iota(jnp.int32, g.shape, 1)
    logsig = jnp.minimum(g, 0.0) - jnp.log1p(jnp.exp(-jnp.abs(g)))
    gl = jnp.where(lane < 2 * NGATE, jnp.where((lane & HEADS) != 0, logsig, g), 0.0)
    tri_prefix = jnp.where(_tri(True), 1.0, 0.0).astype(BF16)
    lane_c = lax.broadcasted_iota(jnp.int32, (CHUNK, GATE_LANES), 1)
    for c in range(tm // CHUNK):
        rows = slice(c * CHUNK, (c + 1) * CHUNK)
        gl_c = gl[rows]
        prefix = _split_dot(tri_prefix, gl_c, 3)
        suffix = prefix[CHUNK - 1:CHUNK, :] - prefix + gl_c
        cum = jnp.where((lane_c & (2 * HEADS)) == 0, prefix, suffix)
        gate_ref[rows, :] = jnp.where(lane_c < NGATE, gl_c, cum)


def _mlstm_pre(hm, wup, conv_w, conv_b, wqk, wv, wg, bg, *, a):
    tm = PRE_TM
    nhalo = NTOK // HALO
    act = jax.ShapeDtypeStruct((NTOK, INNER), BF16)
    act_spec = pl.BlockSpec((tm, INNER), lambda i: (i, 0))
    return pl.pallas_call(
        _mlstm_pre_kernel,
        grid=(NTOK // tm,),
        in_specs=[
            pl.BlockSpec((HALO, D), lambda i: (jnp.maximum(i * (tm // HALO) - 1, 0), 0)),
            pl.BlockSpec((tm, D), lambda i: (i, 0)),
            pl.BlockSpec((HALO, D), lambda i: (jnp.minimum((i + 1) * (tm // HALO), nhalo - 1), 0)),
            _resident((None, D, 2 * INNER), lambda i: (a, 0, 0)),
            _resident((None, CONV_K, 1, INNER), lambda i: (a, 0, 0, 0)),
            _resident((None, 1, INNER), lambda i: (a, 0, 0)),
            _resident((None, INNER // MXU_TILE, MXU_TILE, 2 * MXU_TILE), lambda i: (a, 0, 0, 0)),
            _resident((None, INNER // MXU_TILE, MXU_TILE, MXU_TILE), lambda i: (a, 0, 0, 0)),
            _resident((None, 3 * INNER, GATE_LANES), lambda i: (a, 0, 0)),
            _resident((None, 1, GATE_LANES), lambda i: (a, 0, 0)),
        ],
        out_specs=[act_spec] * 5 + [pl.BlockSpec((tm, GATE_LANES), lambda i: (i, 0))],
        out_shape=[act] * 5 + [jax.ShapeDtypeStruct((NTOK, GATE_LANES), F32)],
        scratch_shapes=[pltpu.VMEM((tm + 2 * HALO, INNER), F32), pltpu.VMEM((tm, INNER), F32)],
        compiler_params=_params(1),
        name="mlstm_pre",
    )(hm, hm, hm, wup, conv_w, conv_b, wqk, wv, wg, bg)


def _tri(lower):
    r = lax.broadcasted_iota(jnp.int32, (CHUNK, CHUNK), 0)
    c = lax.broadcasted_iota(jnp.int32, (CHUNK, CHUNK), 1)
    return (c <= r) if lower else (c >= r)


def _pick_col(x, col):
    lane = lax.broadcasted_iota(jnp.int32, x.shape, 1)
    return jnp.sum(jnp.where(lane == col, x, 0.0), axis=1, keepdims=True)


def _pick_row(x, row):
    if isinstance(row, int):
        return x[row:row + 1, :]
    sub = lax.broadcasted_iota(jnp.int32, x.shape, 0)
    return jnp.sum(jnp.where(sub == row, x, 0.0), axis=0, keepdims=True)


def _unit_gates(gc, gt, head, lower, m0):
    d = 0 if lower else 1
    col_i = d * 2 * HEADS + head
    col_f = NGATE + col_i + HEADS
    bc = _pick_col(gc, col_f)
    ic = _pick_col(gc, col_i)
    br = _pick_row(gt, col_f)
    ir = _pick_row(gt, col_i)
    log_d = jnp.where(_tri(lower), bc - br + ir, -jnp.inf)
    inter = bc + m0
    mt = jnp.maximum(inter, jnp.max(log_d, axis=-1, keepdims=True))
    decay = jnp.exp(log_d - mt)
    ws = jnp.exp(inter - mt)
    b_last = bc[CHUNK - 1:CHUNK, :] if lower else bc[0:1, :]
    g = b_last - bc + ic
    m_new = jnp.maximum(b_last + m0, jnp.max(g, axis=0, keepdims=True))
    w = jnp.exp(g - m_new)
    w_row = jnp.exp(b_last - br + ir - m_new)
    dec = jnp.exp(b_last + m0 - m_new)
    return decay, mt, ws, w, m_new, dec, w_row


def _unit_apply(q, k, v, qk, gates, state):
    decay, mt, ws, w, _, dec, w_row = gates
    s = qk * decay
    num = _dot(s.astype(BF16), v)
    den = jnp.sum(s, axis=-1, keepdims=True)
    if state is not None:
        c0, n0 = state
        num = num + ws * _dot_nt(q, c0.astype(BF16))
        qn = _dot_nt(q, jnp.broadcast_to(n0, (GATE_LANES, DH)).astype(BF16))
        den = den + ws * qn[:, 0:1]
    h = num / jnp.maximum(jnp.abs(den), jnp.exp(-mt))
    c_new = _dot_tn((v.astype(F32) * w).astype(BF16), k)
    n_new = _dot(jnp.broadcast_to(w_row, (16, CHUNK)).astype(BF16), k)[0:1, :]
    if state is not None:
        c_new = dec * c0 + c_new
        n_new = dec * n0 + n_new
    return h, c_new, n_new


def _scan_prompt_kernel(q_ref, k_ref, v_ref, gc_ref, gt_ref, h_ref, c_ref, n_ref, m_ref):
    gc, gt = gc_ref[...], gt_ref[...]
    m0 = jnp.zeros((1, 1), F32)
    gates = [[_unit_gates(gc, gt, j, d == 0, m0) for d in range(2)] for j in range(HEADS)]
    for j in range(HEADS):
        cols = slice(j * DH, (j + 1) * DH)
        q, k, v = q_ref[:, cols], k_ref[:, cols], v_ref[:, cols]
        qk = _dot_nt(q, k)
        h_sum = None
        for d in range(2):
            h, c_new, n_new = _unit_apply(q, k, v, qk, gates[j][d], None)
            h_sum = h if h_sum is None else h_sum + h
            c_ref[d, j] = c_new
            n_ref[d, j:j + 1, :] = n_new
            m_ref[d, j:j + 1, :] = jnp.broadcast_to(gates[j][d][4], (1, GATE_LANES))
        h_ref[:, cols] = h_sum.astype(BF16)


def _scan_prompt(q, k, v, gates, gates_t):
    act_spec = pl.BlockSpec((CHUNK, INNER), lambda b: (b, 0))
    return pl.pallas_call(
        _scan_prompt_kernel,
        grid=(BATCH,),
        in_specs=[act_spec, act_spec, act_spec,
                  pl.BlockSpec((CHUNK, GATE_LANES), lambda b: (b, 0)),
                  pl.BlockSpec((2 * NGATE, CHUNK), lambda b: (0, b))],
        out_specs=[act_spec,
                   pl.BlockSpec((None, None, 2, HEADS, DH, DH), lambda b: (b, 0, 0, 0, 0, 0)),
                   pl.BlockSpec((None, None, 2, HEADS, DH), lambda b: (b, 0, 0, 0, 0)),
                   pl.BlockSpec((None, 2, HEADS, GATE_LANES), lambda b: (b, 0, 0, 0))],
        out_shape=[jax.ShapeDtypeStruct((NP_TOK, INNER), BF16),
                   jax.ShapeDtypeStruct((BATCH, 1, 2, HEADS, DH, DH), F32),
                   jax.ShapeDtypeStruct((BATCH, 1, 2, HEADS, DH), F32),
                   jax.ShapeDtypeStruct((BATCH, 2, HEADS, GATE_LANES), F32)],
        compiler_params=_params(1),
        name="scan_prompt",
    )(q, k, v, gates, gates_t)


def _scan_sample_kernel(qf_ref, kf_ref, vf_ref, gcf_ref, gtf_ref, qb_ref, kb_ref, vb_ref, gcb_ref,
                        gtb_ref, c0_ref, n0_ref, m0_ref, hf_ref, hb_ref, c_s, n_s, m_s):
    @pl.when(pl.program_id(1) == 0)
    def _():
        c_s[...] = c0_ref[...]
        n_s[...] = n0_ref[...]
        m_s[...] = m0_ref[...]

    units = ((qf_ref, kf_ref, vf_ref, gcf_ref, gtf_ref, hf_ref), (qb_ref, kb_ref, vb_ref, gcb_ref, gtb_ref, hb_ref))
    gates = [[_unit_gates(u[3][...], u[4][...], j, d == 0, m_s[d, j:j + 1, 0:1]) for d, u in enumerate(units)]
             for j in range(HEADS)]
    for j in range(HEADS):
        cols = slice(j * DH, (j + 1) * DH)
        for d, (q_ref, k_ref, v_ref, _, _, h_ref) in enumerate(units):
            q, k, v = q_ref[:, cols], k_ref[:, cols], v_ref[:, cols]
            qk = _dot_nt(q, k)
            state = (c_s[d, j], n_s[d, j:j + 1, :])
            h, c_new, n_new = _unit_apply(q, k, v, qk, gates[j][d], state)
            h_ref[:, cols] = h.astype(BF16)
            c_s[d, j] = c_new
            n_s[d, j:j + 1, :] = n_new
            m_s[d, j:j + 1, :] = jnp.broadcast_to(gates[j][d][4], (1, GATE_LANES))


def _scan_sample(q, k, v, gates, gates_t, c0, n0, m0, *, a):
    nc = DEC_SEQ // CHUNK
    base = NP_TOK // CHUNK

    def fwd(b, c):
        return b * nc + c

    def bwd(b, c):
        return b * nc + (nc - 1 - c)

    def specs(blk):
        act = pl.BlockSpec((CHUNK, INNER), lambda b, c: (base + blk(b, c), 0))
        return [act, act, act,
                pl.BlockSpec((CHUNK, GATE_LANES), lambda b, c: (base + blk(b, c), 0)),
                pl.BlockSpec((2 * NGATE, CHUNK), lambda b, c: (0, base + blk(b, c)))]

    return pl.pallas_call(
        _scan_sample_kernel,
        grid=(DEC_BATCH, nc),
        in_specs=specs(fwd) + specs(bwd) + [
            pl.BlockSpec((None, None, 2, HEADS, DH, DH), lambda b, c: (b, a, 0, 0, 0, 0)),
            pl.BlockSpec((None, None, 2, HEADS, DH), lambda b, c: (b, a, 0, 0, 0)),
            pl.BlockSpec((None, 2, HEADS, GATE_LANES), lambda b, c: (b, 0, 0, 0))],
        out_specs=[pl.BlockSpec((CHUNK, INNER), lambda b, c: (fwd(b, c), 0)),
                   pl.BlockSpec((CHUNK, INNER), lambda b, c: (bwd(b, c), 0))],
        out_shape=[jax.ShapeDtypeStruct((NS_TOK, INNER), BF16)] * 2,
        scratch_shapes=[pltpu.VMEM((2, HEADS, DH, DH), F32), pltpu.VMEM((2, HEADS, DH), F32),
                        pltpu.VMEM((2, HEADS, GATE_LANES), F32)],
        compiler_params=_params(2),
        name="scan_sample",
    )(q, k, v, gates, gates_t, q, k, v, gates, gates_t, c0, n0, m0)


def _mlstm_out_kernel(x_ref, hp_ref, hfs_ref, hbs_ref, xc_ref, z_ref, mod_ref, ng_ref, skip_ref, wd_ref,
                      o_ref):
    def tile(h_sum):
        y = None
        for h in range(HEADS):
            cols = slice(h * DH, (h + 1) * DH)
            seg = h_sum(cols)
            mu = jnp.mean(seg, axis=-1, keepdims=True)
            cen = seg - mu
            var = jnp.mean(cen * cen, axis=-1, keepdims=True)
            hn = cen * lax.rsqrt(var + EPS) * ng_ref[:, cols]
            out = (hn + skip_ref[:, cols] * xc_ref[:, cols].astype(F32)) * _silu(z_ref[:, cols].astype(F32))
            part = _dot(out.astype(BF16), wd_ref[cols, :])
            y = part if y is None else y + part
        o_ref[...] = x_ref[...] + mod_ref[5:6, :] * y

    is_prompt = pl.program_id(0) < NP_TOK // OUT_TM

    @pl.when(is_prompt)
    def _():
        tile(lambda cols: hp_ref[:, cols].astype(F32))

    @pl.when(jnp.logical_not(is_prompt))
    def _():
        tile(lambda cols: hfs_ref[:, cols].astype(F32) + hbs_ref[:, cols].astype(F32))


def _mlstm_out(x, h_p, hf_s, hb_s, xc, z, mods, norm_g, skip, w_down, *, layer, a):
    tm = OUT_TM
    npt = NP_TOK // tm
    act_spec = pl.BlockSpec((tm, INNER), lambda i: (i, 0))
    prompt_spec = pl.BlockSpec((tm, INNER), lambda i: (jnp.minimum(i, npt - 1), 0))
    sample_spec = pl.BlockSpec((tm, INNER), lambda i: (jnp.maximum(i - npt, 0), 0))
    return pl.pallas_call(
        _mlstm_out_kernel,
        grid=(NTOK // tm,),
        in_specs=[
            pl.BlockSpec((tm, D), lambda i: (i, 0)),
            prompt_spec, sample_spec, sample_spec, act_spec, act_spec,
            pl.BlockSpec((None, None, 3 * N_SUB, D), lambda i: (layer, (i * tm) // GRP, 0, 0)),
            _resident((None, 1, INNER), lambda i: (a, 0, 0)),
            _resident((None, 1, INNER), lambda i: (a, 0, 0)),
            _resident((None, INNER, D), lambda i: (a, 0, 0)),
        ],
        out_specs=pl.BlockSpec((tm, D), lambda i: (i, 0)),
        out_shape=jax.ShapeDtypeStruct((NTOK, D), F32),
        compiler_params=_params(1),
        name="mlstm_out",
    )(x, h_p, hf_s, hb_s, xc, z, mods, norm_g, skip, w_down)


def _window_count(idx, win, n):
    lo = np.clip(idx - win // 2, 0, n)
    hi = np.clip(idx - win // 2 + win, 0, n)
    return (hi - lo).astype(np.float32)


def _pool_finish(x_ref, o_ref, h_g, p, g, mod_ref, wp_ref, ps_ref, rows):
    cols = slice(g * POOL_GC, (g + 1) * POOL_GC)
    y = _dot((p - h_g).astype(BF16), wp_ref[g]) * ps_ref[:, cols]
    o_ref[rows, cols] = x_ref[rows, cols] + mod_ref[5:6, cols] * y


def _pool_prompt_kernel(x_ref, h_ref, mod_ref, a_ref, cnt_ref, wp_ref, ps_ref, o_ref):
    for g in range(len(POOL_WINDOWS)):
        for s in range(POOL_TM // SEQ):
            rows = slice(s * SEQ, (s + 1) * SEQ)
            h_g = h_ref[rows, g * POOL_GC:(g + 1) * POOL_GC]
            p = _dot(a_ref[g], h_g) / cnt_ref[g]
            _pool_finish(x_ref, o_ref, h_g.astype(F32), p, g, mod_ref, wp_ref, ps_ref, rows)


def _pool_sample_kernel(xc_ref, hp_ref, hc_ref, hn_ref, mod_ref, a_ref, crow_ref, ccol_ref, wp_ref,
                        ps_ref, o_ref, hext):
    tm = POOL_TM
    tiles_per_seq = DEC_SEQ // tm
    j = pl.program_id(0) % tiles_per_seq
    hext[0:tm, :] = jnp.where(j > 0, hp_ref[...].astype(F32), 0.0)
    hext[tm:2 * tm, :] = hc_ref[...].astype(F32)
    hext[2 * tm:3 * tm, :] = jnp.where(j < tiles_per_seq - 1, hn_ref[...].astype(F32), 0.0)
    all_rows = slice(0, tm)
    for g, win in enumerate(POOL_WINDOWS):
        cols = slice(g * POOL_GC, (g + 1) * POOL_GC)
        acc = None
        for dr in range(-(win // 2), win - win // 2):
            term = hext[pl.ds(tm + GRID_W * dr, tm), cols]
            acc = term if acc is None else acc + term
        p_rows = acc / crow_ref[g]
        p = _dot(a_ref[g], p_rows.astype(BF16)) / ccol_ref[g]
        _pool_finish(xc_ref, o_ref, hext[tm:2 * tm, cols], p, g, mod_ref, wp_ref, ps_ref, all_rows)


def _band(n, win):
    t = np.arange(n)
    lo = np.clip(t - win // 2, 0, n)
    hi = np.clip(t - win // 2 + win, 0, n)
    u = np.arange(n)
    return ((u[None, :] >= lo[:, None]) & (u[None, :] < hi[:, None])).astype(np.float32)


def _pool(x, hm, mods, pool_w, pool_scale, *, layer, p):
    tm = POOL_TM
    a1 = jnp.asarray(np.stack([_band(SEQ, w) for w in POOL_WINDOWS]), BF16)
    a2 = jnp.asarray(np.stack([np.kron(np.eye(tm // GRID_W, dtype=np.float32), _band(GRID_W, w))
                               for w in POOL_WINDOWS]), BF16)
    ngroups = len(POOL_WINDOWS)
    base = NP_TOK // tm
    per_seq = DEC_SEQ // tm
    last = NTOK // tm - 1
    tok = np.arange(DEC_SEQ)
    cnt1 = jnp.asarray(np.stack([_window_count(np.arange(SEQ), w, SEQ) for w in POOL_WINDOWS])[..., None])
    crow = np.stack([_window_count(tok // GRID_W, w, DEC_SEQ // GRID_W) for w in POOL_WINDOWS])
    crow = jnp.asarray(crow.reshape(ngroups, per_seq, tm, 1).transpose(1, 0, 2, 3))
    ccol = jnp.asarray(np.stack([_window_count(tok[:tm] % GRID_W, w, GRID_W) for w in POOL_WINDOWS])[..., None])

    wp_spec = _resident((None, ngroups, POOL_GC, POOL_GC), lambda i: (p, 0, 0, 0))
    ps_spec = _resident((None, 1, D), lambda i: (p, 0, 0))
    tile = pl.BlockSpec((tm, D), lambda i: (i, 0))
    out_p = pl.pallas_call(
        _pool_prompt_kernel,
        grid=(NP_TOK // tm,),
        in_specs=[tile, tile,
                  pl.BlockSpec((None, None, 3 * N_SUB, D), lambda i: (layer, 0, 0, 0)),
                  _resident((ngroups, SEQ, SEQ), lambda i: (0, 0, 0)),
                  _resident((ngroups, SEQ, 1), lambda i: (0, 0, 0)), wp_spec, ps_spec],
        out_specs=tile,
        out_shape=jax.ShapeDtypeStruct((NP_TOK, D), F32),
        compiler_params=_params(1),
        name="pool_prompt",
    )(x, hm, mods, a1, cnt1, pool_w, pool_scale)

    out_s = pl.pallas_call(
        _pool_sample_kernel,
        grid=(NS_TOK // tm,),
        in_specs=[pl.BlockSpec((tm, D), lambda i: (base + i, 0)),
                  pl.BlockSpec((tm, D), lambda i: (jnp.maximum(base + i - 1, base), 0)),
                  pl.BlockSpec((tm, D), lambda i: (base + i, 0)),
                  pl.BlockSpec((tm, D), lambda i: (jnp.minimum(base + i + 1, last), 0)),
                  pl.BlockSpec((None, None, 3 * N_SUB, D), lambda i: (layer, 1 + i // per_seq, 0, 0)),
                  _resident((ngroups, tm, tm), lambda i: (0, 0, 0)),
                  pl.BlockSpec((None, ngroups, tm, 1), lambda i: (i % per_seq, 0, 0, 0)),
                  _resident((ngroups, tm, 1), lambda i: (0, 0, 0)), wp_spec, ps_spec],
        out_specs=tile,
        out_shape=jax.ShapeDtypeStruct((NS_TOK, D), F32),
        scratch_shapes=[pltpu.VMEM((3 * tm, D), F32)],
        compiler_params=_params(1),
        name="pool_sample",
    )(x, hm, hm, hm, mods, a2, crow, ccol, pool_w, pool_scale)
    return out_p, out_s


def _block_diag_tiles(w):
    na = w.shape[0]
    ntile = INNER // MXU_TILE
    rows = jnp.swapaxes(w, -1, -2).reshape(na, ntile, MXU_TILE, QKV_BLOCK)
    rows = jnp.tile(rows, (1, 1, 1, MXU_TILE // QKV_BLOCK))
    blk = np.arange(MXU_TILE) // QKV_BLOCK
    mask = jnp.asarray(blk[:, None] == blk[None, :])
    return jnp.where(mask, rows, 0.0).astype(BF16)


def kernel(x_prompt, x_sample, state_C, state_n, state_m, c, c_ctx, ada_w, ada_b, norm_g, ffn_w_in,
           ffn_w_out, mlstm_w_up, mlstm_conv_w, mlstm_conv_b, mlstm_w_q, mlstm_w_k, mlstm_w_v,
           mlstm_w_gate, mlstm_b_gate, mlstm_norm_g, mlstm_skip, mlstm_w_down, pool_w, pool_scale,
           final_g):
    na = mlstm_w_up.shape[0]
    xs = (x_prompt.reshape(NP_TOK, D), x_sample.reshape(NS_TOK, D))
    cond8 = jnp.concatenate([c_ctx[None, :], c, jnp.zeros((8 - 1 - DEC_BATCH, D), F32)], axis=0)
    mods = _adaln(cond8, ada_w, ada_b)
    mods = jnp.transpose(mods[:, :, :1 + DEC_BATCH], (0, 2, 1, 3))

    w_in = ffn_w_in
    w_out = ffn_w_out
    fg = final_g.reshape(1, D)
    wup = mlstm_w_up
    wqk = jnp.concatenate([_block_diag_tiles(mlstm_w_q), _block_diag_tiles(mlstm_w_k)], axis=-1)
    wv = _block_diag_tiles(mlstm_w_v)
    lane_pad = ((0, 0), (0, 0), (0, GATE_LANES - 2 * NGATE))
    wg = jnp.pad(jnp.tile(mlstm_w_gate.reshape(na, 3 * INNER, NGATE), (1, 1, 2)), lane_pad).astype(BF16)
    bg = jnp.pad(jnp.tile(mlstm_b_gate.reshape(na, 1, NGATE), (1, 1, 2)), lane_pad)
    conv_b = mlstm_conv_b.reshape(na, 1, INNER)
    mng = mlstm_norm_g.reshape(na, 1, INNER)
    skip = mlstm_skip.reshape(na, 1, INNER)
    wdown = mlstm_w_down.astype(BF16)
    pw = pool_w.astype(BF16)
    ps = pool_scale.reshape(-1, 1, D)

    new_c, new_n, new_m = [], [], []
    for l in range(DEPTH):
        x, hm = _ffn(xs if l == 0 else (x,), mods, norm_g, w_in, w_out, fg, layer=l, sub=0, which=0)
        if l % 2 == 0:
            a = l // 2
            q, k, v, xc, z, gates = _mlstm_pre(hm, wup, mlstm_conv_w, conv_b, wqk, wv, wg, bg, a=a)
            gates_t = jnp.transpose(gates[:, :2 * NGATE])
            h_p, c_new, n_new, m_new = _scan_prompt(q, k, v, gates, gates_t)
            m0 = jnp.broadcast_to(state_m[:, a, :, :, None], (DEC_BATCH, 2, HEADS, GATE_LANES))
            hf_s, hb_s = _scan_sample(q, k, v, gates, gates_t, state_C, state_n, m0, a=a)
            x = _mlstm_out(x, h_p, hf_s, hb_s, xc, z, mods, mng, skip, wdown, layer=l, a=a)
            new_c.append(c_new)
            new_n.append(n_new)
            new_m.append(m_new[:, None, :, :, 0])
            x = (x,)
        else:
            x = _pool(x, hm, mods, pw, ps, layer=l, p=l // 2)
        x = _ffn(x, mods, norm_g, w_in, w_out, fg, layer=l, sub=2, which=1, final=(l == DEPTH - 1))
    y_prompt = x[0].reshape(BATCH, SEQ, D)
    y_sample = x[1].reshape(DEC_BATCH, DEC_SEQ, D)
    return (y_prompt, y_sample, jnp.concatenate(new_c, axis=1), jnp.concatenate(new_n, axis=1),
            jnp.concatenate(new_m, axis=1))
```

```python
import functools
import math

import numpy as np
import jax
import jax.numpy as jnp
from jax import lax
from jax.experimental import pallas as pl
from jax.experimental.pallas import tpu as pltpu

F32 = jnp.float32
BF16 = jnp.bfloat16

D = 1024
BATCH = 16
SEQ = 256
DEPTH = 2
DEC_BATCH = 2
DEC_SEQ = 4096
GRID_W = 64
N_SUB = 3
INNER = 2 * D
HEADS = 4
DH = INNER // HEADS
QKV_BLOCK = 4
CONV_K = 5
POOL_WINDOWS = (2, 4, 8, 16)
POOL_GC = D // len(POOL_WINDOWS)
D_FF = 2816
EPS = 1e-6

NP_TOK = BATCH * SEQ
NS_TOK = DEC_BATCH * DEC_SEQ
NTOK = NP_TOK + NS_TOK
GRP = 4096
assert NP_TOK == GRP and DEC_SEQ == GRP

MXU_TILE = 256
GATE_LANES = 128
CHUNK = 256
HALO = 16

NGATE = 2 * 2 * HEADS

FFN_TM = 512
PRE_TM = 2 * CHUNK
OUT_TM = 512
POOL_TM = 512
VMEM_LIMIT = 56 * 1024 * 1024


def _silu(x):
    return x * jax.nn.sigmoid(x)


def _dot(a, b):
    return jnp.dot(a, b, preferred_element_type=F32)


def _dot_nt(a, b):
    return lax.dot_general(a, b, (((1,), (1,)), ((), ())), preferred_element_type=F32)


def _dot_tn(a, b):
    return lax.dot_general(a, b, (((0,), (0,)), ((), ())), preferred_element_type=F32)


def _resident(shape, index_map):
    return pl.BlockSpec(shape, index_map, pipeline_mode=pl.Buffered(1))


def _params(n_axes):
    return pltpu.CompilerParams(
        dimension_semantics=("arbitrary",) * n_axes, vmem_limit_bytes=VMEM_LIMIT)


def _modulate(x, g, mod_ref, sub):
    ms = jnp.mean(x * x, axis=-1, keepdims=True)
    gain = g * (1.0 + mod_ref[3 * sub + 1:3 * sub + 2, :])
    return x * lax.rsqrt(ms + EPS) * gain + mod_ref[3 * sub:3 * sub + 1, :]


def _split_dot(a, x, terms, left=True):
    parts = []
    r = x
    for _ in range(terms):
        p = r.astype(BF16)
        parts.append(p)
        r = r - p.astype(F32)
    acc = None
    for p in reversed(parts):
        t = _dot(a, p) if left else _dot(p, a)
        acc = t if acc is None else acc + t
    return acc


def _adaln_kernel(cond_ref, w_ref, b_ref, o_ref):
    s = _silu(cond_ref[...]).astype(BF16)
    o_ref[...] = _dot(s, w_ref[...].astype(BF16)) + b_ref[...]


def _adaln(cond8, ada_w, ada_b):
    return pl.pallas_call(
        _adaln_kernel,
        grid=(DEPTH, 3 * N_SUB),
        in_specs=[
            pl.BlockSpec((8, D), lambda l, j: (0, 0)),
            pl.BlockSpec((None, D, D), lambda l, j: (l, 0, j)),
            pl.BlockSpec((None, None, 1, D), lambda l, j: (l, j, 0, 0)),
        ],
        out_specs=pl.BlockSpec((None, None, 8, D), lambda l, j: (l, j, 0, 0)),
        out_shape=jax.ShapeDtypeStruct((DEPTH, 3 * N_SUB, 8, D), F32),
        compiler_params=_params(2),
        name="adaln",
    )(cond8, ada_w, ada_b.reshape(DEPTH, 3 * N_SUB, 1, D))


def _ffn_kernel(*refs, layer, sub, split_in, emit_h, final):
    n_x = 2 if split_in else 1
    x_refs, (mod_ref, g_ref, win_ref, wout_ref, fg_ref), o_refs = refs[:n_x], refs[n_x:n_x + 5], refs[n_x + 5:]
    is_prompt = pl.program_id(0) < NP_TOK // FFN_TM
    x = jnp.where(is_prompt, x_refs[0][...], x_refs[1][...]) if split_in else x_refs[0][...]
    h = _modulate(x, g_ref[layer, sub:sub + 1, :], mod_ref, sub).astype(BF16)
    acc = jnp.zeros(x.shape, F32)
    for c in range(D_FF // MXU_TILE):
        lo = c * MXU_TILE
        a = _dot(h, win_ref[:, lo:lo + MXU_TILE].astype(BF16))
        b = _dot(h, win_ref[:, D_FF + lo:D_FF + lo + MXU_TILE].astype(BF16))
        u = (_silu(a) * b).astype(BF16)
        acc = acc + _dot(u, wout_ref[lo:lo + MXU_TILE, :].astype(BF16))
    out = x + (0.5 * mod_ref[3 * sub + 2:3 * sub + 3, :]) * acc
    if not final:
        o_refs[0][...] = out
        if emit_h:
            o_refs[1][...] = _modulate(out, g_ref[layer, 1:2, :], mod_ref, 1).astype(BF16)
        return
    ms = jnp.mean(out * out, axis=-1, keepdims=True)
    out = out * lax.rsqrt(ms + EPS) * fg_ref[...]

    @pl.when(is_prompt)
    def _():
        o_refs[0][...] = out

    @pl.when(jnp.logical_not(is_prompt))
    def _():
        o_refs[1][...] = out


def _ffn(xs, mods, norm_g, w_in, w_out, final_g, *, layer, sub, which, final=False):
    tm = FFN_TM
    npt = NP_TOK // tm
    prompt_spec = pl.BlockSpec((tm, D), lambda i: (jnp.minimum(i, npt - 1), 0))
    sample_spec = pl.BlockSpec((tm, D), lambda i: (jnp.maximum(i - npt, 0), 0))
    whole_spec = pl.BlockSpec((tm, D), lambda i: (i, 0))
    split_in = len(xs) == 2
    emit_h = sub == 0
    if final:
        out_specs = [prompt_spec, sample_spec]
        out_shape = [jax.ShapeDtypeStruct((NP_TOK, D), F32), jax.ShapeDtypeStruct((NS_TOK, D), F32)]
    elif emit_h:
        out_specs = [whole_spec, whole_spec]
        out_shape = [jax.ShapeDtypeStruct((NTOK, D), F32), jax.ShapeDtypeStruct((NTOK, D), BF16)]
    else:
        out_specs = whole_spec
        out_shape = jax.ShapeDtypeStruct((NTOK, D), F32)
    return pl.pallas_call(
        functools.partial(_ffn_kernel, layer=layer, sub=sub, split_in=split_in, emit_h=emit_h,
                          final=final),
        grid=(NTOK // tm,),
        in_specs=([prompt_spec, sample_spec] if split_in else [whole_spec]) + [
            pl.BlockSpec((None, None, 3 * N_SUB, D), lambda i: (layer, (i * tm) // GRP, 0, 0)),
            _resident((DEPTH, N_SUB, D), lambda i: (0, 0, 0)),
            _resident((None, None, D, 2 * D_FF), lambda i: (layer, which, 0, 0)),
            _resident((None, None, D_FF, D), lambda i: (layer, which, 0, 0)),
            _resident((1, D), lambda i: (0, 0)),
        ],
        out_specs=out_specs,
        out_shape=out_shape,
        compiler_params=_params(1),
        name=f"ffn_l{layer}_{which}",
    )(*xs, mods, norm_g, w_in, w_out, final_g)


def _mlstm_pre_kernel(hp_ref, hc_ref, hn_ref, wup_ref, cw_ref, cb_ref, wqk_ref, wv_ref, wg_ref,
                      bg_ref, q_ref, k_ref, v_ref, xco_ref, z_ref, gate_ref, gate_t_ref, xm_s, acc_s):
    tm = PRE_TM
    nrow = tm + 2 * HALO
    i = pl.program_id(0)
    he = jnp.concatenate([hp_ref[...], hc_ref[...], hn_ref[...]], axis=0)
    xm_s[...] = _dot(he, wup_ref[:, :INNER].astype(BF16))
    z_ref[...] = _dot(hc_ref[...], wup_ref[:, INNER:].astype(BF16)).astype(BF16)

    is_prompt = i < NP_TOK // tm
    seqlen = jnp.where(is_prompt, SEQ, DEC_SEQ)
    starts_seq = ((i * tm) & (seqlen - 1)) == 0
    ends_seq = ((i * tm + tm) & (seqlen - 1)) == 0
    xm_s[0:HALO, :] = jnp.where(starts_seq, 0.0, xm_s[0:HALO, :])
    xm_s[HALO + tm:, :] = jnp.where(ends_seq, 0.0, xm_s[HALO + tm:, :])
    xm_all = xm_s[...]
    taps = []
    for j in range(CONV_K):
        off = j - CONV_K // 2
        taps.append(xm_all if off == 0 else pltpu.roll(xm_all, (nrow - off) % nrow, axis=0))

    def conv_rows(lo, hi, at_seq_end):
        acc = cb_ref[...]
        for j in range(CONV_K):
            off = j - CONV_K // 2
            term = taps[j][HALO + lo:HALO + hi]
            if at_seq_end and off != 0:
                pos = (i * tm + lo + lax.broadcasted_iota(jnp.int32, (hi - lo, 1), 0)) & (seqlen - 1)
                inside = (pos + off >= 0) if off < 0 else (pos + off < seqlen)
                term = jnp.where(inside, term, 0.0)
            acc = acc + term * cw_ref[j]
        return acc

    edge = 8
    lo = 0
    for b in range(SEQ, tm, SEQ):
        acc_s[lo:b - edge, :] = conv_rows(lo, b - edge, False)
        acc_s[b - edge:b + edge, :] = conv_rows(b - edge, b + edge, True)
        lo = b + edge
    acc_s[lo:tm, :] = conv_rows(lo, tm, False)
    xc = _silu(acc_s[...])
    xco_ref[...] = xc.astype(BF16)

    for t in range(INNER // MXU_TILE):
        cols = slice(t * MXU_TILE, (t + 1) * MXU_TILE)
        qk = _dot(xc[:, cols].astype(BF16), wqk_ref[t])
        q_ref[:, cols] = qk[:, :MXU_TILE].astype(BF16)
        k_ref[:, cols] = (qk[:, MXU_TILE:] / math.sqrt(DH)).astype(BF16)
        xm = xm_s[pl.ds(HALO, tm), cols].astype(BF16)
        v_ref[:, cols] = _dot(xm, wv_ref[t]).astype(BF16)

    g = (_dot(q_ref[...], wg_ref[0:INNER, :]) + _dot(k_ref[...], wg_ref[INNER:2 * INNER, :])
         + _dot(v_ref[...], wg_ref[2 * INNER:, :]) + bg_ref[...])
    lane = lax.broadcasted_iota(jnp.int32, g.shape, 1)
    logsig = jnp.minimum(g, 0.0) - jnp.log1p(jnp.exp(-jnp.abs(g)))
    gl = jnp.where(lane < 2 * NGATE, jnp.where((lane & HEADS) != 0, logsig, g), 0.0)
    tri_prefix = jnp.where(_tri(True), 1.0, 0.0).astype(BF16)
    lane_c = lax.broadcasted_iota(jnp.int32, (CHUNK, GATE_LANES), 1)
    for c in range(tm // CHUNK):
        rows = slice(c * CHUNK, (c + 1) * CHUNK)
        gl_c = gl[rows]
        prefix = _split_dot(tri_prefix, gl_c, 3)
        suffix = prefix[CHUNK - 1:CHUNK, :] - prefix + gl_c
        cum = jnp.where((lane_c & (2 * HEADS)) == 0, prefix, suffix)
        gates = jnp.where(lane_c < NGATE, gl_c, cum)
        gate_ref[rows, :] = gates
        gate_t_ref[:, rows] = jnp.transpose(gates)[0:2 * NGATE, :]


def _mlstm_pre(hm, wup, conv_w, conv_b, wqk, wv, wg, bg, *, a):
    tm = PRE_TM
    nhalo = NTOK // HALO
    act = jax.ShapeDtypeStruct((NTOK, INNER), BF16)
    act_spec = pl.BlockSpec((tm, INNER), lambda i: (i, 0))
    return pl.pallas_call(
        _mlstm_pre_kernel,
        grid=(NTOK // tm,),
        in_specs=[
            pl.BlockSpec((HALO, D), lambda i: (jnp.maximum(i * (tm // HALO) - 1, 0), 0)),
            pl.BlockSpec((tm, D), lambda i: (i, 0)),
            pl.BlockSpec((HALO, D), lambda i: (jnp.minimum((i + 1) * (tm // HALO), nhalo - 1), 0)),
            _resident((None, D, 2 * INNER), lambda i: (a, 0, 0)),
            _resident((None, CONV_K, 1, INNER), lambda i: (a, 0, 0, 0)),
            _resident((None, 1, INNER), lambda i: (a, 0, 0)),
            _resident((None, INNER // MXU_TILE, MXU_TILE, 2 * MXU_TILE), lambda i: (a, 0, 0, 0)),
            _resident((None, INNER // MXU_TILE, MXU_TILE, MXU_TILE), lambda i: (a, 0, 0, 0)),
            _resident((None, 3 * INNER, GATE_LANES), lambda i: (a, 0, 0)),
            _resident((None, 1, GATE_LANES), lambda i: (a, 0, 0)),
        ],
        out_specs=[act_spec] * 5 + [pl.BlockSpec((tm, GATE_LANES), lambda i: (i, 0)),
                                    pl.BlockSpec((2 * NGATE, tm), lambda i: (0, i))],
        out_shape=[act] * 5 + [jax.ShapeDtypeStruct((NTOK, GATE_LANES), F32),
                               jax.ShapeDtypeStruct((2 * NGATE, NTOK), F32)],
        scratch_shapes=[pltpu.VMEM((tm + 2 * HALO, INNER), F32), pltpu.VMEM((tm, INNER), F32)],
        compiler_params=_params(1),
        name="mlstm_pre",
    )(hm, hm, hm, wup, conv_w, conv_b, wqk, wv, wg, bg)


def _tri(lower):
    r = lax.broadcasted_iota(jnp.int32, (CHUNK, CHUNK), 0)
    c = lax.broadcasted_iota(jnp.int32, (CHUNK, CHUNK), 1)
    return (c <= r) if lower else (c >= r)


def _pick_col(x, col):
    lane = lax.broadcasted_iota(jnp.int32, x.shape, 1)
    return jnp.sum(jnp.where(lane == col, x, 0.0), axis=1, keepdims=True)


def _pick_row(x, row):
    if isinstance(row, int):
        return x[row:row + 1, :]
    sub = lax.broadcasted_iota(jnp.int32, x.shape, 0)
    return jnp.sum(jnp.where(sub == row, x, 0.0), axis=0, keepdims=True)


def _unit_gates(gc, gt, head, lower, m0):
    d = 0 if lower else 1
    col_i = d * 2 * HEADS + head
    col_f = NGATE + col_i + HEADS
    bc = _pick_col(gc, col_f)
    ic = _pick_col(gc, col_i)
    br = _pick_row(gt, col_f)
    ir = _pick_row(gt, col_i)
    log_d = jnp.where(_tri(lower), bc - br + ir, -jnp.inf)
    inter = bc + m0
    mt = jnp.maximum(inter, jnp.max(log_d, axis=-1, keepdims=True))
    decay = jnp.exp(log_d - mt)
    ws = jnp.exp(inter - mt)
    b_last = bc[CHUNK - 1:CHUNK, :] if lower else bc[0:1, :]
    g = b_last - bc + ic
    m_new = jnp.maximum(b_last + m0, jnp.max(g, axis=0, keepdims=True))
    w = jnp.exp(g - m_new)
    w_row = jnp.exp(b_last - br + ir - m_new)
    dec = jnp.exp(b_last + m0 - m_new)
    return decay, mt, ws, w, m_new, dec, w_row


def _unit_apply(q, k, v, qk, gates, state):
    decay, mt, ws, w, _, dec, w_row = gates
    s = qk * decay
    num = _dot(s.astype(BF16), v)
    den = jnp.sum(s, axis=-1, keepdims=True)
    if state is not None:
        c0, n0 = state
        num = num + ws * _dot_nt(q, c0.astype(BF16))
        qn = _dot_nt(q, jnp.broadcast_to(n0, (GATE_LANES, DH)).astype(BF16))
        den = den + ws * qn[:, 0:1]
    h = num / jnp.maximum(jnp.abs(den), jnp.exp(-mt))
    c_new = _dot_tn((v.astype(F32) * w).astype(BF16), k)
    n_new = _dot(jnp.broadcast_to(w_row, (16, CHUNK)).astype(BF16), k)[0:1, :]
    if state is not None:
        c_new = dec * c0 + c_new
        n_new = dec * n0 + n_new
    return h, c_new, n_new


def _scan_prompt_kernel(q_ref, k_ref, v_ref, gc_ref, gt_ref, h_ref, c_ref, n_ref, m_ref):
    gc, gt = gc_ref[...], gt_ref[...]
    m0 = jnp.zeros((1, 1), F32)
    gates = [[_unit_gates(gc, gt, j, d == 0, m0) for d in range(2)] for j in range(HEADS)]
    for j in range(HEADS):
        cols = slice(j * DH, (j + 1) * DH)
        q, k, v = q_ref[:, cols], k_ref[:, cols], v_ref[:, cols]
        qk = _dot_nt(q, k)
        h_sum = None
        for d in range(2):
            h, c_new, n_new = _unit_apply(q, k, v, qk, gates[j][d], None)
            h_sum = h if h_sum is None else h_sum + h
            c_ref[d, j] = c_new
            n_ref[d, j:j + 1, :] = n_new
            m_ref[d, j:j + 1, :] = jnp.broadcast_to(gates[j][d][4], (1, GATE_LANES))
        h_ref[:, cols] = h_sum.astype(BF16)


def _scan_prompt(q, k, v, gates, gates_t):
    act_spec = pl.BlockSpec((CHUNK, INNER), lambda b: (b, 0))
    return pl.pallas_call(
        _scan_prompt_kernel,
        grid=(BATCH,),
        in_specs=[act_spec, act_spec, act_spec,
                  pl.BlockSpec((CHUNK, GATE_LANES), lambda b: (b, 0)),
                  pl.BlockSpec((2 * NGATE, CHUNK), lambda b: (0, b))],
        out_specs=[act_spec,
                   pl.BlockSpec((None, None, 2, HEADS, DH, DH), lambda b: (b, 0, 0, 0, 0, 0)),
                   pl.BlockSpec((None, None, 2, HEADS, DH), lambda b: (b, 0, 0, 0, 0)),
                   pl.BlockSpec((None, 2, HEADS, GATE_LANES), lambda b: (b, 0, 0, 0))],
        out_shape=[jax.ShapeDtypeStruct((NP_TOK, INNER), BF16),
                   jax.ShapeDtypeStruct((BATCH, 1, 2, HEADS, DH, DH), F32),
                   jax.ShapeDtypeStruct((BATCH, 1, 2, HEADS, DH), F32),
                   jax.ShapeDtypeStruct((BATCH, 2, HEADS, GATE_LANES), F32)],
        compiler_params=_params(1),
        name="scan_prompt",
    )(q, k, v, gates, gates_t)


def _scan_sample_kernel(qf_ref, kf_ref, vf_ref, gcf_ref, gtf_ref, qb_ref, kb_ref, vb_ref, gcb_ref,
                        gtb_ref, c0_ref, n0_ref, m0_ref, hf_ref, hb_ref, c_s, n_s, m_s):
    @pl.when(pl.program_id(1) == 0)
    def _():
        c_s[...] = c0_ref[...]
        n_s[...] = n0_ref[...]
        m_s[...] = m0_ref[...]

    units = ((qf_ref, kf_ref, vf_ref, gcf_ref, gtf_ref, hf_ref), (qb_ref, kb_ref, vb_ref, gcb_ref, gtb_ref, hb_ref))
    gates = [[_unit_gates(u[3][...], u[4][...], j, d == 0, m_s[d, j:j + 1, 0:1]) for d, u in enumerate(units)]
             for j in range(HEADS)]
    for j in range(HEADS):
        cols = slice(j * DH, (j + 1) * DH)
        for d, (q_ref, k_ref, v_ref, _, _, h_ref) in enumerate(units):
            q, k, v = q_ref[:, cols], k_ref[:, cols], v_ref[:, cols]
            qk = _dot_nt(q, k)
            state = (c_s[d, j], n_s[d, j:j + 1, :])
            h, c_new, n_new = _unit_apply(q, k, v, qk, gates[j][d], state)
            h_ref[:, cols] = h.astype(BF16)
            c_s[d, j] = c_new
            n_s[d, j:j + 1, :] = n_new
            m_s[d, j:j + 1, :] = jnp.broadcast_to(gates[j][d][4], (1, GATE_LANES))


def _scan_sample(q, k, v, gates, gates_t, c0, n0, m0, *, a):
    nc = DEC_SEQ // CHUNK
    base = NP_TOK // CHUNK

    def fwd(b, c):
        return b * nc + c

    def bwd(b, c):
        return b * nc + (nc - 1 - c)

    def specs(blk):
        act = pl.BlockSpec((CHUNK, INNER), lambda b, c: (base + blk(b, c), 0))
        return [act, act, act,
                pl.BlockSpec((CHUNK, GATE_LANES), lambda b, c: (base + blk(b, c), 0)),
                pl.BlockSpec((2 * NGATE, CHUNK), lambda b, c: (0, base + blk(b, c)))]

    return pl.pallas_call(
        _scan_sample_kernel,
        grid=(DEC_BATCH, nc),
        in_specs=specs(fwd) + specs(bwd) + [
            pl.BlockSpec((None, None, 2, HEADS, DH, DH), lambda b, c: (b, a, 0, 0, 0, 0)),
            pl.BlockSpec((None, None, 2, HEADS, DH), lambda b, c: (b, a, 0, 0, 0)),
            pl.BlockSpec((None, 2, HEADS, GATE_LANES), lambda b, c: (b, 0, 0, 0))],
        out_specs=[pl.BlockSpec((CHUNK, INNER), lambda b, c: (fwd(b, c), 0)),
                   pl.BlockSpec((CHUNK, INNER), lambda b, c: (bwd(b, c), 0))],
        out_shape=[jax.ShapeDtypeStruct((NS_TOK, INNER), BF16)] * 2,
        scratch_shapes=[pltpu.VMEM((2, HEADS, DH, DH), F32), pltpu.VMEM((2, HEADS, DH), F32),
                        pltpu.VMEM((2, HEADS, GATE_LANES), F32)],
        compiler_params=_params(2),
        name="scan_sample",
    )(q, k, v, gates, gates_t, q, k, v, gates, gates_t, c0, n0, m0)


def _mlstm_out_kernel(x_ref, hp_ref, hfs_ref, hbs_ref, xc_ref, z_ref, mod_ref, ng_ref, skip_ref, wd_ref,
                      o_ref):
    def tile(h_sum):
        y = None
        for h in range(HEADS):
            cols = slice(h * DH, (h + 1) * DH)
            seg = h_sum(cols)
            mu = jnp.mean(seg, axis=-1, keepdims=True)
            cen = seg - mu
            var = jnp.mean(cen * cen, axis=-1, keepdims=True)
            hn = cen * lax.rsqrt(var + EPS) * ng_ref[:, cols]
            out = (hn + skip_ref[:, cols] * xc_ref[:, cols].astype(F32)) * _silu(z_ref[:, cols].astype(F32))
            part = _dot(out.astype(BF16), wd_ref[cols, :].astype(BF16))
            y = part if y is None else y + part
        o_ref[...] = x_ref[...] + mod_ref[5:6, :] * y

    is_prompt = pl.program_id(0) < NP_TOK // OUT_TM

    @pl.when(is_prompt)
    def _():
        tile(lambda cols: hp_ref[:, cols].astype(F32))

    @pl.when(jnp.logical_not(is_prompt))
    def _():
        tile(lambda cols: hfs_ref[:, cols].astype(F32) + hbs_ref[:, cols].astype(F32))


def _mlstm_out(x, h_p, hf_s, hb_s, xc, z, mods, norm_g, skip, w_down, *, layer, a):
    tm = OUT_TM
    npt = NP_TOK // tm
    act_spec = pl.BlockSpec((tm, INNER), lambda i: (i, 0))
    prompt_spec = pl.BlockSpec((tm, INNER), lambda i: (jnp.minimum(i, npt - 1), 0))
    sample_spec = pl.BlockSpec((tm, INNER), lambda i: (jnp.maximum(i - npt, 0), 0))
    return pl.pallas_call(
        _mlstm_out_kernel,
        grid=(NTOK // tm,),
        in_specs=[
            pl.BlockSpec((tm, D), lambda i: (i, 0)),
            prompt_spec, sample_spec, sample_spec, act_spec, act_spec,
            pl.BlockSpec((None, None, 3 * N_SUB, D), lambda i: (layer, (i * tm) // GRP, 0, 0)),
            _resident((None, 1, INNER), lambda i: (a, 0, 0)),
            _resident((None, 1, INNER), lambda i: (a, 0, 0)),
            _resident((None, INNER, D), lambda i: (a, 0, 0)),
        ],
        out_specs=pl.BlockSpec((tm, D), lambda i: (i, 0)),
        out_shape=jax.ShapeDtypeStruct((NTOK, D), F32),
        compiler_params=_params(1),
        name="mlstm_out",
    )(x, h_p, hf_s, hb_s, xc, z, mods, norm_g, skip, w_down)


def _window_count(idx, win, n):
    lo = np.clip(idx - win // 2, 0, n)
    hi = np.clip(idx - win // 2 + win, 0, n)
    return (hi - lo).astype(np.float32)


def _pool_finish(x_ref, o_ref, h_g, p, g, mod_ref, wp_ref, ps_ref, rows):
    cols = slice(g * POOL_GC, (g + 1) * POOL_GC)
    y = _dot((p - h_g).astype(BF16), wp_ref[g]) * ps_ref[:, cols]
    o_ref[rows, cols] = x_ref[rows, cols] + mod_ref[5:6, cols] * y


def _pool_prompt_kernel(x_ref, h_ref, mod_ref, a_ref, cnt_ref, wp_ref, ps_ref, o_ref):
    for g in range(len(POOL_WINDOWS)):
        for s in range(POOL_TM // SEQ):
            rows = slice(s * SEQ, (s + 1) * SEQ)
            h_g = h_ref[rows, g * POOL_GC:(g + 1) * POOL_GC]
            p = _dot(a_ref[g], h_g) / cnt_ref[g]
            _pool_finish(x_ref, o_ref, h_g.astype(F32), p, g, mod_ref, wp_ref, ps_ref, rows)


def _pool_sample_kernel(xc_ref, hp_ref, hc_ref, hn_ref, mod_ref, a_ref, crow_ref, ccol_ref, wp_ref,
                        ps_ref, o_ref, hext):
    tm = POOL_TM
    tiles_per_seq = DEC_SEQ // tm
    j = pl.program_id(0) % tiles_per_seq
    hext[0:tm, :] = jnp.where(j > 0, hp_ref[...].astype(F32), 0.0)
    hext[tm:2 * tm, :] = hc_ref[...].astype(F32)
    hext[2 * tm:3 * tm, :] = jnp.where(j < tiles_per_seq - 1, hn_ref[...].astype(F32), 0.0)
    all_rows = slice(0, tm)
    for g, win in enumerate(POOL_WINDOWS):
        cols = slice(g * POOL_GC, (g + 1) * POOL_GC)
        acc = None
        for dr in range(-(win // 2), win - win // 2):
            term = hext[pl.ds(tm + GRID_W * dr, tm), cols]
            acc = term if acc is None else acc + term
        p_rows = acc / crow_ref[g]
        p = _dot(a_ref[g], p_rows.astype(BF16)) / ccol_ref[g]
        _pool_finish(xc_ref, o_ref, hext[tm:2 * tm, cols], p, g, mod_ref, wp_ref, ps_ref, all_rows)


def _band(n, win):
    t = np.arange(n)
    lo = np.clip(t - win // 2, 0, n)
    hi = np.clip(t - win // 2 + win, 0, n)
    u = np.arange(n)
    return ((u[None, :] >= lo[:, None]) & (u[None, :] < hi[:, None])).astype(np.float32)


def _pool(x, hm, mods, pool_w, pool_scale, *, layer, p):
    tm = POOL_TM
    a1 = jnp.asarray(np.stack([_band(SEQ, w) for w in POOL_WINDOWS]), BF16)
    a2 = jnp.asarray(np.stack([np.kron(np.eye(tm // GRID_W, dtype=np.float32), _band(GRID_W, w))
                               for w in POOL_WINDOWS]), BF16)
    ngroups = len(POOL_WINDOWS)
    base = NP_TOK // tm
    per_seq = DEC_SEQ // tm
    last = NTOK // tm - 1
    tok = np.arange(DEC_SEQ)
    cnt1 = jnp.asarray(np.stack([_window_count(np.arange(SEQ), w, SEQ) for w in POOL_WINDOWS])[..., None])
    crow = np.stack([_window_count(tok // GRID_W, w, DEC_SEQ // GRID_W) for w in POOL_WINDOWS])
    crow = jnp.asarray(crow.reshape(ngroups, per_seq, tm, 1).transpose(1, 0, 2, 3))
    ccol = jnp.asarray(np.stack([_window_count(tok[:tm] % GRID_W, w, GRID_W) for w in POOL_WINDOWS])[..., None])

    wp_spec = _resident((None, ngroups, POOL_GC, POOL_GC), lambda i: (p, 0, 0, 0))
    ps_spec = _resident((None, 1, D), lambda i: (p, 0, 0))
    tile = pl.BlockSpec((tm, D), lambda i: (i, 0))
    out_p = pl.pallas_call(
        _pool_prompt_kernel,
        grid=(NP_TOK // tm,),
        in_specs=[tile, tile,
                  pl.BlockSpec((None, None, 3 * N_SUB, D), lambda i: (layer, 0, 0, 0)),
                  _resident((ngroups, SEQ, SEQ), lambda i: (0, 0, 0)),
                  _resident((ngroups, SEQ, 1), lambda i: (0, 0, 0)), wp_spec, ps_spec],
        out_specs=tile,
        out_shape=jax.ShapeDtypeStruct((NP_TOK, D), F32),
        compiler_params=_params(1),
        name="pool_prompt",
    )(x, hm, mods, a1, cnt1, pool_w, pool_scale)

    out_s = pl.pallas_call(
        _pool_sample_kernel,
        grid=(NS_TOK // tm,),
        in_specs=[pl.BlockSpec((tm, D), lambda i: (base + i, 0)),
                  pl.BlockSpec((tm, D), lambda i: (jnp.maximum(base + i - 1, base), 0)),
                  pl.BlockSpec((tm, D), lambda i: (base + i, 0)),
                  pl.BlockSpec((tm, D), lambda i: (jnp.minimum(base + i + 1, last), 0)),
                  pl.BlockSpec((None, None, 3 * N_SUB, D), lambda i: (layer, 1 + i // per_seq, 0, 0)),
                  _resident((ngroups, tm, tm), lambda i: (0, 0, 0)),
                  pl.BlockSpec((None, ngroups, tm, 1), lambda i: (i % per_seq, 0, 0, 0)),
                  _resident((ngroups, tm, 1), lambda i: (0, 0, 0)), wp_spec, ps_spec],
        out_specs=tile,
        out_shape=jax.ShapeDtypeStruct((NS_TOK, D), F32),
        scratch_shapes=[pltpu.VMEM((3 * tm, D), F32)],
        compiler_params=_params(1),
        name="pool_sample",
    )(x, hm, hm, hm, mods, a2, crow, ccol, pool_w, pool_scale)
    return out_p, out_s


def _block_diag_tiles(w):
    na = w.shape[0]
    ntile = INNER // MXU_TILE
    rows = jnp.swapaxes(w, -1, -2).reshape(na, ntile, MXU_TILE, QKV_BLOCK)
    rows = jnp.tile(rows, (1, 1, 1, MXU_TILE // QKV_BLOCK))
    blk = np.arange(MXU_TILE) // QKV_BLOCK
    mask = jnp.asarray(blk[:, None] == blk[None, :])
    return jnp.where(mask, rows, 0.0).astype(BF16)


def kernel(x_prompt, x_sample, state_C, state_n, state_m, c, c_ctx, ada_w, ada_b, norm_g, ffn_w_in,
           ffn_w_out, mlstm_w_up, mlstm_conv_w, mlstm_conv_b, mlstm_w_q, mlstm_w_k, mlstm_w_v,
           mlstm_w_gate, mlstm_b_gate, mlstm_norm_g, mlstm_skip, mlstm_w_down, pool_w, pool_scale,
           final_g):
    na = mlstm_w_up.shape[0]
    xs = (x_prompt.reshape(NP_TOK, D), x_sample.reshape(NS_TOK, D))
    cond8 = jnp.concatenate([c_ctx[None, :], c, jnp.zeros((8 - 1 - DEC_BATCH, D), F32)], axis=0)
    mods = _adaln(cond8, ada_w, ada_b)
    mods = jnp.transpose(mods[:, :, :1 + DEC_BATCH], (0, 2, 1, 3))

    w_in = ffn_w_in
    w_out = ffn_w_out
    fg = final_g.reshape(1, D)
    wup = mlstm_w_up
    wqk = jnp.concatenate([_block_diag_tiles(mlstm_w_q), _block_diag_tiles(mlstm_w_k)], axis=-1)
    wv = _block_diag_tiles(mlstm_w_v)
    lane_pad = ((0, 0), (0, 0), (0, GATE_LANES - 2 * NGATE))
    wg = jnp.pad(jnp.tile(mlstm_w_gate.reshape(na, 3 * INNER, NGATE), (1, 1, 2)), lane_pad).astype(BF16)
    bg = jnp.pad(jnp.tile(mlstm_b_gate.reshape(na, 1, NGATE), (1, 1, 2)), lane_pad)
    conv_b = mlstm_conv_b.reshape(na, 1, INNER)
    mng = mlstm_norm_g.reshape(na, 1, INNER)
    skip = mlstm_skip.reshape(na, 1, INNER)
    wdown = mlstm_w_down
    pw = pool_w.astype(BF16)
    ps = pool_scale.reshape(-1, 1, D)

    new_c, new_n, new_m = [], [], []
    for l in range(DEPTH):
        x, hm = _ffn(xs if l == 0 else (x,), mods, norm_g, w_in, w_out, fg, layer=l, sub=0, which=0)
        if l % 2 == 0:
            a = l // 2
            q, k, v, xc, z, gates, gates_t = _mlstm_pre(hm, wup, mlstm_conv_w, conv_b, wqk, wv, wg, bg, a=a)
            h_p, c_new, n_new, m_new = _scan_prompt(q, k, v, gates, gates_t)
            m0 = jnp.broadcast_to(state_m[:, a, :, :, None], (DEC_BATCH, 2, HEADS, GATE_LANES))
            hf_s, hb_s = _scan_sample(q, k, v, gates, gates_t, state_C, state_n, m0, a=a)
            x = _mlstm_out(x, h_p, hf_s, hb_s, xc, z, mods, mng, skip, wdown, layer=l, a=a)
            new_c.append(c_new)
            new_n.append(n_new)
            new_m.append(m_new[:, None, :, :, 0])
            x = (x,)
        else:
            x = _pool(x, hm, mods, pw, ps, layer=l, p=l // 2)
        x = _ffn(x, mods, norm_g, w_in, w_out, fg, layer=l, sub=2, which=1, final=(l == DEPTH - 1))
    y_prompt = x[0].reshape(BATCH, SEQ, D)
    y_sample = x[1].reshape(DEC_BATCH, DEC_SEQ, D)
    return (y_prompt, y_sample, jnp.concatenate(new_c, axis=1), jnp.concatenate(new_n, axis=1),
            jnp.concatenate(new_m, axis=1))
```

```python
import functools
import math

import numpy as np
import jax
import jax.numpy as jnp
from jax import lax
from jax.experimental import pallas as pl
from jax.experimental.pallas import tpu as pltpu

F32 = jnp.float32
BF16 = jnp.bfloat16

D = 1024
BATCH = 16
SEQ = 256
DEPTH = 2
DEC_BATCH = 2
DEC_SEQ = 4096
GRID_W = 64
N_SUB = 3
INNER = 2 * D
HEADS = 4
DH = INNER // HEADS
QKV_BLOCK = 4
CONV_K = 5
POOL_WINDOWS = (2, 4, 8, 16)
POOL_GC = D // len(POOL_WINDOWS)
D_FF = 2816
EPS = 1e-6

NP_TOK = BATCH * SEQ
NS_TOK = DEC_BATCH * DEC_SEQ
NTOK = NP_TOK + NS_TOK
GRP = 4096
N_GROUPS = 1 + DEC_BATCH
assert NP_TOK == GRP and DEC_SEQ == GRP

MXU_TILE = 256
GATE_LANES = 128
CHUNK = 256
HALO = 16

NGATE = 2 * 2 * HEADS

FFN_TM = 512
PRE_TM = 2 * CHUNK
OUT_TM = 512
POOL_TM = 512
VMEM_LIMIT = 56 * 1024 * 1024


def _silu(x):
    return x * jax.nn.sigmoid(x)


def _dot(a, b):
    return jnp.dot(a, b, preferred_element_type=F32)


def _dot_nt(a, b):
    return lax.dot_general(a, b, (((1,), (1,)), ((), ())), preferred_element_type=F32)


def _dot_tn(a, b):
    return lax.dot_general(a, b, (((0,), (0,)), ((), ())), preferred_element_type=F32)


def _resident(shape, index_map):
    return pl.BlockSpec(shape, index_map, pipeline_mode=pl.Buffered(1))


def _params(n_axes):
    return pltpu.CompilerParams(
        dimension_semantics=("arbitrary",) * n_axes, vmem_limit_bytes=VMEM_LIMIT)


def _modulate(x, g, mod_ref, sub):
    ms = jnp.mean(x * x, axis=-1, keepdims=True)
    gain = g * (1.0 + mod_ref[3 * sub + 1:3 * sub + 2, :])
    return x * lax.rsqrt(ms + EPS) * gain + mod_ref[3 * sub:3 * sub + 1, :]


def _split_dot(a, x, terms, left=True):
    parts = []
    r = x
    for _ in range(terms):
        p = r.astype(BF16)
        parts.append(p)
        r = r - p.astype(F32)
    acc = None
    for p in reversed(parts):
        t = _dot(a, p) if left else _dot(p, a)
        acc = t if acc is None else acc + t
    return acc


def _adaln_kernel(cond_ref, w_ref, b_ref, o_ref):
    l, j = pl.program_id(0), pl.program_id(1)
    s = _silu(cond_ref[...]).astype(BF16)
    res = _dot(s, w_ref[...].astype(BF16)) + b_ref[pl.ds(l, 1), :]
    for r in range(N_GROUPS):
        o_ref[r, pl.ds(j, 1), :] = res[r:r + 1, :]


def _adaln(cond8, ada_w, ada_b):
    return pl.pallas_call(
        _adaln_kernel,
        grid=(DEPTH, 3 * N_SUB),
        in_specs=[
            pl.BlockSpec((8, D), lambda l, j: (0, 0)),
            pl.BlockSpec((None, D, D), lambda l, j: (l, 0, j)),
            pl.BlockSpec((DEPTH, D), lambda l, j: (0, j)),
        ],
        out_specs=pl.BlockSpec((None, N_GROUPS, 3 * N_SUB, D), lambda l, j: (l, 0, 0, 0)),
        out_shape=jax.ShapeDtypeStruct((DEPTH, N_GROUPS, 3 * N_SUB, D), F32),
        compiler_params=_params(2),
        name="adaln",
    )(cond8, ada_w, ada_b)


def _ffn_kernel(*refs, layer, sub, split_in, emit_h, final):
    n_x = 2 if split_in else 1
    x_refs, (mod_ref, g_ref, win_ref, wout_ref, fg_ref), o_refs = refs[:n_x], refs[n_x:n_x + 5], refs[n_x + 5:]
    is_prompt = pl.program_id(0) < NP_TOK // FFN_TM
    x = jnp.where(is_prompt, x_refs[0][...], x_refs[1][...]) if split_in else x_refs[0][...]
    h = _modulate(x, g_ref[layer, sub:sub + 1, :], mod_ref, sub).astype(BF16)
    acc = jnp.zeros(x.shape, F32)
    for c in range(D_FF // MXU_TILE):
        lo = c * MXU_TILE
        a = _dot(h, win_ref[:, lo:lo + MXU_TILE].astype(BF16))
        b = _dot(h, win_ref[:, D_FF + lo:D_FF + lo + MXU_TILE].astype(BF16))
        u = (_silu(a) * b).astype(BF16)
        acc = acc + _dot(u, wout_ref[lo:lo + MXU_TILE, :].astype(BF16))
    out = x + (0.5 * mod_ref[3 * sub + 2:3 * sub + 3, :]) * acc
    if not final:
        o_refs[0][...] = out
        if emit_h:
            o_refs[1][...] = _modulate(out, g_ref[layer, 1:2, :], mod_ref, 1).astype(BF16)
        return
    ms = jnp.mean(out * out, axis=-1, keepdims=True)
    out = out * lax.rsqrt(ms + EPS) * fg_ref[...]

    @pl.when(is_prompt)
    def _():
        o_refs[0][...] = out

    @pl.when(jnp.logical_not(is_prompt))
    def _():
        o_refs[1][...] = out


def _ffn(xs, mods, norm_g, w_in, w_out, final_g, *, layer, sub, which, final=False):
    tm = FFN_TM
    npt = NP_TOK // tm
    prompt_spec = pl.BlockSpec((tm, D), lambda i: (jnp.minimum(i, npt - 1), 0))
    sample_spec = pl.BlockSpec((tm, D), lambda i: (jnp.maximum(i - npt, 0), 0))
    whole_spec = pl.BlockSpec((tm, D), lambda i: (i, 0))
    split_in = len(xs) == 2
    emit_h = sub == 0
    if final:
        out_specs = [prompt_spec, sample_spec]
        out_shape = [jax.ShapeDtypeStruct((NP_TOK, D), F32), jax.ShapeDtypeStruct((NS_TOK, D), F32)]
    elif emit_h:
        out_specs = [whole_spec, whole_spec]
        out_shape = [jax.ShapeDtypeStruct((NTOK, D), F32), jax.ShapeDtypeStruct((NTOK, D), BF16)]
    else:
        out_specs = whole_spec
        out_shape = jax.ShapeDtypeStruct((NTOK, D), F32)
    return pl.pallas_call(
        functools.partial(_ffn_kernel, layer=layer, sub=sub, split_in=split_in, emit_h=emit_h,
                          final=final),
        grid=(NTOK // tm,),
        in_specs=([prompt_spec, sample_spec] if split_in else [whole_spec]) + [
            pl.BlockSpec((None, None, 3 * N_SUB, D), lambda i: (layer, (i * tm) // GRP, 0, 0)),
            _resident((DEPTH, N_SUB, D), lambda i: (0, 0, 0)),
            _resident((None, None, D, 2 * D_FF), lambda i: (layer, which, 0, 0)),
            _resident((None, None, D_FF, D), lambda i: (layer, which, 0, 0)),
            _resident((1, D), lambda i: (0, 0)),
        ],
        out_specs=out_specs,
        out_shape=out_shape,
        compiler_params=_params(1),
        name=f"ffn_l{layer}_{which}",
    )(*xs, mods, norm_g, w_in, w_out, final_g)


def _mlstm_pre_kernel(hp_ref, hc_ref, hn_ref, wup_ref, cw_ref, cb_ref, wqkv_ref, wg_ref,
                      bg_ref, q_ref, k_ref, v_ref, xco_ref, z_ref, gate_ref, gate_t_ref, xm_s, acc_s):
    tm = PRE_TM
    nrow = tm + 2 * HALO
    i = pl.program_id(0)
    he = jnp.concatenate([hp_ref[...], hc_ref[...], hn_ref[...]], axis=0)
    xm_s[...] = _dot(he, wup_ref[:, :INNER].astype(BF16))
    z_ref[...] = _dot(hc_ref[...], wup_ref[:, INNER:].astype(BF16)).astype(BF16)

    is_prompt = i < NP_TOK // tm
    seqlen = jnp.where(is_prompt, SEQ, DEC_SEQ)
    starts_seq = ((i * tm) & (seqlen - 1)) == 0
    ends_seq = ((i * tm + tm) & (seqlen - 1)) == 0
    xm_s[0:HALO, :] = jnp.where(starts_seq, 0.0, xm_s[0:HALO, :])
    xm_s[HALO + tm:, :] = jnp.where(ends_seq, 0.0, xm_s[HALO + tm:, :])
    xm_all = xm_s[...]
    taps = []
    for j in range(CONV_K):
        off = j - CONV_K // 2
        taps.append(xm_all if off == 0 else pltpu.roll(xm_all, (nrow - off) % nrow, axis=0))

    def conv_rows(lo, hi, at_seq_end):
        acc = cb_ref[...]
        for j in range(CONV_K):
            off = j - CONV_K // 2
            term = taps[j][HALO + lo:HALO + hi]
            if at_seq_end and off != 0:
                pos = (i * tm + lo + lax.broadcasted_iota(jnp.int32, (hi - lo, 1), 0)) & (seqlen - 1)
                inside = (pos + off >= 0) if off < 0 else (pos + off < seqlen)
                term = jnp.where(inside, term, 0.0)
            acc = acc + term * cw_ref[j]
        return acc

    edge = 8
    lo = 0
    for b in range(SEQ, tm, SEQ):
        acc_s[lo:b - edge, :] = conv_rows(lo, b - edge, False)
        acc_s[b - edge:b + edge, :] = conv_rows(b - edge, b + edge, True)
        lo = b + edge
    acc_s[lo:tm, :] = conv_rows(lo, tm, False)
    xc = _silu(acc_s[...])
    xco_ref[...] = xc.astype(BF16)

    for t in range(INNER // MXU_TILE):
        cols = slice(t * MXU_TILE, (t + 1) * MXU_TILE)
        xc_t = xc[:, cols].astype(BF16)
        q_ref[:, cols] = _dot(xc_t, wqkv_ref[0, t]).astype(BF16)
        k_ref[:, cols] = (_dot(xc_t, wqkv_ref[1, t]) / math.sqrt(DH)).astype(BF16)
        xm = xm_s[pl.ds(HALO, tm), cols].astype(BF16)
        v_ref[:, cols] = _dot(xm, wqkv_ref[2, t]).astype(BF16)

    g = (_dot(q_ref[...], wg_ref[0:INNER, :]) + _dot(k_ref[...], wg_ref[INNER:2 * INNER, :])
         + _dot(v_ref[...], wg_ref[2 * INNER:, :]) + bg_ref[...])
    lane = lax.broadcasted_iota(jnp.int32, g.shape, 1)
    logsig = jnp.minimum(g, 0.0) - jnp.log1p(jnp.exp(-jnp.abs(g)))
    gl = jnp.where(lane < 2 * NGATE, jnp.where((lane & HEADS) != 0, logsig, g), 0.0)
    tri_prefix = jnp.where(_tri(True), 1.0, 0.0).astype(BF16)
    lane_c = lax.broadcasted_iota(jnp.int32, (CHUNK, GATE_LANES), 1)
    for c in range(tm // CHUNK):
        rows = slice(c * CHUNK, (c + 1) * CHUNK)
        gl_c = gl[rows]
        prefix = _split_dot(tri_prefix, gl_c, 3)
        suffix = prefix[CHUNK - 1:CHUNK, :] - prefix + gl_c
        cum = jnp.where((lane_c & (2 * HEADS)) == 0, prefix, suffix)
        gates = jnp.where(lane_c < NGATE, gl_c, cum)
        gate_ref[rows, :] = gates
        gate_t_ref[:, rows] = jnp.transpose(gates)[0:2 * NGATE, :]


def _mlstm_pre(hm, wup, conv_w, conv_b, wqkv, wg, bg, *, a):
    tm = PRE_TM
    nhalo = NTOK // HALO
    act = jax.ShapeDtypeStruct((NTOK, INNER), BF16)
    act_spec = pl.BlockSpec((tm, INNER), lambda i: (i, 0))
    return pl.pallas_call(
        _mlstm_pre_kernel,
        grid=(NTOK // tm,),
        in_specs=[
            pl.BlockSpec((HALO, D), lambda i: (jnp.maximum(i * (tm // HALO) - 1, 0), 0)),
            pl.BlockSpec((tm, D), lambda i: (i, 0)),
            pl.BlockSpec((HALO, D), lambda i: (jnp.minimum((i + 1) * (tm // HALO), nhalo - 1), 0)),
            _resident((None, D, 2 * INNER), lambda i: (a, 0, 0)),
            _resident((None, CONV_K, 1, INNER), lambda i: (a, 0, 0, 0)),
            _resident((None, 1, INNER), lambda i: (a, 0, 0)),
            _resident((None, 3, INNER // MXU_TILE, MXU_TILE, MXU_TILE), lambda i: (a, 0, 0, 0, 0)),
            _resident((None, 3 * INNER, GATE_LANES), lambda i: (a, 0, 0)),
            _resident((None, 1, GATE_LANES), lambda i: (a, 0, 0)),
        ],
        out_specs=[act_spec] * 5 + [pl.BlockSpec((tm, GATE_LANES), lambda i: (i, 0)),
                                    pl.BlockSpec((2 * NGATE, tm), lambda i: (0, i))],
        out_shape=[act] * 5 + [jax.ShapeDtypeStruct((NTOK, GATE_LANES), F32),
                               jax.ShapeDtypeStruct((2 * NGATE, NTOK), F32)],
        scratch_shapes=[pltpu.VMEM((tm + 2 * HALO, INNER), F32), pltpu.VMEM((tm, INNER), F32)],
        compiler_params=_params(1),
        name="mlstm_pre",
    )(hm, hm, hm, wup, conv_w, conv_b, wqkv, wg, bg)


def _tri(lower):
    r = lax.broadcasted_iota(jnp.int32, (CHUNK, CHUNK), 0)
    c = lax.broadcasted_iota(jnp.int32, (CHUNK, CHUNK), 1)
    return (c <= r) if lower else (c >= r)


def _pick_col(x, col):
    lane = lax.broadcasted_iota(jnp.int32, x.shape, 1)
    return jnp.sum(jnp.where(lane == col, x, 0.0), axis=1, keepdims=True)


def _pick_row(x, row):
    if isinstance(row, int):
        return x[row:row + 1, :]
    sub = lax.broadcasted_iota(jnp.int32, x.shape, 0)
    return jnp.sum(jnp.where(sub == row, x, 0.0), axis=0, keepdims=True)


def _unit_gates(gc, gt, head, lower, m0):
    d = 0 if lower else 1
    col_i = d * 2 * HEADS + head
    col_f = NGATE + col_i + HEADS
    bc = _pick_col(gc, col_f)
    ic = _pick_col(gc, col_i)
    br = _pick_row(gt, col_f)
    ir = _pick_row(gt, col_i)
    log_d = jnp.where(_tri(lower), bc - br + ir, -jnp.inf)
    inter = bc + m0
    mt = jnp.maximum(inter, jnp.max(log_d, axis=-1, keepdims=True))
    decay = jnp.exp(log_d - mt)
    ws = jnp.exp(inter - mt)
    b_last = bc[CHUNK - 1:CHUNK, :] if lower else bc[0:1, :]
    g = b_last - bc + ic
    m_new = jnp.maximum(b_last + m0, jnp.max(g, axis=0, keepdims=True))
    w = jnp.exp(g - m_new)
    w_row = jnp.exp(b_last - br + ir - m_new)
    dec = jnp.exp(b_last + m0 - m_new)
    return decay, mt, ws, w, m_new, dec, w_row


def _unit_apply(q, k, v, qk, gates, state):
    decay, mt, ws, w, _, dec, w_row = gates
    s = qk * decay
    num = _dot(s.astype(BF16), v)
    den = jnp.sum(s, axis=-1, keepdims=True)
    if state is not None:
        c0, n0 = state
        num = num + ws * _dot_nt(q, c0.astype(BF16))
        qn = _dot_nt(q, jnp.broadcast_to(n0, (GATE_LANES, DH)).astype(BF16))
        den = den + ws * qn[:, 0:1]
    h = num / jnp.maximum(jnp.abs(den), jnp.exp(-mt))
    c_new = _dot_tn((v.astype(F32) * w).astype(BF16), k)
    n_new = _dot(jnp.broadcast_to(w_row, (16, CHUNK)).astype(BF16), k)[0:1, :]
    if state is not None:
        c_new = dec * c0 + c_new
        n_new = dec * n0 + n_new
    return h, c_new, n_new


def _scan_prompt_kernel(q_ref, k_ref, v_ref, gc_ref, gt_ref, h_ref, c_ref, n_ref, m_ref):
    gc, gt = gc_ref[...], gt_ref[...]
    m0 = jnp.zeros((1, 1), F32)
    gates = [[_unit_gates(gc, gt, j, d == 0, m0) for d in range(2)] for j in range(HEADS)]
    for j in range(HEADS):
        cols = slice(j * DH, (j + 1) * DH)
        q, k, v = q_ref[:, cols], k_ref[:, cols], v_ref[:, cols]
        qk = _dot_nt(q, k)
        h_sum = None
        for d in range(2):
            h, c_new, n_new = _unit_apply(q, k, v, qk, gates[j][d], None)
            h_sum = h if h_sum is None else h_sum + h
            c_ref[d, j] = c_new
            n_ref[d, j:j + 1, :] = n_new
            m_ref[d, j:j + 1, :] = jnp.broadcast_to(gates[j][d][4], (1, GATE_LANES))
        h_ref[:, cols] = h_sum.astype(BF16)


def _scan_prompt(q, k, v, gates, gates_t):
    act_spec = pl.BlockSpec((CHUNK, INNER), lambda b: (b, 0))
    return pl.pallas_call(
        _scan_prompt_kernel,
        grid=(BATCH,),
        in_specs=[act_spec, act_spec, act_spec,
                  pl.BlockSpec((CHUNK, GATE_LANES), lambda b: (b, 0)),
                  pl.BlockSpec((2 * NGATE, CHUNK), lambda b: (0, b))],
        out_specs=[act_spec,
                   pl.BlockSpec((None, None, 2, HEADS, DH, DH), lambda b: (b, 0, 0, 0, 0, 0)),
                   pl.BlockSpec((None, None, 2, HEADS, DH), lambda b: (b, 0, 0, 0, 0)),
                   pl.BlockSpec((None, 2, HEADS, GATE_LANES), lambda b: (b, 0, 0, 0))],
        out_shape=[jax.ShapeDtypeStruct((NP_TOK, INNER), BF16),
                   jax.ShapeDtypeStruct((BATCH, 1, 2, HEADS, DH, DH), F32),
                   jax.ShapeDtypeStruct((BATCH, 1, 2, HEADS, DH), F32),
                   jax.ShapeDtypeStruct((BATCH, 2, HEADS, GATE_LANES), F32)],
        compiler_params=_params(1),
        name="scan_prompt",
    )(q, k, v, gates, gates_t)


def _scan_sample_kernel(qf_ref, kf_ref, vf_ref, gcf_ref, gtf_ref, qb_ref, kb_ref, vb_ref, gcb_ref,
                        gtb_ref, c0_ref, n0_ref, m0_ref, hf_ref, hb_ref, c_s, n_s, m_s):
    @pl.when(pl.program_id(1) == 0)
    def _():
        c_s[...] = c0_ref[...]
        n_s[...] = n0_ref[...]
        m_s[...] = m0_ref[...]

    units = ((qf_ref, kf_ref, vf_ref, gcf_ref, gtf_ref, hf_ref), (qb_ref, kb_ref, vb_ref, gcb_ref, gtb_ref, hb_ref))
    gates = [[_unit_gates(u[3][...], u[4][...], j, d == 0, m_s[d, j:j + 1, 0:1]) for d, u in enumerate(units)]
             for j in range(HEADS)]
    for j in range(HEADS):
        cols = slice(j * DH, (j + 1) * DH)
        for d, (q_ref, k_ref, v_ref, _, _, h_ref) in enumerate(units):
            q, k, v = q_ref[:, cols], k_ref[:, cols], v_ref[:, cols]
            qk = _dot_nt(q, k)
            state = (c_s[d, j], n_s[d, j:j + 1, :])
            h, c_new, n_new = _unit_apply(q, k, v, qk, gates[j][d], state)
            h_ref[:, cols] = h.astype(BF16)
            c_s[d, j] = c_new
            n_s[d, j:j + 1, :] = n_new
            m_s[d, j:j + 1, :] = jnp.broadcast_to(gates[j][d][4], (1, GATE_LANES))


def _scan_sample(q, k, v, gates, gates_t, c0, n0, m0, *, a):
    nc = DEC_SEQ // CHUNK
    base = NP_TOK // CHUNK

    def fwd(b, c):
        return b * nc + c

    def bwd(b, c):
        return b * nc + (nc - 1 - c)

    def specs(blk):
        act = pl.BlockSpec((CHUNK, INNER), lambda b, c: (base + blk(b, c), 0))
        return [act, act, act,
                pl.BlockSpec((CHUNK, GATE_LANES), lambda b, c: (base + blk(b, c), 0)),
                pl.BlockSpec((2 * NGATE, CHUNK), lambda b, c: (0, base + blk(b, c)))]

    return pl.pallas_call(
        _scan_sample_kernel,
        grid=(DEC_BATCH, nc),
        in_specs=specs(fwd) + specs(bwd) + [
            pl.BlockSpec((None, None, 2, HEADS, DH, DH), lambda b, c: (b, a, 0, 0, 0, 0)),
            pl.BlockSpec((None, None, 2, HEADS, DH), lambda b, c: (b, a, 0, 0, 0)),
            pl.BlockSpec((None, 2, HEADS, GATE_LANES), lambda b, c: (b, 0, 0, 0))],
        out_specs=[pl.BlockSpec((CHUNK, INNER), lambda b, c: (fwd(b, c), 0)),
                   pl.BlockSpec((CHUNK, INNER), lambda b, c: (bwd(b, c), 0))],
        out_shape=[jax.ShapeDtypeStruct((NS_TOK, INNER), BF16)] * 2,
        scratch_shapes=[pltpu.VMEM((2, HEADS, DH, DH), F32), pltpu.VMEM((2, HEADS, DH), F32),
                        pltpu.VMEM((2, HEADS, GATE_LANES), F32)],
        compiler_params=_params(2),
        name="scan_sample",
    )(q, k, v, gates, gates_t, q, k, v, gates, gates_t, c0, n0, m0)


def _mlstm_out_kernel(x_ref, hp_ref, hfs_ref, hbs_ref, xc_ref, z_ref, mod_ref, ng_ref, skip_ref, wd_ref,
                      o_ref):
    def tile(h_sum):
        y = None
        for h in range(HEADS):
            cols = slice(h * DH, (h + 1) * DH)
            seg = h_sum(cols)
            mu = jnp.mean(seg, axis=-1, keepdims=True)
            cen = seg - mu
            var = jnp.mean(cen * cen, axis=-1, keepdims=True)
            hn = cen * lax.rsqrt(var + EPS) * ng_ref[:, cols]
            out = (hn + skip_ref[:, cols] * xc_ref[:, cols].astype(F32)) * _silu(z_ref[:, cols].astype(F32))
            part = _dot(out.astype(BF16), wd_ref[cols, :].astype(BF16))
            y = part if y is None else y + part
        o_ref[...] = x_ref[...] + mod_ref[5:6, :] * y

    is_prompt = pl.program_id(0) < NP_TOK // OUT_TM

    @pl.when(is_prompt)
    def _():
        tile(lambda cols: hp_ref[:, cols].astype(F32))

    @pl.when(jnp.logical_not(is_prompt))
    def _():
        tile(lambda cols: hfs_ref[:, cols].astype(F32) + hbs_ref[:, cols].astype(F32))


def _mlstm_out(x, h_p, hf_s, hb_s, xc, z, mods, norm_g, skip, w_down, *, layer, a):
    tm = OUT_TM
    npt = NP_TOK // tm
    act_spec = pl.BlockSpec((tm, INNER), lambda i: (i, 0))
    prompt_spec = pl.BlockSpec((tm, INNER), lambda i: (jnp.minimum(i, npt - 1), 0))
    sample_spec = pl.BlockSpec((tm, INNER), lambda i: (jnp.maximum(i - npt, 0), 0))
    return pl.pallas_call(
        _mlstm_out_kernel,
        grid=(NTOK // tm,),
        in_specs=[
            pl.BlockSpec((tm, D), lambda i: (i, 0)),
            prompt_spec, sample_spec, sample_spec, act_spec, act_spec,
            pl.BlockSpec((None, None, 3 * N_SUB, D), lambda i: (layer, (i * tm) // GRP, 0, 0)),
            _resident((None, 1, INNER), lambda i: (a, 0, 0)),
            _resident((None, 1, INNER), lambda i: (a, 0, 0)),
            _resident((None, INNER, D), lambda i: (a, 0, 0)),
        ],
        out_specs=pl.BlockSpec((tm, D), lambda i: (i, 0)),
        out_shape=jax.ShapeDtypeStruct((NTOK, D), F32),
        compiler_params=_params(1),
        name="mlstm_out",
    )(x, h_p, hf_s, hb_s, xc, z, mods, norm_g, skip, w_down)


def _window_count(idx, win, n):
    lo = np.clip(idx - win // 2, 0, n)
    hi = np.clip(idx - win // 2 + win, 0, n)
    return (hi - lo).astype(np.float32)


def _pool_finish(x_ref, o_ref, h_g, p, g, mod_ref, wp_ref, ps_ref, rows):
    cols = slice(g * POOL_GC, (g + 1) * POOL_GC)
    y = _dot((p - h_g).astype(BF16), wp_ref[g]) * ps_ref[:, cols]
    o_ref[rows, cols] = x_ref[rows, cols] + mod_ref[5:6, cols] * y


def _pool_prompt_kernel(x_ref, h_ref, mod_ref, a_ref, cnt_ref, wp_ref, ps_ref, o_ref):
    for g in range(len(POOL_WINDOWS)):
        for s in range(POOL_TM // SEQ):
            rows = slice(s * SEQ, (s + 1) * SEQ)
            h_g = h_ref[rows, g * POOL_GC:(g + 1) * POOL_GC]
            p = _dot(a_ref[g], h_g) / cnt_ref[g]
            _pool_finish(x_ref, o_ref, h_g.astype(F32), p, g, mod_ref, wp_ref, ps_ref, rows)


def _pool_sample_kernel(xc_ref, hp_ref, hc_ref, hn_ref, mod_ref, a_ref, crow_ref, ccol_ref, wp_ref,
                        ps_ref, o_ref, hext):
    tm = POOL_TM
    tiles_per_seq = DEC_SEQ // tm
    j = pl.program_id(0) % tiles_per_seq
    hext[0:tm, :] = jnp.where(j > 0, hp_ref[...].astype(F32), 0.0)
    hext[tm:2 * tm, :] = hc_ref[...].astype(F32)
    hext[2 * tm:3 * tm, :] = jnp.where(j < tiles_per_seq - 1, hn_ref[...].astype(F32), 0.0)
    all_rows = slice(0, tm)
    for g, win in enumerate(POOL_WINDOWS):
        cols = slice(g * POOL_GC, (g + 1) * POOL_GC)
        acc = None
        for dr in range(-(win // 2), win - win // 2):
            term = hext[pl.ds(tm + GRID_W * dr, tm), cols]
            acc = term if acc is None else acc + term
        p_rows = acc / crow_ref[g]
        p = _dot(a_ref[g], p_rows.astype(BF16)) / ccol_ref[g]
        _pool_finish(xc_ref, o_ref, hext[tm:2 * tm, cols], p, g, mod_ref, wp_ref, ps_ref, all_rows)


def _band(n, win):
    t = np.arange(n)
    lo = np.clip(t - win // 2, 0, n)
    hi = np.clip(t - win // 2 + win, 0, n)
    u = np.arange(n)
    return ((u[None, :] >= lo[:, None]) & (u[None, :] < hi[:, None])).astype(np.float32)


def _pool(x, hm, mods, pool_w, pool_scale, *, layer, p):
    tm = POOL_TM
    a1 = jnp.asarray(np.stack([_band(SEQ, w) for w in POOL_WINDOWS]), BF16)
    a2 = jnp.asarray(np.stack([np.kron(np.eye(tm // GRID_W, dtype=np.float32), _band(GRID_W, w))
                               for w in POOL_WINDOWS]), BF16)
    ngroups = len(POOL_WINDOWS)
    base = NP_TOK // tm
    per_seq = DEC_SEQ // tm
    last = NTOK // tm - 1
    tok = np.arange(DEC_SEQ)
    cnt1 = jnp.asarray(np.stack([_window_count(np.arange(SEQ), w, SEQ) for w in POOL_WINDOWS])[..., None])
    crow = np.stack([_window_count(tok // GRID_W, w, DEC_SEQ // GRID_W) for w in POOL_WINDOWS])
    crow = jnp.asarray(crow.reshape(ngroups, per_seq, tm, 1).transpose(1, 0, 2, 3))
    ccol = jnp.asarray(np.stack([_window_count(tok[:tm] % GRID_W, w, GRID_W) for w in POOL_WINDOWS])[..., None])

    wp_spec = _resident((None, ngroups, POOL_GC, POOL_GC), lambda i: (p, 0, 0, 0))
    ps_spec = _resident((None, 1, D), lambda i: (p, 0, 0))
    tile = pl.BlockSpec((tm, D), lambda i: (i, 0))
    out_p = pl.pallas_call(
        _pool_prompt_kernel,
        grid=(NP_TOK // tm,),
        in_specs=[tile, tile,
                  pl.BlockSpec((None, None, 3 * N_SUB, D), lambda i: (layer, 0, 0, 0)),
                  _resident((ngroups, SEQ, SEQ), lambda i: (0, 0, 0)),
                  _resident((ngroups, SEQ, 1), lambda i: (0, 0, 0)), wp_spec, ps_spec],
        out_specs=tile,
        out_shape=jax.ShapeDtypeStruct((NP_TOK, D), F32),
        compiler_params=_params(1),
        name="pool_prompt",
    )(x, hm, mods, a1, cnt1, pool_w, pool_scale)

    out_s = pl.pallas_call(
        _pool_sample_kernel,
        grid=(NS_TOK // tm,),
        in_specs=[pl.BlockSpec((tm, D), lambda i: (base + i, 0)),
                  pl.BlockSpec((tm, D), lambda i: (jnp.maximum(base + i - 1, base), 0)),
                  pl.BlockSpec((tm, D), lambda i: (base + i, 0)),
                  pl.BlockSpec((tm, D), lambda i: (jnp.minimum(base + i + 1, last), 0)),
                  pl.BlockSpec((None, None, 3 * N_SUB, D), lambda i: (layer, 1 + i // per_seq, 0, 0)),
                  _resident((ngroups, tm, tm), lambda i: (0, 0, 0)),
                  pl.BlockSpec((None, ngroups, tm, 1), lambda i: (i % per_seq, 0, 0, 0)),
                  _resident((ngroups, tm, 1), lambda i: (0, 0, 0)), wp_spec, ps_spec],
        out_specs=tile,
        out_shape=jax.ShapeDtypeStruct((NS_TOK, D), F32),
        scratch_shapes=[pltpu.VMEM((3 * tm, D), F32)],
        compiler_params=_params(1),
        name="pool_sample",
    )(x, hm, hm, hm, mods, a2, crow, ccol, pool_w, pool_scale)
    return out_p, out_s


def _block_diag_tiles(w_q, w_k, w_v):
    w = jnp.stack([w_q, w_k, w_v], axis=1)
    na = w.shape[0]
    ntile = INNER // MXU_TILE
    rows = jnp.swapaxes(w, -1, -2).reshape(na, 3, ntile, MXU_TILE, QKV_BLOCK)
    rows = jnp.tile(rows, (1, 1, 1, 1, MXU_TILE // QKV_BLOCK))
    blk = np.arange(MXU_TILE) // QKV_BLOCK
    mask = jnp.asarray(blk[:, None] == blk[None, :])
    return jnp.where(mask, rows, 0.0).astype(BF16)


def kernel(x_prompt, x_sample, state_C, state_n, state_m, c, c_ctx, ada_w, ada_b, norm_g, ffn_w_in,
           ffn_w_out, mlstm_w_up, mlstm_conv_w, mlstm_conv_b, mlstm_w_q, mlstm_w_k, mlstm_w_v,
           mlstm_w_gate, mlstm_b_gate, mlstm_norm_g, mlstm_skip, mlstm_w_down, pool_w, pool_scale,
           final_g):
    na = mlstm_w_up.shape[0]
    xs = (x_prompt.reshape(NP_TOK, D), x_sample.reshape(NS_TOK, D))
    cond8 = jnp.concatenate([c_ctx[None, :], c, jnp.zeros((8 - 1 - DEC_BATCH, D), F32)], axis=0)
    mods = _adaln(cond8, ada_w, ada_b)

    w_in = ffn_w_in
    w_out = ffn_w_out
    fg = final_g.reshape(1, D)
    wup = mlstm_w_up
    wqkv = _block_diag_tiles(mlstm_w_q, mlstm_w_k, mlstm_w_v)
    lane_pad = ((0, 0), (0, 0), (0, GATE_LANES - 2 * NGATE))
    wg = jnp.pad(jnp.tile(mlstm_w_gate.reshape(na, 3 * INNER, NGATE), (1, 1, 2)), lane_pad).astype(BF16)
    bg = jnp.pad(jnp.tile(mlstm_b_gate.reshape(na, 1, NGATE), (1, 1, 2)), lane_pad)
    conv_b = mlstm_conv_b.reshape(na, 1, INNER)
    mng = mlstm_norm_g.reshape(na, 1, INNER)
    skip = mlstm_skip.reshape(na, 1, INNER)
    wdown = mlstm_w_down
    pw = pool_w.astype(BF16)
    ps = pool_scale.reshape(-1, 1, D)

    new_c, new_n, new_m = [], [], []
    for l in range(DEPTH):
        x, hm = _ffn(xs if l == 0 else (x,), mods, norm_g, w_in, w_out, fg, layer=l, sub=0, which=0)
        if l % 2 == 0:
            a = l // 2
            q, k, v, xc, z, gates, gates_t = _mlstm_pre(hm, wup, mlstm_conv_w, conv_b, wqkv, wg, bg, a=a)
            h_p, c_new, n_new, m_new = _scan_prompt(q, k, v, gates, gates_t)
            m0 = jnp.broadcast_to(state_m[:, a, :, :, None], (DEC_BATCH, 2, HEADS, GATE_LANES))
            hf_s, hb_s = _scan_sample(q, k, v, gates, gates_t, state_C, state_n, m0, a=a)
            x = _mlstm_out(x, h_p, hf_s, hb_s, xc, z, mods, mng, skip, wdown, layer=l, a=a)
            new_c.append(c_new)
            new_n.append(n_new)
            new_m.append(m_new[:, None, :, :, 0])
            x = (x,)
        else:
            x = _pool(x, hm, mods, pw, ps, layer=l, p=l // 2)
        x = _ffn(x, mods, norm_g, w_in, w_out, fg, layer=l, sub=2, which=1, final=(l == DEPTH - 1))
    y_prompt = x[0].reshape(BATCH, SEQ, D)
    y_sample = x[1].reshape(DEC_BATCH, DEC_SEQ, D)
    return (y_prompt, y_sample, jnp.concatenate(new_c, axis=1), jnp.concatenate(new_n, axis=1),
            jnp.concatenate(new_m, axis=1))
```

```python
import functools
import math

import numpy as np
import jax
import jax.numpy as jnp
from jax import lax
from jax.experimental import pallas as pl
from jax.experimental.pallas import tpu as pltpu

F32 = jnp.float32
BF16 = jnp.bfloat16

D = 1024
BATCH = 16
SEQ = 256
DEPTH = 2
DEC_BATCH = 2
DEC_SEQ = 4096
GRID_W = 64
N_SUB = 3
INNER = 2 * D
HEADS = 4
DH = INNER // HEADS
QKV_BLOCK = 4
CONV_K = 5
POOL_WINDOWS = (2, 4, 8, 16)
POOL_GC = D // len(POOL_WINDOWS)
D_FF = 2816
EPS = 1e-6

NP_TOK = BATCH * SEQ
NS_TOK = DEC_BATCH * DEC_SEQ
NTOK = NP_TOK + NS_TOK
GRP = 4096
N_GROUPS = 1 + DEC_BATCH
assert NP_TOK == GRP and DEC_SEQ == GRP

MXU_TILE = 256
GATE_LANES = 128
CHUNK = 256
F32_ROWS = 8
BF16_ROWS = 16
HALO = BF16_ROWS

NGATE = 2 * 2 * HEADS

FFN_TM = 512
PRE_TM = 2 * CHUNK
OUT_TM = 512
POOL_TM = 512
VMEM_LIMIT = 56 * 1024 * 1024


def _silu(x):
    return x * jax.nn.sigmoid(x)


def _dot(a, b):
    return jnp.dot(a, b, preferred_element_type=F32)


def _dot_nt(a, b):
    return lax.dot_general(a, b, (((1,), (1,)), ((), ())), preferred_element_type=F32)


def _dot_tn(a, b):
    return lax.dot_general(a, b, (((0,), (0,)), ((), ())), preferred_element_type=F32)


def _resident(shape, index_map):
    return pl.BlockSpec(shape, index_map, pipeline_mode=pl.Buffered(1))


def _params(n_axes):
    return pltpu.CompilerParams(
        dimension_semantics=("arbitrary",) * n_axes, vmem_limit_bytes=VMEM_LIMIT)


def _modulate(x, g, mod_ref, sub):
    ms = jnp.mean(x * x, axis=-1, keepdims=True)
    gain = g * (1.0 + mod_ref[3 * sub + 1:3 * sub + 2, :])
    return x * lax.rsqrt(ms + EPS) * gain + mod_ref[3 * sub:3 * sub + 1, :]


def _split_dot(a, x, terms, left=True):
    parts = []
    r = x
    for _ in range(terms):
        p = r.astype(BF16)
        parts.append(p)
        r = r - p.astype(F32)
    acc = None
    for p in reversed(parts):
        t = _dot(a, p) if left else _dot(p, a)
        acc = t if acc is None else acc + t
    return acc


def _adaln_kernel(cond_ref, w_ref, b_ref, o_ref):
    l, j = pl.program_id(0), pl.program_id(1)
    s = _silu(cond_ref[...]).astype(BF16)
    res = _dot(s, w_ref[...].astype(BF16)) + b_ref[pl.ds(l, 1), :]
    for r in range(N_GROUPS):
        o_ref[r, pl.ds(j, 1), :] = res[r:r + 1, :]


def _adaln(cond8, ada_w, ada_b):
    return pl.pallas_call(
        _adaln_kernel,
        grid=(DEPTH, 3 * N_SUB),
        in_specs=[
            pl.BlockSpec((8, D), lambda l, j: (0, 0)),
            pl.BlockSpec((None, D, D), lambda l, j: (l, 0, j)),
            pl.BlockSpec((DEPTH, D), lambda l, j: (0, j)),
        ],
        out_specs=pl.BlockSpec((None, N_GROUPS, 3 * N_SUB, D), lambda l, j: (l, 0, 0, 0)),
        out_shape=jax.ShapeDtypeStruct((DEPTH, N_GROUPS, 3 * N_SUB, D), F32),
        compiler_params=_params(2),
        name="adaln",
    )(cond8, ada_w, ada_b)


def _ffn_kernel(*refs, layer, sub, split_in, emit_h, final):
    n_x = 2 if split_in else 1
    x_refs, (mod_ref, g_ref, win_ref, wout_ref, fg_ref), o_refs = refs[:n_x], refs[n_x:n_x + 5], refs[n_x + 5:]
    is_prompt = pl.program_id(0) < NP_TOK // FFN_TM
    x = jnp.where(is_prompt, x_refs[0][...], x_refs[1][...]) if split_in else x_refs[0][...]
    h = _modulate(x, g_ref[layer, sub:sub + 1, :], mod_ref, sub).astype(BF16)
    acc = jnp.zeros(x.shape, F32)
    for c in range(D_FF // MXU_TILE):
        lo = c * MXU_TILE
        a = _dot(h, win_ref[:, lo:lo + MXU_TILE].astype(BF16))
        b = _dot(h, win_ref[:, D_FF + lo:D_FF + lo + MXU_TILE].astype(BF16))
        u = (_silu(a) * b).astype(BF16)
        acc = acc + _dot(u, wout_ref[lo:lo + MXU_TILE, :].astype(BF16))
    out = x + (0.5 * mod_ref[3 * sub + 2:3 * sub + 3, :]) * acc
    if not final:
        o_refs[0][...] = out
        if emit_h:
            o_refs[1][...] = _modulate(out, g_ref[layer, 1:2, :], mod_ref, 1).astype(BF16)
        return
    ms = jnp.mean(out * out, axis=-1, keepdims=True)
    out = out * lax.rsqrt(ms + EPS) * fg_ref[...]

    @pl.when(is_prompt)
    def _():
        o_refs[0][...] = out

    @pl.when(jnp.logical_not(is_prompt))
    def _():
        o_refs[1][...] = out


def _ffn(xs, mods, norm_g, w_in, w_out, final_g, *, layer, sub, which, final=False):
    tm = FFN_TM
    npt = NP_TOK // tm
    prompt_spec = pl.BlockSpec((tm, D), lambda i: (jnp.minimum(i, npt - 1), 0))
    sample_spec = pl.BlockSpec((tm, D), lambda i: (jnp.maximum(i - npt, 0), 0))
    whole_spec = pl.BlockSpec((tm, D), lambda i: (i, 0))
    split_in = len(xs) == 2
    emit_h = sub == 0
    if final:
        out_specs = [prompt_spec, sample_spec]
        out_shape = [jax.ShapeDtypeStruct((NP_TOK, D), F32), jax.ShapeDtypeStruct((NS_TOK, D), F32)]
    elif emit_h:
        out_specs = [whole_spec, whole_spec]
        out_shape = [jax.ShapeDtypeStruct((NTOK, D), F32), jax.ShapeDtypeStruct((NTOK, D), BF16)]
    else:
        out_specs = whole_spec
        out_shape = jax.ShapeDtypeStruct((NTOK, D), F32)
    return pl.pallas_call(
        functools.partial(_ffn_kernel, layer=layer, sub=sub, split_in=split_in, emit_h=emit_h,
                          final=final),
        grid=(NTOK // tm,),
        in_specs=([prompt_spec, sample_spec] if split_in else [whole_spec]) + [
            pl.BlockSpec((None, None, 3 * N_SUB, D), lambda i: (layer, (i * tm) // GRP, 0, 0)),
            _resident((DEPTH, N_SUB, D), lambda i: (0, 0, 0)),
            _resident((None, None, D, 2 * D_FF), lambda i: (layer, which, 0, 0)),
            _resident((None, None, D_FF, D), lambda i: (layer, which, 0, 0)),
            _resident((1, D), lambda i: (0, 0)),
        ],
        out_specs=out_specs,
        out_shape=out_shape,
        compiler_params=_params(1),
        name=f"ffn_l{layer}_{which}",
    )(*xs, mods, norm_g, w_in, w_out, final_g)


def _mlstm_pre_kernel(hp_ref, hc_ref, hn_ref, wup_ref, cw_ref, cb_ref, wqkv_ref, wg_ref,
                      bg_ref, q_ref, k_ref, v_ref, xco_ref, z_ref, gate_ref, gate_t_ref, xm_s, acc_s):
    tm = PRE_TM
    nrow = tm + 2 * HALO
    i = pl.program_id(0)
    he = jnp.concatenate([hp_ref[...], hc_ref[...], hn_ref[...]], axis=0)
    xm_s[...] = _dot(he, wup_ref[:, :INNER].astype(BF16))
    z_ref[...] = _dot(hc_ref[...], wup_ref[:, INNER:].astype(BF16)).astype(BF16)

    is_prompt = i < NP_TOK // tm
    seqlen = jnp.where(is_prompt, SEQ, DEC_SEQ)
    starts_seq = ((i * tm) & (seqlen - 1)) == 0
    ends_seq = ((i * tm + tm) & (seqlen - 1)) == 0
    xm_s[0:HALO, :] = jnp.where(starts_seq, 0.0, xm_s[0:HALO, :])
    xm_s[HALO + tm:, :] = jnp.where(ends_seq, 0.0, xm_s[HALO + tm:, :])
    xm_all = xm_s[...]
    taps = []
    for j in range(CONV_K):
        off = j - CONV_K // 2
        taps.append(xm_all if off == 0 else pltpu.roll(xm_all, (nrow - off) % nrow, axis=0))

    def conv_rows(lo, hi, at_seq_end):
        acc = cb_ref[...]
        for j in range(CONV_K):
            off = j - CONV_K // 2
            term = taps[j][HALO + lo:HALO + hi]
            if at_seq_end and off != 0:
                pos = (i * tm + lo + lax.broadcasted_iota(jnp.int32, (hi - lo, 1), 0)) & (seqlen - 1)
                inside = (pos + off >= 0) if off < 0 else (pos + off < seqlen)
                term = jnp.where(inside, term, 0.0)
            acc = acc + term * cw_ref[j]
        return acc

    edge = F32_ROWS
    lo = 0
    for b in range(SEQ, tm, SEQ):
        acc_s[lo:b - edge, :] = conv_rows(lo, b - edge, False)
        acc_s[b - edge:b + edge, :] = conv_rows(b - edge, b + edge, True)
        lo = b + edge
    acc_s[lo:tm, :] = conv_rows(lo, tm, False)
    xc = _silu(acc_s[...])
    xco_ref[...] = xc.astype(BF16)

    for t in range(INNER // MXU_TILE):
        cols = slice(t * MXU_TILE, (t + 1) * MXU_TILE)
        xc_t = xc[:, cols].astype(BF16)
        q_ref[:, cols] = _dot(xc_t, wqkv_ref[0, t]).astype(BF16)
        k_ref[:, cols] = (_dot(xc_t, wqkv_ref[1, t]) / math.sqrt(DH)).astype(BF16)
        xm = xm_s[pl.ds(HALO, tm), cols].astype(BF16)
        v_ref[:, cols] = _dot(xm, wqkv_ref[2, t]).astype(BF16)

    g = (_dot(q_ref[...], wg_ref[0:INNER, :]) + _dot(k_ref[...], wg_ref[INNER:2 * INNER, :])
         + _dot(v_ref[...], wg_ref[2 * INNER:, :]) + bg_ref[...])
    lane = lax.broadcasted_iota(jnp.int32, g.shape, 1)
    logsig = jnp.minimum(g, 0.0) - jnp.log1p(jnp.exp(-jnp.abs(g)))
    gl = jnp.where(lane < 2 * NGATE, jnp.where((lane & HEADS) != 0, logsig, g), 0.0)
    tri_prefix = jnp.where(_tri(True), 1.0, 0.0).astype(BF16)
    lane_c = lax.broadcasted_iota(jnp.int32, (CHUNK, GATE_LANES), 1)
    for c in range(tm // CHUNK):
        rows = slice(c * CHUNK, (c + 1) * CHUNK)
        gl_c = gl[rows]
        prefix = _split_dot(tri_prefix, gl_c, 3)
        suffix = prefix[CHUNK - 1:CHUNK, :] - prefix + gl_c
        cum = jnp.where((lane_c & (2 * HEADS)) == 0, prefix, suffix)
        gates = jnp.where(lane_c < NGATE, gl_c, cum)
        gate_ref[rows, :] = gates
        gate_t_ref[:, rows] = jnp.transpose(gates)[0:2 * NGATE, :]


def _mlstm_pre(hm, wup, conv_w, conv_b, wqkv, wg, bg, *, a):
    tm = PRE_TM
    nhalo = NTOK // HALO
    act = jax.ShapeDtypeStruct((NTOK, INNER), BF16)
    act_spec = pl.BlockSpec((tm, INNER), lambda i: (i, 0))
    return pl.pallas_call(
        _mlstm_pre_kernel,
        grid=(NTOK // tm,),
        in_specs=[
            pl.BlockSpec((HALO, D), lambda i: (jnp.maximum(i * (tm // HALO) - 1, 0), 0)),
            pl.BlockSpec((tm, D), lambda i: (i, 0)),
            pl.BlockSpec((HALO, D), lambda i: (jnp.minimum((i + 1) * (tm // HALO), nhalo - 1), 0)),
            _resident((None, D, 2 * INNER), lambda i: (a, 0, 0)),
            _resident((None, CONV_K, 1, INNER), lambda i: (a, 0, 0, 0)),
            _resident((None, 1, INNER), lambda i: (a, 0, 0)),
            _resident((None, 3, INNER // MXU_TILE, MXU_TILE, MXU_TILE), lambda i: (a, 0, 0, 0, 0)),
            _resident((None, 3 * INNER, GATE_LANES), lambda i: (a, 0, 0)),
            _resident((None, 1, GATE_LANES), lambda i: (a, 0, 0)),
        ],
        out_specs=[act_spec] * 5 + [pl.BlockSpec((tm, GATE_LANES), lambda i: (i, 0)),
                                    pl.BlockSpec((2 * NGATE, tm), lambda i: (0, i))],
        out_shape=[act] * 5 + [jax.ShapeDtypeStruct((NTOK, GATE_LANES), F32),
                               jax.ShapeDtypeStruct((2 * NGATE, NTOK), F32)],
        scratch_shapes=[pltpu.VMEM((tm + 2 * HALO, INNER), F32), pltpu.VMEM((tm, INNER), F32)],
        compiler_params=_params(1),
        name="mlstm_pre",
    )(hm, hm, hm, wup, conv_w, conv_b, wqkv, wg, bg)


def _tri(lower):
    r = lax.broadcasted_iota(jnp.int32, (CHUNK, CHUNK), 0)
    c = lax.broadcasted_iota(jnp.int32, (CHUNK, CHUNK), 1)
    return (c <= r) if lower else (c >= r)


def _pick_col(x, col):
    lane = lax.broadcasted_iota(jnp.int32, x.shape, 1)
    return jnp.sum(jnp.where(lane == col, x, 0.0), axis=1, keepdims=True)


def _pick_row(x, row):
    if isinstance(row, int):
        return x[row:row + 1, :]
    sub = lax.broadcasted_iota(jnp.int32, x.shape, 0)
    return jnp.sum(jnp.where(sub == row, x, 0.0), axis=0, keepdims=True)


def _unit_gates(gc, gt, head, lower, m0):
    d = 0 if lower else 1
    col_i = d * 2 * HEADS + head
    col_f = NGATE + col_i + HEADS
    bc = _pick_col(gc, col_f)
    ic = _pick_col(gc, col_i)
    br = _pick_row(gt, col_f)
    ir = _pick_row(gt, col_i)
    log_d = jnp.where(_tri(lower), bc - br + ir, -jnp.inf)
    inter = bc + m0
    mt = jnp.maximum(inter, jnp.max(log_d, axis=-1, keepdims=True))
    decay = jnp.exp(log_d - mt)
    ws = jnp.exp(inter - mt)
    b_last = bc[CHUNK - 1:CHUNK, :] if lower else bc[0:1, :]
    g = b_last - bc + ic
    m_new = jnp.maximum(b_last + m0, jnp.max(g, axis=0, keepdims=True))
    w = jnp.exp(g - m_new)
    w_row = jnp.exp(b_last - br + ir - m_new)
    dec = jnp.exp(b_last + m0 - m_new)
    return decay, mt, ws, w, m_new, dec, w_row


def _unit_apply(q, k, v, qk, gates, state):
    decay, mt, ws, w, _, dec, w_row = gates
    s = qk * decay
    num = _dot(s.astype(BF16), v)
    den = jnp.sum(s, axis=-1, keepdims=True)
    if state is not None:
        c0, n0 = state
        num = num + ws * _dot_nt(q, c0.astype(BF16))
        qn = _dot_nt(q, jnp.broadcast_to(n0, (GATE_LANES, DH)).astype(BF16))
        den = den + ws * qn[:, 0:1]
    h = num / jnp.maximum(jnp.abs(den), jnp.exp(-mt))
    c_new = _dot_tn((v.astype(F32) * w).astype(BF16), k)
    n_new = _dot(jnp.broadcast_to(w_row, (BF16_ROWS, CHUNK)).astype(BF16), k)[0:1, :]
    if state is not None:
        c_new = dec * c0 + c_new
        n_new = dec * n0 + n_new
    return h, c_new, n_new


def _scan_prompt_kernel(q_ref, k_ref, v_ref, gc_ref, gt_ref, h_ref, c_ref, n_ref, m_ref):
    gc, gt = gc_ref[...], gt_ref[...]
    m0 = jnp.zeros((1, 1), F32)
    gates = [[_unit_gates(gc, gt, j, d == 0, m0) for d in range(2)] for j in range(HEADS)]
    for j in range(HEADS):
        cols = slice(j * DH, (j + 1) * DH)
        q, k, v = q_ref[:, cols], k_ref[:, cols], v_ref[:, cols]
        qk = _dot_nt(q, k)
        h_sum = None
        for d in range(2):
            h, c_new, n_new = _unit_apply(q, k, v, qk, gates[j][d], None)
            h_sum = h if h_sum is None else h_sum + h
            c_ref[d, j] = c_new
            n_ref[d, j:j + 1, :] = n_new
            m_ref[d, j:j + 1, :] = jnp.broadcast_to(gates[j][d][4], (1, GATE_LANES))
        h_ref[:, cols] = h_sum.astype(BF16)


def _scan_prompt(q, k, v, gates, gates_t):
    act_spec = pl.BlockSpec((CHUNK, INNER), lambda b: (b, 0))
    return pl.pallas_call(
        _scan_prompt_kernel,
        grid=(BATCH,),
        in_specs=[act_spec, act_spec, act_spec,
                  pl.BlockSpec((CHUNK, GATE_LANES), lambda b: (b, 0)),
                  pl.BlockSpec((2 * NGATE, CHUNK), lambda b: (0, b))],
        out_specs=[act_spec,
                   pl.BlockSpec((None, None, 2, HEADS, DH, DH), lambda b: (b, 0, 0, 0, 0, 0)),
                   pl.BlockSpec((None, None, 2, HEADS, DH), lambda b: (b, 0, 0, 0, 0)),
                   pl.BlockSpec((None, 2, HEADS, GATE_LANES), lambda b: (b, 0, 0, 0))],
        out_shape=[jax.ShapeDtypeStruct((NP_TOK, INNER), BF16),
                   jax.ShapeDtypeStruct((BATCH, 1, 2, HEADS, DH, DH), F32),
                   jax.ShapeDtypeStruct((BATCH, 1, 2, HEADS, DH), F32),
                   jax.ShapeDtypeStruct((BATCH, 2, HEADS, GATE_LANES), F32)],
        compiler_params=_params(1),
        name="scan_prompt",
    )(q, k, v, gates, gates_t)


def _scan_sample_kernel(qf_ref, kf_ref, vf_ref, gcf_ref, gtf_ref, qb_ref, kb_ref, vb_ref, gcb_ref,
                        gtb_ref, c0_ref, n0_ref, m0_ref, hf_ref, hb_ref, c_s, n_s, m_s):
    @pl.when(pl.program_id(1) == 0)
    def _():
        c_s[...] = c0_ref[...]
        n_s[...] = n0_ref[...]
        for d in range(2):
            for j in range(HEADS):
                m_s[d, j:j + 1, :] = jnp.broadcast_to(m0_ref[d:d + 1, j:j + 1], (1, GATE_LANES))

    units = ((qf_ref, kf_ref, vf_ref, gcf_ref, gtf_ref, hf_ref), (qb_ref, kb_ref, vb_ref, gcb_ref, gtb_ref, hb_ref))
    gates = [[_unit_gates(u[3][...], u[4][...], j, d == 0, m_s[d, j:j + 1, 0:1]) for d, u in enumerate(units)]
             for j in range(HEADS)]
    for j in range(HEADS):
        cols = slice(j * DH, (j + 1) * DH)
        for d, (q_ref, k_ref, v_ref, _, _, h_ref) in enumerate(units):
            q, k, v = q_ref[:, cols], k_ref[:, cols], v_ref[:, cols]
            qk = _dot_nt(q, k)
            state = (c_s[d, j], n_s[d, j:j + 1, :])
            h, c_new, n_new = _unit_apply(q, k, v, qk, gates[j][d], state)
            h_ref[:, cols] = h.astype(BF16)
            c_s[d, j] = c_new
            n_s[d, j:j + 1, :] = n_new
            m_s[d, j:j + 1, :] = jnp.broadcast_to(gates[j][d][4], (1, GATE_LANES))


def _scan_sample(q, k, v, gates, gates_t, c0, n0, m0, *, a):
    nc = DEC_SEQ // CHUNK
    base = NP_TOK // CHUNK

    def fwd(b, c):
        return b * nc + c

    def bwd(b, c):
        return b * nc + (nc - 1 - c)

    def specs(blk):
        act = pl.BlockSpec((CHUNK, INNER), lambda b, c: (base + blk(b, c), 0))
        return [act, act, act,
                pl.BlockSpec((CHUNK, GATE_LANES), lambda b, c: (base + blk(b, c), 0)),
                pl.BlockSpec((2 * NGATE, CHUNK), lambda b, c: (0, base + blk(b, c)))]

    return pl.pallas_call(
        _scan_sample_kernel,
        grid=(DEC_BATCH, nc),
        in_specs=specs(fwd) + specs(bwd) + [
            pl.BlockSpec((None, None, 2, HEADS, DH, DH), lambda b, c: (b, a, 0, 0, 0, 0)),
            pl.BlockSpec((None, None, 2, HEADS, DH), lambda b, c: (b, a, 0, 0, 0)),
            pl.BlockSpec((None, None, 2, HEADS), lambda b, c: (b, a, 0, 0))],
        out_specs=[pl.BlockSpec((CHUNK, INNER), lambda b, c: (fwd(b, c), 0)),
                   pl.BlockSpec((CHUNK, INNER), lambda b, c: (bwd(b, c), 0))],
        out_shape=[jax.ShapeDtypeStruct((NS_TOK, INNER), BF16)] * 2,
        scratch_shapes=[pltpu.VMEM((2, HEADS, DH, DH), F32), pltpu.VMEM((2, HEADS, DH), F32),
                        pltpu.VMEM((2, HEADS, GATE_LANES), F32)],
        compiler_params=_params(2),
        name="scan_sample",
    )(q, k, v, gates, gates_t, q, k, v, gates, gates_t, c0, n0, m0)


def _mlstm_out_kernel(x_ref, hp_ref, hfs_ref, hbs_ref, xc_ref, z_ref, mod_ref, ng_ref, skip_ref, wd_ref,
                      o_ref):
    def tile(h_sum):
        y = None
        for h in range(HEADS):
            cols = slice(h * DH, (h + 1) * DH)
            seg = h_sum(cols)
            mu = jnp.mean(seg, axis=-1, keepdims=True)
            cen = seg - mu
            var = jnp.mean(cen * cen, axis=-1, keepdims=True)
            hn = cen * lax.rsqrt(var + EPS) * ng_ref[:, cols]
            out = (hn + skip_ref[:, cols] * xc_ref[:, cols].astype(F32)) * _silu(z_ref[:, cols].astype(F32))
            part = _dot(out.astype(BF16), wd_ref[cols, :].astype(BF16))
            y = part if y is None else y + part
        o_ref[...] = x_ref[...] + mod_ref[5:6, :] * y

    is_prompt = pl.program_id(0) < NP_TOK // OUT_TM

    @pl.when(is_prompt)
    def _():
        tile(lambda cols: hp_ref[:, cols].astype(F32))

    @pl.when(jnp.logical_not(is_prompt))
    def _():
        tile(lambda cols: hfs_ref[:, cols].astype(F32) + hbs_ref[:, cols].astype(F32))


def _mlstm_out(x, h_p, hf_s, hb_s, xc, z, mods, norm_g, skip, w_down, *, layer, a):
    tm = OUT_TM
    npt = NP_TOK // tm
    act_spec = pl.BlockSpec((tm, INNER), lambda i: (i, 0))
    prompt_spec = pl.BlockSpec((tm, INNER), lambda i: (jnp.minimum(i, npt - 1), 0))
    sample_spec = pl.BlockSpec((tm, INNER), lambda i: (jnp.maximum(i - npt, 0), 0))
    return pl.pallas_call(
        _mlstm_out_kernel,
        grid=(NTOK // tm,),
        in_specs=[
            pl.BlockSpec((tm, D), lambda i: (i, 0)),
            prompt_spec, sample_spec, sample_spec, act_spec, act_spec,
            pl.BlockSpec((None, None, 3 * N_SUB, D), lambda i: (layer, (i * tm) // GRP, 0, 0)),
            _resident((None, 1, INNER), lambda i: (a, 0, 0)),
            _resident((None, 1, INNER), lambda i: (a, 0, 0)),
            _resident((None, INNER, D), lambda i: (a, 0, 0)),
        ],
        out_specs=pl.BlockSpec((tm, D), lambda i: (i, 0)),
        out_shape=jax.ShapeDtypeStruct((NTOK, D), F32),
        compiler_params=_params(1),
        name="mlstm_out",
    )(x, h_p, hf_s, hb_s, xc, z, mods, norm_g, skip, w_down)


def _window_count(idx, win, n):
    lo = np.clip(idx - win // 2, 0, n)
    hi = np.clip(idx - win // 2 + win, 0, n)
    return (hi - lo).astype(np.float32)


def _pool_finish(x_ref, o_ref, h_g, p, g, mod_ref, wp_ref, ps_ref, rows):
    cols = slice(g * POOL_GC, (g + 1) * POOL_GC)
    y = _dot((p - h_g).astype(BF16), wp_ref[g]) * ps_ref[:, cols]
    o_ref[rows, cols] = x_ref[rows, cols] + mod_ref[5:6, cols] * y


def _pool_prompt_kernel(x_ref, h_ref, mod_ref, a_ref, cnt_ref, wp_ref, ps_ref, o_ref):
    for g in range(len(POOL_WINDOWS)):
        for s in range(POOL_TM // SEQ):
            rows = slice(s * SEQ, (s + 1) * SEQ)
            h_g = h_ref[rows, g * POOL_GC:(g + 1) * POOL_GC]
            p = _dot(a_ref[g], h_g) / cnt_ref[g]
            _pool_finish(x_ref, o_ref, h_g.astype(F32), p, g, mod_ref, wp_ref, ps_ref, rows)


def _pool_sample_kernel(xc_ref, hp_ref, hc_ref, hn_ref, mod_ref, a_ref, crow_ref, ccol_ref, wp_ref,
                        ps_ref, o_ref, hext):
    tm = POOL_TM
    tiles_per_seq = DEC_SEQ // tm
    j = pl.program_id(0) % tiles_per_seq
    hext[0:tm, :] = jnp.where(j > 0, hp_ref[...].astype(F32), 0.0)
    hext[tm:2 * tm, :] = hc_ref[...].astype(F32)
    hext[2 * tm:3 * tm, :] = jnp.where(j < tiles_per_seq - 1, hn_ref[...].astype(F32), 0.0)
    all_rows = slice(0, tm)
    for g, win in enumerate(POOL_WINDOWS):
        cols = slice(g * POOL_GC, (g + 1) * POOL_GC)
        acc = None
        for dr in range(-(win // 2), win - win // 2):
            term = hext[pl.ds(tm + GRID_W * dr, tm), cols]
            acc = term if acc is None else acc + term
        p_rows = acc / crow_ref[g]
        p = _dot(a_ref[g], p_rows.astype(BF16)) / ccol_ref[g]
        _pool_finish(xc_ref, o_ref, hext[tm:2 * tm, cols], p, g, mod_ref, wp_ref, ps_ref, all_rows)


def _band(n, win):
    t = np.arange(n)
    lo = np.clip(t - win // 2, 0, n)
    hi = np.clip(t - win // 2 + win, 0, n)
    u = np.arange(n)
    return ((u[None, :] >= lo[:, None]) & (u[None, :] < hi[:, None])).astype(np.float32)


def _pool(x, hm, mods, pool_w, pool_scale, *, layer, p):
    tm = POOL_TM
    a1 = jnp.asarray(np.stack([_band(SEQ, w) for w in POOL_WINDOWS]), BF16)
    a2 = jnp.asarray(np.stack([np.kron(np.eye(tm // GRID_W, dtype=np.float32), _band(GRID_W, w))
                               for w in POOL_WINDOWS]), BF16)
    ngroups = len(POOL_WINDOWS)
    base = NP_TOK // tm
    per_seq = DEC_SEQ // tm
    last = NTOK // tm - 1
    tok = np.arange(DEC_SEQ)
    cnt1 = jnp.asarray(np.stack([_window_count(np.arange(SEQ), w, SEQ) for w in POOL_WINDOWS])[..., None])
    crow = np.stack([_window_count(tok // GRID_W, w, DEC_SEQ // GRID_W) for w in POOL_WINDOWS])
    crow = jnp.asarray(crow.reshape(ngroups, per_seq, tm, 1).transpose(1, 0, 2, 3))
    ccol = jnp.asarray(np.stack([_window_count(tok[:tm] % GRID_W, w, GRID_W) for w in POOL_WINDOWS])[..., None])

    wp_spec = _resident((None, ngroups, POOL_GC, POOL_GC), lambda i: (p, 0, 0, 0))
    ps_spec = _resident((None, 1, D), lambda i: (p, 0, 0))
    tile = pl.BlockSpec((tm, D), lambda i: (i, 0))
    out_p = pl.pallas_call(
        _pool_prompt_kernel,
        grid=(NP_TOK // tm,),
        in_specs=[tile, tile,
                  pl.BlockSpec((None, None, 3 * N_SUB, D), lambda i: (layer, 0, 0, 0)),
                  _resident((ngroups, SEQ, SEQ), lambda i: (0, 0, 0)),
                  _resident((ngroups, SEQ, 1), lambda i: (0, 0, 0)), wp_spec, ps_spec],
        out_specs=tile,
        out_shape=jax.ShapeDtypeStruct((NP_TOK, D), F32),
        compiler_params=_params(1),
        name="pool_prompt",
    )(x, hm, mods, a1, cnt1, pool_w, pool_scale)

    out_s = pl.pallas_call(
        _pool_sample_kernel,
        grid=(NS_TOK // tm,),
        in_specs=[pl.BlockSpec((tm, D), lambda i: (base + i, 0)),
                  pl.BlockSpec((tm, D), lambda i: (jnp.maximum(base + i - 1, base), 0)),
                  pl.BlockSpec((tm, D), lambda i: (base + i, 0)),
                  pl.BlockSpec((tm, D), lambda i: (jnp.minimum(base + i + 1, last), 0)),
                  pl.BlockSpec((None, None, 3 * N_SUB, D), lambda i: (layer, 1 + i // per_seq, 0, 0)),
                  _resident((ngroups, tm, tm), lambda i: (0, 0, 0)),
                  pl.BlockSpec((None, ngroups, tm, 1), lambda i: (i % per_seq, 0, 0, 0)),
                  _resident((ngroups, tm, 1), lambda i: (0, 0, 0)), wp_spec, ps_spec],
        out_specs=tile,
        out_shape=jax.ShapeDtypeStruct((NS_TOK, D), F32),
        scratch_shapes=[pltpu.VMEM((3 * tm, D), F32)],
        compiler_params=_params(1),
        name="pool_sample",
    )(x, hm, hm, hm, mods, a2, crow, ccol, pool_w, pool_scale)
    return out_p, out_s


def _block_diag_tiles(w_q, w_k, w_v):
    w = jnp.stack([w_q, w_k, w_v], axis=1)
    na = w.shape[0]
    ntile = INNER // MXU_TILE
    rows = jnp.swapaxes(w, -1, -2).reshape(na, 3, ntile, MXU_TILE, QKV_BLOCK)
    rows = jnp.tile(rows, (1, 1, 1, 1, MXU_TILE // QKV_BLOCK))
    blk = np.arange(MXU_TILE) // QKV_BLOCK
    mask = jnp.asarray(blk[:, None] == blk[None, :])
    return jnp.where(mask, rows, 0.0).astype(BF16)


def kernel(x_prompt, x_sample, state_C, state_n, state_m, c, c_ctx, ada_w, ada_b, norm_g, ffn_w_in,
           ffn_w_out, mlstm_w_up, mlstm_conv_w, mlstm_conv_b, mlstm_w_q, mlstm_w_k, mlstm_w_v,
           mlstm_w_gate, mlstm_b_gate, mlstm_norm_g, mlstm_skip, mlstm_w_down, pool_w, pool_scale,
           final_g):
    na = mlstm_w_up.shape[0]
    xs = (x_prompt.reshape(NP_TOK, D), x_sample.reshape(NS_TOK, D))
    cond8 = jnp.concatenate([c_ctx[None, :], c, jnp.zeros((8 - 1 - DEC_BATCH, D), F32)], axis=0)
    mods = _adaln(cond8, ada_w, ada_b)

    w_in = ffn_w_in
    w_out = ffn_w_out
    fg = final_g.reshape(1, D)
    wup = mlstm_w_up
    wqkv = _block_diag_tiles(mlstm_w_q, mlstm_w_k, mlstm_w_v)
    lane_pad = ((0, 0), (0, 0), (0, GATE_LANES - 2 * NGATE))
    wg = jnp.pad(jnp.tile(mlstm_w_gate.reshape(na, 3 * INNER, NGATE), (1, 1, 2)), lane_pad).astype(BF16)
    bg = jnp.pad(jnp.tile(mlstm_b_gate.reshape(na, 1, NGATE), (1, 1, 2)), lane_pad)
    conv_b = mlstm_conv_b.reshape(na, 1, INNER)
    mng = mlstm_norm_g.reshape(na, 1, INNER)
    skip = mlstm_skip.reshape(na, 1, INNER)
    wdown = mlstm_w_down
    pw = pool_w.astype(BF16)
    ps = pool_scale.reshape(-1, 1, D)

    new_c, new_n, new_m = [], [], []
    for l in range(DEPTH):
        x, hm = _ffn(xs if l == 0 else (x,), mods, norm_g, w_in, w_out, fg, layer=l, sub=0, which=0)
        if l % 2 == 0:
            a = l // 2
            q, k, v, xc, z, gates, gates_t = _mlstm_pre(hm, wup, mlstm_conv_w, conv_b, wqkv, wg, bg, a=a)
            h_p, c_new, n_new, m_new = _scan_prompt(q, k, v, gates, gates_t)
            hf_s, hb_s = _scan_sample(q, k, v, gates, gates_t, state_C, state_n, state_m, a=a)
            x = _mlstm_out(x, h_p, hf_s, hb_s, xc, z, mods, mng, skip, wdown, layer=l, a=a)
            new_c.append(c_new)
            new_n.append(n_new)
            new_m.append(m_new[:, None, :, :, 0])
            x = (x,)
        else:
            x = _pool(x, hm, mods, pw, ps, layer=l, p=l // 2)
        x = _ffn(x, mods, norm_g, w_in, w_out, fg, layer=l, sub=2, which=1, final=(l == DEPTH - 1))
    y_prompt = x[0].reshape(BATCH, SEQ, D)
    y_sample = x[1].reshape(DEC_BATCH, DEC_SEQ, D)
    return (y_prompt, y_sample, jnp.concatenate(new_c, axis=1), jnp.concatenate(new_n, axis=1),
            jnp.concatenate(new_m, axis=1))
```

```python
import functools
import math

import numpy as np
import jax
import jax.numpy as jnp
from jax import lax
from jax.experimental import pallas as pl
from jax.experimental.pallas import tpu as pltpu

F32 = jnp.float32
BF16 = jnp.bfloat16

D = 1024
BATCH = 16
SEQ = 256
DEPTH = 2
DEC_BATCH = 2
DEC_SEQ = 4096
GRID_W = 64
N_SUB = 3
INNER = 2 * D
HEADS = 4
DH = INNER // HEADS
QKV_BLOCK = 4
CONV_K = 5
POOL_WINDOWS = (2, 4, 8, 16)
POOL_GC = D // len(POOL_WINDOWS)
D_FF = 2816
EPS = 1e-6

NP_TOK = BATCH * SEQ
NS_TOK = DEC_BATCH * DEC_SEQ
NTOK = NP_TOK + NS_TOK
GRP = 4096
N_GROUPS = 1 + DEC_BATCH
assert NP_TOK == GRP and DEC_SEQ == GRP

MXU_TILE = 256
GATE_LANES = 128
CHUNK = 256
F32_ROWS = 8
BF16_ROWS = 16
HALO = BF16_ROWS

NGATE = 2 * 2 * HEADS

FFN_TM = 512
PRE_TM = 2 * CHUNK
OUT_TM = 512
POOL_TM = 512
VMEM_LIMIT = 56 * 1024 * 1024
FFN_VMEM_LIMIT = 60 * 1024 * 1024


def _silu(x):
    return x * jax.nn.sigmoid(x)


def _dot(a, b):
    return jnp.dot(a, b, preferred_element_type=F32)


def _dot_nt(a, b):
    return lax.dot_general(a, b, (((1,), (1,)), ((), ())), preferred_element_type=F32)


def _dot_tn(a, b):
    return lax.dot_general(a, b, (((0,), (0,)), ((), ())), preferred_element_type=F32)


def _resident(shape, index_map):
    return pl.BlockSpec(shape, index_map, pipeline_mode=pl.Buffered(1))


def _params(n_axes, vmem_limit=VMEM_LIMIT):
    return pltpu.CompilerParams(
        dimension_semantics=("arbitrary",) * n_axes, vmem_limit_bytes=vmem_limit)


def _modulate(x, g, mod_ref, sub):
    ms = jnp.mean(x * x, axis=-1, keepdims=True)
    gain = g * (1.0 + mod_ref[3 * sub + 1:3 * sub + 2, :])
    return x * lax.rsqrt(ms + EPS) * gain + mod_ref[3 * sub:3 * sub + 1, :]


def _split_dot(a, x, terms, left=True):
    parts = []
    r = x
    for _ in range(terms):
        p = r.astype(BF16)
        parts.append(p)
        r = r - p.astype(F32)
    acc = None
    for p in reversed(parts):
        t = _dot(a, p) if left else _dot(p, a)
        acc = t if acc is None else acc + t
    return acc


def _adaln_kernel(cond_ref, w_ref, b_ref, o_ref):
    l, j = pl.program_id(0), pl.program_id(1)
    s = _silu(cond_ref[...]).astype(BF16)
    res = _dot(s, w_ref[...].astype(BF16)) + b_ref[pl.ds(l, 1), :]
    for r in range(N_GROUPS):
        o_ref[r, pl.ds(j, 1), :] = res[r:r + 1, :]


def _adaln(cond8, ada_w, ada_b):
    return pl.pallas_call(
        _adaln_kernel,
        grid=(DEPTH, 3 * N_SUB),
        in_specs=[
            pl.BlockSpec((8, D), lambda l, j: (0, 0)),
            pl.BlockSpec((None, D, D), lambda l, j: (l, 0, j)),
            pl.BlockSpec((DEPTH, D), lambda l, j: (0, j)),
        ],
        out_specs=pl.BlockSpec((None, N_GROUPS, 3 * N_SUB, D), lambda l, j: (l, 0, 0, 0)),
        out_shape=jax.ShapeDtypeStruct((DEPTH, N_GROUPS, 3 * N_SUB, D), F32),
        compiler_params=_params(2),
        name="adaln",
    )(cond8, ada_w, ada_b)


def _ffn_kernel(*refs, layer, sub, n_x, n_y, emit_h, final):
    x_refs, y_refs = refs[:n_x], refs[n_x:n_x + n_y]
    (mod_ref, g_ref, win_ref, wout_ref, fg_ref), o_refs = refs[n_x + n_y:n_x + n_y + 5], refs[n_x + n_y + 5:]
    is_prompt = pl.program_id(0) < NP_TOK // FFN_TM

    def pick(group):
        return jnp.where(is_prompt, group[0][...], group[1][...]) if len(group) == 2 else group[0][...]

    x = pick(x_refs)
    if n_y:
        x = x + mod_ref[5:6, :] * pick(y_refs)
    h = _modulate(x, g_ref[layer, sub:sub + 1, :], mod_ref, sub).astype(BF16)
    acc = jnp.zeros(x.shape, F32)
    for c in range(D_FF // MXU_TILE):
        lo = c * MXU_TILE
        a = _dot(h, win_ref[:, lo:lo + MXU_TILE].astype(BF16))
        b = _dot(h, win_ref[:, D_FF + lo:D_FF + lo + MXU_TILE].astype(BF16))
        u = (_silu(a) * b).astype(BF16)
        acc = acc + _dot(u, wout_ref[lo:lo + MXU_TILE, :].astype(BF16))
    out = x + (0.5 * mod_ref[3 * sub + 2:3 * sub + 3, :]) * acc
    if not final:
        o_refs[0][...] = out
        if emit_h:
            o_refs[1][...] = _modulate(out, g_ref[layer, 1:2, :], mod_ref, 1).astype(BF16)
        return
    ms = jnp.mean(out * out, axis=-1, keepdims=True)
    out = out * lax.rsqrt(ms + EPS) * fg_ref[...]

    @pl.when(is_prompt)
    def _():
        o_refs[0][...] = out

    @pl.when(jnp.logical_not(is_prompt))
    def _():
        o_refs[1][...] = out


def _ffn(xs, mods, norm_g, w_in, w_out, final_g, *, layer, sub, which, ys=(), final=False):
    tm = FFN_TM
    npt = NP_TOK // tm
    prompt_spec = pl.BlockSpec((tm, D), lambda i: (jnp.minimum(i, npt - 1), 0))
    sample_spec = pl.BlockSpec((tm, D), lambda i: (jnp.maximum(i - npt, 0), 0))
    whole_spec = pl.BlockSpec((tm, D), lambda i: (i, 0))

    def specs(group):
        return [prompt_spec, sample_spec] if len(group) == 2 else [whole_spec] * len(group)

    emit_h = sub == 0
    if final:
        out_specs = [prompt_spec, sample_spec]
        out_shape = [jax.ShapeDtypeStruct((NP_TOK, D), F32), jax.ShapeDtypeStruct((NS_TOK, D), F32)]
    elif emit_h:
        out_specs = [whole_spec, whole_spec]
        out_shape = [jax.ShapeDtypeStruct((NTOK, D), F32), jax.ShapeDtypeStruct((NTOK, D), BF16)]
    else:
        out_specs = whole_spec
        out_shape = jax.ShapeDtypeStruct((NTOK, D), F32)
    return pl.pallas_call(
        functools.partial(_ffn_kernel, layer=layer, sub=sub, n_x=len(xs), n_y=len(ys), emit_h=emit_h,
                          final=final),
        grid=(NTOK // tm,),
        in_specs=specs(xs) + specs(ys) + [
            pl.BlockSpec((None, None, 3 * N_SUB, D), lambda i: (layer, (i * tm) // GRP, 0, 0)),
            _resident((DEPTH, N_SUB, D), lambda i: (0, 0, 0)),
            _resident((None, None, D, 2 * D_FF), lambda i: (layer, which, 0, 0)),
            _resident((None, None, D_FF, D), lambda i: (layer, which, 0, 0)),
            _resident((1, D), lambda i: (0, 0)),
        ],
        out_specs=out_specs,
        out_shape=out_shape,
        compiler_params=_params(1, FFN_VMEM_LIMIT),
        name=f"ffn_l{layer}_{which}",
    )(*xs, *ys, mods, norm_g, w_in, w_out, final_g)


def _mlstm_pre_kernel(hp_ref, hc_ref, hn_ref, wup_ref, cw_ref, cb_ref, wqkv_ref, wg_ref,
                      bg_ref, q_ref, k_ref, v_ref, xco_ref, z_ref, gate_ref, gate_t_ref, xm_s, acc_s):
    tm = PRE_TM
    nrow = tm + 2 * HALO
    i = pl.program_id(0)
    he = jnp.concatenate([hp_ref[...], hc_ref[...], hn_ref[...]], axis=0)
    xm_s[...] = _dot(he, wup_ref[:, :INNER].astype(BF16))
    z_ref[...] = _dot(hc_ref[...], wup_ref[:, INNER:].astype(BF16)).astype(BF16)

    is_prompt = i < NP_TOK // tm
    seqlen = jnp.where(is_prompt, SEQ, DEC_SEQ)
    starts_seq = ((i * tm) & (seqlen - 1)) == 0
    ends_seq = ((i * tm + tm) & (seqlen - 1)) == 0
    xm_s[0:HALO, :] = jnp.where(starts_seq, 0.0, xm_s[0:HALO, :])
    xm_s[HALO + tm:, :] = jnp.where(ends_seq, 0.0, xm_s[HALO + tm:, :])
    xm_all = xm_s[...]
    taps = []
    for j in range(CONV_K):
        off = j - CONV_K // 2
        taps.append(xm_all if off == 0 else pltpu.roll(xm_all, (nrow - off) % nrow, axis=0))

    def conv_rows(lo, hi, at_seq_end):
        acc = cb_ref[...]
        for j in range(CONV_K):
            off = j - CONV_K // 2
            term = taps[j][HALO + lo:HALO + hi]
            if at_seq_end and off != 0:
                pos = (i * tm + lo + lax.broadcasted_iota(jnp.int32, (hi - lo, 1), 0)) & (seqlen - 1)
                inside = (pos + off >= 0) if off < 0 else (pos + off < seqlen)
                term = jnp.where(inside, term, 0.0)
            acc = acc + term * cw_ref[j]
        return acc

    edge = F32_ROWS
    lo = 0
    for b in range(SEQ, tm, SEQ):
        acc_s[lo:b - edge, :] = conv_rows(lo, b - edge, False)
        acc_s[b - edge:b + edge, :] = conv_rows(b - edge, b + edge, True)
        lo = b + edge
    acc_s[lo:tm, :] = conv_rows(lo, tm, False)
    xc = _silu(acc_s[...])
    xco_ref[...] = xc.astype(BF16)

    for t in range(INNER // MXU_TILE):
        cols = slice(t * MXU_TILE, (t + 1) * MXU_TILE)
        xc_t = xc[:, cols].astype(BF16)
        q_ref[:, cols] = _dot(xc_t, wqkv_ref[0, t]).astype(BF16)
        k_ref[:, cols] = (_dot(xc_t, wqkv_ref[1, t]) / math.sqrt(DH)).astype(BF16)
        xm = xm_s[pl.ds(HALO, tm), cols].astype(BF16)
        v_ref[:, cols] = _dot(xm, wqkv_ref[2, t]).astype(BF16)

    g = (_dot(q_ref[...], wg_ref[0:INNER, :]) + _dot(k_ref[...], wg_ref[INNER:2 * INNER, :])
         + _dot(v_ref[...], wg_ref[2 * INNER:, :]) + bg_ref[...])
    lane = lax.broadcasted_iota(jnp.int32, g.shape, 1)
    logsig = jnp.minimum(g, 0.0) - jnp.log1p(jnp.exp(-jnp.abs(g)))
    gl = jnp.where(lane < 2 * NGATE, jnp.where((lane & HEADS) != 0, logsig, g), 0.0)
    tri_prefix = jnp.where(_tri(True), 1.0, 0.0).astype(BF16)
    lane_c = lax.broadcasted_iota(jnp.int32, (CHUNK, GATE_LANES), 1)
    for c in range(tm // CHUNK):
        rows = slice(c * CHUNK, (c + 1) * CHUNK)
        gl_c = gl[rows]
        prefix = _split_dot(tri_prefix, gl_c, 3)
        suffix = prefix[CHUNK - 1:CHUNK, :] - prefix + gl_c
        cum = jnp.where((lane_c & (2 * HEADS)) == 0, prefix, suffix)
        gates = jnp.where(lane_c < NGATE, gl_c, cum)
        gate_ref[rows, :] = gates
        gate_t_ref[:, rows] = jnp.transpose(gates)[0:2 * NGATE, :]


def _mlstm_pre(hm, wup, conv_w, conv_b, wqkv, wg, bg, *, a):
    tm = PRE_TM
    nhalo = NTOK // HALO
    act = jax.ShapeDtypeStruct((NTOK, INNER), BF16)
    act_spec = pl.BlockSpec((tm, INNER), lambda i: (i, 0))
    return pl.pallas_call(
        _mlstm_pre_kernel,
        grid=(NTOK // tm,),
        in_specs=[
            pl.BlockSpec((HALO, D), lambda i: (jnp.maximum(i * (tm // HALO) - 1, 0), 0)),
            pl.BlockSpec((tm, D), lambda i: (i, 0)),
            pl.BlockSpec((HALO, D), lambda i: (jnp.minimum((i + 1) * (tm // HALO), nhalo - 1), 0)),
            _resident((None, D, 2 * INNER), lambda i: (a, 0, 0)),
            _resident((None, CONV_K, 1, INNER), lambda i: (a, 0, 0, 0)),
            _resident((None, 1, INNER), lambda i: (a, 0, 0)),
            _resident((None, 3, INNER // MXU_TILE, MXU_TILE, MXU_TILE), lambda i: (a, 0, 0, 0, 0)),
            _resident((None, 3 * INNER, GATE_LANES), lambda i: (a, 0, 0)),
            _resident((None, 1, GATE_LANES), lambda i: (a, 0, 0)),
        ],
        out_specs=[act_spec] * 5 + [pl.BlockSpec((tm, GATE_LANES), lambda i: (i, 0)),
                                    pl.BlockSpec((2 * NGATE, tm), lambda i: (0, i))],
        out_shape=[act] * 5 + [jax.ShapeDtypeStruct((NTOK, GATE_LANES), F32),
                               jax.ShapeDtypeStruct((2 * NGATE, NTOK), F32)],
        scratch_shapes=[pltpu.VMEM((tm + 2 * HALO, INNER), F32), pltpu.VMEM((tm, INNER), F32)],
        compiler_params=_params(1),
        name="mlstm_pre",
    )(hm, hm, hm, wup, conv_w, conv_b, wqkv, wg, bg)


def _tri(lower):
    r = lax.broadcasted_iota(jnp.int32, (CHUNK, CHUNK), 0)
    c = lax.broadcasted_iota(jnp.int32, (CHUNK, CHUNK), 1)
    return (c <= r) if lower else (c >= r)


def _pick_col(x, col):
    lane = lax.broadcasted_iota(jnp.int32, x.shape, 1)
    return jnp.sum(jnp.where(lane == col, x, 0.0), axis=1, keepdims=True)


def _pick_row(x, row):
    if isinstance(row, int):
        return x[row:row + 1, :]
    sub = lax.broadcasted_iota(jnp.int32, x.shape, 0)
    return jnp.sum(jnp.where(sub == row, x, 0.0), axis=0, keepdims=True)


def _unit_gates(gc, gt, head, lower, m0):
    d = 0 if lower else 1
    col_i = d * 2 * HEADS + head
    col_f = NGATE + col_i + HEADS
    bc = _pick_col(gc, col_f)
    ic = _pick_col(gc, col_i)
    br = _pick_row(gt, col_f)
    ir = _pick_row(gt, col_i)
    log_d = jnp.where(_tri(lower), bc - br + ir, -jnp.inf)
    inter = bc + m0
    mt = jnp.maximum(inter, jnp.max(log_d, axis=-1, keepdims=True))
    decay = jnp.exp(log_d - mt)
    ws = jnp.exp(inter - mt)
    b_last = bc[CHUNK - 1:CHUNK, :] if lower else bc[0:1, :]
    g = b_last - bc + ic
    m_new = jnp.maximum(b_last + m0, jnp.max(g, axis=0, keepdims=True))
    w = jnp.exp(g - m_new)
    w_row = jnp.exp(b_last - br + ir - m_new)
    dec = jnp.exp(b_last + m0 - m_new)
    return decay, mt, ws, w, m_new, dec, w_row


def _unit_apply(q, k, v, qk, gates, state):
    decay, mt, ws, w, _, dec, w_row = gates
    s = qk * decay
    num = _dot(s.astype(BF16), v)
    den = jnp.sum(s, axis=-1, keepdims=True)
    if state is not None:
        c0, n0 = state
        num = num + ws * _dot_nt(q, c0.astype(BF16))
        qn = _dot_nt(q, jnp.broadcast_to(n0, (GATE_LANES, DH)).astype(BF16))
        den = den + ws * qn[:, 0:1]
    h = num / jnp.maximum(jnp.abs(den), jnp.exp(-mt))
    c_new = _dot_tn((v.astype(F32) * w).astype(BF16), k)
    n_new = _dot(jnp.broadcast_to(w_row, (BF16_ROWS, CHUNK)).astype(BF16), k)[0:1, :]
    if state is not None:
        c_new = dec * c0 + c_new
        n_new = dec * n0 + n_new
    return h, c_new, n_new


def _scan_prompt_kernel(q_ref, k_ref, v_ref, gc_ref, gt_ref, h_ref, c_ref, n_ref, m_ref):
    gc, gt = gc_ref[...], gt_ref[...]
    m0 = jnp.zeros((1, 1), F32)
    gates = [[_unit_gates(gc, gt, j, d == 0, m0) for d in range(2)] for j in range(HEADS)]
    for j in range(HEADS):
        cols = slice(j * DH, (j + 1) * DH)
        q, k, v = q_ref[:, cols], k_ref[:, cols], v_ref[:, cols]
        qk = _dot_nt(q, k)
        h_sum = None
        for d in range(2):
            h, c_new, n_new = _unit_apply(q, k, v, qk, gates[j][d], None)
            h_sum = h if h_sum is None else h_sum + h
            c_ref[d, j] = c_new
            n_ref[d, j:j + 1, :] = n_new
            m_ref[d, j:j + 1, :] = jnp.broadcast_to(gates[j][d][4], (1, GATE_LANES))
        h_ref[:, cols] = h_sum.astype(BF16)


def _scan_prompt(q, k, v, gates, gates_t):
    act_spec = pl.BlockSpec((CHUNK, INNER), lambda b: (b, 0))
    return pl.pallas_call(
        _scan_prompt_kernel,
        grid=(BATCH,),
        in_specs=[act_spec, act_spec, act_spec,
                  pl.BlockSpec((CHUNK, GATE_LANES), lambda b: (b, 0)),
                  pl.BlockSpec((2 * NGATE, CHUNK), lambda b: (0, b))],
        out_specs=[act_spec,
                   pl.BlockSpec((None, None, 2, HEADS, DH, DH), lambda b: (b, 0, 0, 0, 0, 0)),
                   pl.BlockSpec((None, None, 2, HEADS, DH), lambda b: (b, 0, 0, 0, 0)),
                   pl.BlockSpec((None, 2, HEADS, GATE_LANES), lambda b: (b, 0, 0, 0))],
        out_shape=[jax.ShapeDtypeStruct((NP_TOK, INNER), BF16),
                   jax.ShapeDtypeStruct((BATCH, 1, 2, HEADS, DH, DH), F32),
                   jax.ShapeDtypeStruct((BATCH, 1, 2, HEADS, DH), F32),
                   jax.ShapeDtypeStruct((BATCH, 2, HEADS, GATE_LANES), F32)],
        compiler_params=_params(1),
        name="scan_prompt",
    )(q, k, v, gates, gates_t)


def _scan_sample_kernel(qf_ref, kf_ref, vf_ref, gcf_ref, gtf_ref, qb_ref, kb_ref, vb_ref, gcb_ref,
                        gtb_ref, c0_ref, n0_ref, m0_ref, hf_ref, hb_ref, c_s, n_s, m_s):
    @pl.when(pl.program_id(1) == 0)
    def _():
        c_s[...] = c0_ref[...]
        n_s[...] = n0_ref[...]
        for d in range(2):
            for j in range(HEADS):
                m_s[d, j:j + 1, :] = jnp.broadcast_to(m0_ref[d:d + 1, j:j + 1], (1, GATE_LANES))

    units = ((qf_ref, kf_ref, vf_ref, gcf_ref, gtf_ref, hf_ref), (qb_ref, kb_ref, vb_ref, gcb_ref, gtb_ref, hb_ref))
    gates = [[_unit_gates(u[3][...], u[4][...], j, d == 0, m_s[d, j:j + 1, 0:1]) for d, u in enumerate(units)]
             for j in range(HEADS)]
    for j in range(HEADS):
        cols = slice(j * DH, (j + 1) * DH)
        for d, (q_ref, k_ref, v_ref, _, _, h_ref) in enumerate(units):
            q, k, v = q_ref[:, cols], k_ref[:, cols], v_ref[:, cols]
            qk = _dot_nt(q, k)
            state = (c_s[d, j], n_s[d, j:j + 1, :])
            h, c_new, n_new = _unit_apply(q, k, v, qk, gates[j][d], state)
            h_ref[:, cols] = h.astype(BF16)
            c_s[d, j] = c_new
            n_s[d, j:j + 1, :] = n_new
            m_s[d, j:j + 1, :] = jnp.broadcast_to(gates[j][d][4], (1, GATE_LANES))


def _scan_sample(q, k, v, gates, gates_t, c0, n0, m0, *, a):
    nc = DEC_SEQ // CHUNK
    base = NP_TOK // CHUNK

    def fwd(b, c):
        return b * nc + c

    def bwd(b, c):
        return b * nc + (nc - 1 - c)

    def specs(blk):
        act = pl.BlockSpec((CHUNK, INNER), lambda b, c: (base + blk(b, c), 0))
        return [act, act, act,
                pl.BlockSpec((CHUNK, GATE_LANES), lambda b, c: (base + blk(b, c), 0)),
                pl.BlockSpec((2 * NGATE, CHUNK), lambda b, c: (0, base + blk(b, c)))]

    return pl.pallas_call(
        _scan_sample_kernel,
        grid=(DEC_BATCH, nc),
        in_specs=specs(fwd) + specs(bwd) + [
            pl.BlockSpec((None, None, 2, HEADS, DH, DH), lambda b, c: (b, a, 0, 0, 0, 0)),
            pl.BlockSpec((None, None, 2, HEADS, DH), lambda b, c: (b, a, 0, 0, 0)),
            pl.BlockSpec((None, None, 2, HEADS), lambda b, c: (b, a, 0, 0))],
        out_specs=[pl.BlockSpec((CHUNK, INNER), lambda b, c: (fwd(b, c), 0)),
                   pl.BlockSpec((CHUNK, INNER), lambda b, c: (bwd(b, c), 0))],
        out_shape=[jax.ShapeDtypeStruct((NS_TOK, INNER), BF16)] * 2,
        scratch_shapes=[pltpu.VMEM((2, HEADS, DH, DH), F32), pltpu.VMEM((2, HEADS, DH), F32),
                        pltpu.VMEM((2, HEADS, GATE_LANES), F32)],
        compiler_params=_params(2),
        name="scan_sample",
    )(q, k, v, gates, gates_t, q, k, v, gates, gates_t, c0, n0, m0)


def _mlstm_out_kernel(hp_ref, hfs_ref, hbs_ref, xc_ref, z_ref, ng_ref, skip_ref, wd_ref, o_ref):
    def tile(h_sum):
        y = None
        for h in range(HEADS):
            cols = slice(h * DH, (h + 1) * DH)
            seg = h_sum(cols)
            mu = jnp.mean(seg, axis=-1, keepdims=True)
            cen = seg - mu
            var = jnp.mean(cen * cen, axis=-1, keepdims=True)
            hn = cen * lax.rsqrt(var + EPS) * ng_ref[:, cols]
            out = (hn + skip_ref[:, cols] * xc_ref[:, cols].astype(F32)) * _silu(z_ref[:, cols].astype(F32))
            part = _dot(out.astype(BF16), wd_ref[cols, :].astype(BF16))
            y = part if y is None else y + part
        o_ref[...] = y

    is_prompt = pl.program_id(0) < NP_TOK // OUT_TM

    @pl.when(is_prompt)
    def _():
        tile(lambda cols: hp_ref[:, cols].astype(F32))

    @pl.when(jnp.logical_not(is_prompt))
    def _():
        tile(lambda cols: hfs_ref[:, cols].astype(F32) + hbs_ref[:, cols].astype(F32))


def _mlstm_out(h_p, hf_s, hb_s, xc, z, norm_g, skip, w_down, *, a):
    tm = OUT_TM
    npt = NP_TOK // tm
    act_spec = pl.BlockSpec((tm, INNER), lambda i: (i, 0))
    prompt_spec = pl.BlockSpec((tm, INNER), lambda i: (jnp.minimum(i, npt - 1), 0))
    sample_spec = pl.BlockSpec((tm, INNER), lambda i: (jnp.maximum(i - npt, 0), 0))
    return pl.pallas_call(
        _mlstm_out_kernel,
        grid=(NTOK // tm,),
        in_specs=[
            prompt_spec, sample_spec, sample_spec, act_spec, act_spec,
            _resident((None, 1, INNER), lambda i: (a, 0, 0)),
            _resident((None, 1, INNER), lambda i: (a, 0, 0)),
            _resident((None, INNER, D), lambda i: (a, 0, 0)),
        ],
        out_specs=pl.BlockSpec((tm, D), lambda i: (i, 0)),
        out_shape=jax.ShapeDtypeStruct((NTOK, D), F32),
        compiler_params=_params(1),
        name="mlstm_out",
    )(h_p, hf_s, hb_s, xc, z, norm_g, skip, w_down)


def _window_count(idx, win, n):
    lo = np.clip(idx - win // 2, 0, n)
    hi = np.clip(idx - win // 2 + win, 0, n)
    return (hi - lo).astype(np.float32)


def _pool_finish(o_ref, h_g, p, g, wp_ref, ps_ref, rows):
    cols = slice(g * POOL_GC, (g + 1) * POOL_GC)
    o_ref[rows, cols] = _dot((p - h_g).astype(BF16), wp_ref[g]) * ps_ref[:, cols]


def _pool_prompt_kernel(h_ref, a_ref, cnt_ref, wp_ref, ps_ref, o_ref):
    for g in range(len(POOL_WINDOWS)):
        for s in range(POOL_TM // SEQ):
            rows = slice(s * SEQ, (s + 1) * SEQ)
            h_g = h_ref[rows, g * POOL_GC:(g + 1) * POOL_GC]
            p = _dot(a_ref[g], h_g) / cnt_ref[g]
            _pool_finish(o_ref, h_g.astype(F32), p, g, wp_ref, ps_ref, rows)


def _pool_sample_kernel(hp_ref, hc_ref, hn_ref, a_ref, crow_ref, ccol_ref, wp_ref, ps_ref, o_ref, hext):
    tm = POOL_TM
    tiles_per_seq = DEC_SEQ // tm
    j = pl.program_id(0) % tiles_per_seq
    hext[0:tm, :] = jnp.where(j > 0, hp_ref[...].astype(F32), 0.0)
    hext[tm:2 * tm, :] = hc_ref[...].astype(F32)
    hext[2 * tm:3 * tm, :] = jnp.where(j < tiles_per_seq - 1, hn_ref[...].astype(F32), 0.0)
    all_rows = slice(0, tm)
    for g, win in enumerate(POOL_WINDOWS):
        cols = slice(g * POOL_GC, (g + 1) * POOL_GC)
        acc = None
        for dr in range(-(win // 2), win - win // 2):
            term = hext[pl.ds(tm + GRID_W * dr, tm), cols]
            acc = term if acc is None else acc + term
        p_rows = acc / crow_ref[g]
        p = _dot(a_ref[g], p_rows.astype(BF16)) / ccol_ref[g]
        _pool_finish(o_ref, hext[tm:2 * tm, cols], p, g, wp_ref, ps_ref, all_rows)


def _band(n, win):
    t = np.arange(n)
    lo = np.clip(t - win // 2, 0, n)
    hi = np.clip(t - win // 2 + win, 0, n)
    u = np.arange(n)
    return ((u[None, :] >= lo[:, None]) & (u[None, :] < hi[:, None])).astype(np.float32)


def _pool(hm, pool_w, pool_scale, *, p):
    tm = POOL_TM
    a1 = jnp.asarray(np.stack([_band(SEQ, w) for w in POOL_WINDOWS]), BF16)
    a2 = jnp.asarray(np.stack([np.kron(np.eye(tm // GRID_W, dtype=np.float32), _band(GRID_W, w))
                               for w in POOL_WINDOWS]), BF16)
    ngroups = len(POOL_WINDOWS)
    base = NP_TOK // tm
    per_seq = DEC_SEQ // tm
    last = NTOK // tm - 1
    tok = np.arange(DEC_SEQ)
    cnt1 = jnp.asarray(np.stack([_window_count(np.arange(SEQ), w, SEQ) for w in POOL_WINDOWS])[..., None])
    crow = np.stack([_window_count(tok // GRID_W, w, DEC_SEQ // GRID_W) for w in POOL_WINDOWS])
    crow = jnp.asarray(crow.reshape(ngroups, per_seq, tm, 1).transpose(1, 0, 2, 3))
    ccol = jnp.asarray(np.stack([_window_count(tok[:tm] % GRID_W, w, GRID_W) for w in POOL_WINDOWS])[..., None])

    wp_spec = _resident((None, ngroups, POOL_GC, POOL_GC), lambda i: (p, 0, 0, 0))
    ps_spec = _resident((None, 1, D), lambda i: (p, 0, 0))
    tile = pl.BlockSpec((tm, D), lambda i: (i, 0))
    out_p = pl.pallas_call(
        _pool_prompt_kernel,
        grid=(NP_TOK // tm,),
        in_specs=[tile,
                  _resident((ngroups, SEQ, SEQ), lambda i: (0, 0, 0)),
                  _resident((ngroups, SEQ, 1), lambda i: (0, 0, 0)), wp_spec, ps_spec],
        out_specs=tile,
        out_shape=jax.ShapeDtypeStruct((NP_TOK, D), F32),
        compiler_params=_params(1),
        name="pool_prompt",
    )(hm, a1, cnt1, pool_w, pool_scale)

    out_s = pl.pallas_call(
        _pool_sample_kernel,
        grid=(NS_TOK // tm,),
        in_specs=[pl.BlockSpec((tm, D), lambda i: (jnp.maximum(base + i - 1, base), 0)),
                  pl.BlockSpec((tm, D), lambda i: (base + i, 0)),
                  pl.BlockSpec((tm, D), lambda i: (jnp.minimum(base + i + 1, last), 0)),
                  _resident((ngroups, tm, tm), lambda i: (0, 0, 0)),
                  pl.BlockSpec((None, ngroups, tm, 1), lambda i: (i % per_seq, 0, 0, 0)),
                  _resident((ngroups, tm, 1), lambda i: (0, 0, 0)), wp_spec, ps_spec],
        out_specs=tile,
        out_shape=jax.ShapeDtypeStruct((NS_TOK, D), F32),
        scratch_shapes=[pltpu.VMEM((3 * tm, D), F32)],
        compiler_params=_params(1),
        name="pool_sample",
    )(hm, hm, hm, a2, crow, ccol, pool_w, pool_scale)
    return out_p, out_s


def _block_diag_tiles(w_q, w_k, w_v):
    w = jnp.stack([w_q, w_k, w_v], axis=1)
    na = w.shape[0]
    ntile = INNER // MXU_TILE
    rows = jnp.swapaxes(w, -1, -2).reshape(na, 3, ntile, MXU_TILE, QKV_BLOCK)
    rows = jnp.tile(rows, (1, 1, 1, 1, MXU_TILE // QKV_BLOCK))
    blk = np.arange(MXU_TILE) // QKV_BLOCK
    mask = jnp.asarray(blk[:, None] == blk[None, :])
    return jnp.where(mask, rows, 0.0).astype(BF16)


def kernel(x_prompt, x_sample, state_C, state_n, state_m, c, c_ctx, ada_w, ada_b, norm_g, ffn_w_in,
           ffn_w_out, mlstm_w_up, mlstm_conv_w, mlstm_conv_b, mlstm_w_q, mlstm_w_k, mlstm_w_v,
           mlstm_w_gate, mlstm_b_gate, mlstm_norm_g, mlstm_skip, mlstm_w_down, pool_w, pool_scale,
           final_g):
    na = mlstm_w_up.shape[0]
    xs = (x_prompt.reshape(NP_TOK, D), x_sample.reshape(NS_TOK, D))
    cond8 = jnp.concatenate([c_ctx[None, :], c, jnp.zeros((8 - 1 - DEC_BATCH, D), F32)], axis=0)
    mods = _adaln(cond8, ada_w, ada_b)

    w_in = ffn_w_in
    w_out = ffn_w_out
    fg = final_g.reshape(1, D)
    wup = mlstm_w_up
    wqkv = _block_diag_tiles(mlstm_w_q, mlstm_w_k, mlstm_w_v)
    lane_pad = ((0, 0), (0, 0), (0, GATE_LANES - 2 * NGATE))
    wg = jnp.pad(jnp.tile(mlstm_w_gate.reshape(na, 3 * INNER, NGATE), (1, 1, 2)), lane_pad).astype(BF16)
    bg = jnp.pad(jnp.tile(mlstm_b_gate.reshape(na, 1, NGATE), (1, 1, 2)), lane_pad)
    conv_b = mlstm_conv_b.reshape(na, 1, INNER)
    mng = mlstm_norm_g.reshape(na, 1, INNER)
    skip = mlstm_skip.reshape(na, 1, INNER)
    wdown = mlstm_w_down
    pw = pool_w.astype(BF16)
    ps = pool_scale.reshape(-1, 1, D)

    new_c, new_n, new_m = [], [], []
    for l in range(DEPTH):
        x, hm = _ffn(xs if l == 0 else (x,), mods, norm_g, w_in, w_out, fg, layer=l, sub=0, which=0)
        if l % 2 == 0:
            a = l // 2
            q, k, v, xc, z, gates, gates_t = _mlstm_pre(hm, wup, mlstm_conv_w, conv_b, wqkv, wg, bg, a=a)
            h_p, c_new, n_new, m_new = _scan_prompt(q, k, v, gates, gates_t)
            hf_s, hb_s = _scan_sample(q, k, v, gates, gates_t, state_C, state_n, state_m, a=a)
            ys = (_mlstm_out(h_p, hf_s, hb_s, xc, z, mng, skip, wdown, a=a),)
            new_c.append(c_new)
            new_n.append(n_new)
            new_m.append(m_new[:, None, :, :, 0])
        else:
            ys = _pool(hm, pw, ps, p=l // 2)
        x = _ffn((x,), mods, norm_g, w_in, w_out, fg, layer=l, sub=2, which=1, ys=ys, final=(l == DEPTH - 1))
    y_prompt = x[0].reshape(BATCH, SEQ, D)
    y_sample = x[1].reshape(DEC_BATCH, DEC_SEQ, D)
    return (y_prompt, y_sample, jnp.concatenate(new_c, axis=1), jnp.concatenate(new_n, axis=1),
            jnp.concatenate(new_m, axis=1))
```

```python
import functools
import math

import numpy as np
import jax
import jax.numpy as jnp
from jax import lax
from jax.experimental import pallas as pl
from jax.experimental.pallas import tpu as pltpu

F32 = jnp.float32
BF16 = jnp.bfloat16

D = 1024
BATCH = 16
SEQ = 256
DEPTH = 2
DEC_BATCH = 2
DEC_SEQ = 4096
GRID_W = 64
N_SUB = 3
INNER = 2 * D
HEADS = 4
DH = INNER // HEADS
QKV_BLOCK = 4
CONV_K = 5
POOL_WINDOWS = (2, 4, 8, 16)
POOL_GC = D // len(POOL_WINDOWS)
D_FF = 2816
EPS = 1e-6

NP_TOK = BATCH * SEQ
NS_TOK = DEC_BATCH * DEC_SEQ
NTOK = NP_TOK + NS_TOK
GRP = 4096
N_GROUPS = 1 + DEC_BATCH
assert NP_TOK == GRP and DEC_SEQ == GRP

MXU_TILE = 256
GATE_LANES = 128
CHUNK = 256
F32_ROWS = 8
BF16_ROWS = 16
HALO = BF16_ROWS

NGATE = 2 * 2 * HEADS

FFN_TM = 512
PRE_TM = 2 * CHUNK
OUT_TM = 512
OUT_BUFS = 3
POOL_TM = 512
VMEM_LIMIT = 56 * 1024 * 1024
FFN_VMEM_LIMIT = 60 * 1024 * 1024


def _silu(x):
    return x * jax.nn.sigmoid(x)


def _dot(a, b):
    return jnp.dot(a, b, preferred_element_type=F32)


def _dot_nt(a, b):
    return lax.dot_general(a, b, (((1,), (1,)), ((), ())), preferred_element_type=F32)


def _dot_tn(a, b):
    return lax.dot_general(a, b, (((0,), (0,)), ((), ())), preferred_element_type=F32)


def _resident(shape, index_map):
    return pl.BlockSpec(shape, index_map, pipeline_mode=pl.Buffered(1))


def _params(n_axes, vmem_limit=VMEM_LIMIT):
    return pltpu.CompilerParams(
        dimension_semantics=("arbitrary",) * n_axes, vmem_limit_bytes=vmem_limit)


def _modulate(x, g, mod_ref, sub):
    ms = jnp.mean(x * x, axis=-1, keepdims=True)
    gain = g * (1.0 + mod_ref[3 * sub + 1:3 * sub + 2, :])
    return x * lax.rsqrt(ms + EPS) * gain + mod_ref[3 * sub:3 * sub + 1, :]


def _split_dot(a, x, terms, left=True):
    parts = []
    r = x
    for _ in range(terms):
        p = r.astype(BF16)
        parts.append(p)
        r = r - p.astype(F32)
    acc = None
    for p in reversed(parts):
        t = _dot(a, p) if left else _dot(p, a)
        acc = t if acc is None else acc + t
    return acc


def _adaln_kernel(cond_ref, w_ref, b_ref, o_ref):
    l, j = pl.program_id(0), pl.program_id(1)
    s = _silu(cond_ref[...]).astype(BF16)
    res = _dot(s, w_ref[...].astype(BF16)) + b_ref[pl.ds(l, 1), :]
    for r in range(N_GROUPS):
        o_ref[r, pl.ds(j, 1), :] = res[r:r + 1, :]


def _adaln(cond8, ada_w, ada_b):
    return pl.pallas_call(
        _adaln_kernel,
        grid=(DEPTH, 3 * N_SUB),
        in_specs=[
            pl.BlockSpec((8, D), lambda l, j: (0, 0)),
            pl.BlockSpec((None, D, D), lambda l, j: (l, 0, j)),
            pl.BlockSpec((DEPTH, D), lambda l, j: (0, j)),
        ],
        out_specs=pl.BlockSpec((None, N_GROUPS, 3 * N_SUB, D), lambda l, j: (l, 0, 0, 0)),
        out_shape=jax.ShapeDtypeStruct((DEPTH, N_GROUPS, 3 * N_SUB, D), F32),
        compiler_params=_params(2),
        name="adaln",
    )(cond8, ada_w, ada_b)


def _ffn_kernel(*refs, layer, sub, n_x, n_y, emit_h, final):
    x_refs, y_refs = refs[:n_x], refs[n_x:n_x + n_y]
    (mod_ref, g_ref, win_ref, wout_ref, fg_ref), o_refs = refs[n_x + n_y:n_x + n_y + 5], refs[n_x + n_y + 5:]
    is_prompt = pl.program_id(0) < NP_TOK // FFN_TM

    def pick(group):
        return jnp.where(is_prompt, group[0][...], group[1][...]) if len(group) == 2 else group[0][...]

    x = pick(x_refs)
    if n_y:
        x = x + mod_ref[5:6, :] * pick(y_refs)
    h = _modulate(x, g_ref[layer, sub:sub + 1, :], mod_ref, sub).astype(BF16)
    acc = jnp.zeros(x.shape, F32)
    for c in range(D_FF // MXU_TILE):
        lo = c * MXU_TILE
        a = _dot(h, win_ref[:, lo:lo + MXU_TILE].astype(BF16))
        b = _dot(h, win_ref[:, D_FF + lo:D_FF + lo + MXU_TILE].astype(BF16))
        u = (_silu(a) * b).astype(BF16)
        acc = acc + _dot(u, wout_ref[lo:lo + MXU_TILE, :].astype(BF16))
    out = x + (0.5 * mod_ref[3 * sub + 2:3 * sub + 3, :]) * acc
    if not final:
        o_refs[0][...] = out
        if emit_h:
            o_refs[1][...] = _modulate(out, g_ref[layer, 1:2, :], mod_ref, 1).astype(BF16)
        return
    ms = jnp.mean(out * out, axis=-1, keepdims=True)
    out = out * lax.rsqrt(ms + EPS) * fg_ref[...]

    @pl.when(is_prompt)
    def _():
        o_refs[0][...] = out

    @pl.when(jnp.logical_not(is_prompt))
    def _():
        o_refs[1][...] = out


def _ffn(xs, mods, norm_g, w_in, w_out, final_g, *, layer, sub, which, ys=(), final=False):
    tm = FFN_TM
    npt = NP_TOK // tm
    prompt_spec = pl.BlockSpec((tm, D), lambda i: (jnp.minimum(i, npt - 1), 0))
    sample_spec = pl.BlockSpec((tm, D), lambda i: (jnp.maximum(i - npt, 0), 0))
    whole_spec = pl.BlockSpec((tm, D), lambda i: (i, 0))

    def specs(group):
        return [prompt_spec, sample_spec] if len(group) == 2 else [whole_spec] * len(group)

    emit_h = sub == 0
    if final:
        out_specs = [prompt_spec, sample_spec]
        out_shape = [jax.ShapeDtypeStruct((NP_TOK, D), F32), jax.ShapeDtypeStruct((NS_TOK, D), F32)]
    elif emit_h:
        out_specs = [whole_spec, whole_spec]
        out_shape = [jax.ShapeDtypeStruct((NTOK, D), F32), jax.ShapeDtypeStruct((NTOK, D), BF16)]
    else:
        out_specs = whole_spec
        out_shape = jax.ShapeDtypeStruct((NTOK, D), F32)
    return pl.pallas_call(
        functools.partial(_ffn_kernel, layer=layer, sub=sub, n_x=len(xs), n_y=len(ys), emit_h=emit_h,
                          final=final),
        grid=(NTOK // tm,),
        in_specs=specs(xs) + specs(ys) + [
            pl.BlockSpec((None, None, 3 * N_SUB, D), lambda i: (layer, (i * tm) // GRP, 0, 0)),
            _resident((DEPTH, N_SUB, D), lambda i: (0, 0, 0)),
            _resident((None, None, D, 2 * D_FF), lambda i: (layer, which, 0, 0)),
            _resident((None, None, D_FF, D), lambda i: (layer, which, 0, 0)),
            _resident((1, D), lambda i: (0, 0)),
        ],
        out_specs=out_specs,
        out_shape=out_shape,
        compiler_params=_params(1, FFN_VMEM_LIMIT),
        name=f"ffn_l{layer}_{which}",
    )(*xs, *ys, mods, norm_g, w_in, w_out, final_g)


def _mlstm_pre_kernel(hp_ref, hc_ref, hn_ref, wup_ref, cw_ref, cb_ref, wqkv_ref, wg_ref,
                      bg_ref, q_ref, k_ref, v_ref, xco_ref, z_ref, gate_ref, gate_t_ref, xm_s, acc_s):
    tm = PRE_TM
    nrow = tm + 2 * HALO
    i = pl.program_id(0)
    he = jnp.concatenate([hp_ref[...], hc_ref[...], hn_ref[...]], axis=0)
    xm_s[...] = _dot(he, wup_ref[:, :INNER].astype(BF16))
    z_ref[...] = _dot(hc_ref[...], wup_ref[:, INNER:].astype(BF16)).astype(BF16)

    is_prompt = i < NP_TOK // tm
    seqlen = jnp.where(is_prompt, SEQ, DEC_SEQ)
    starts_seq = ((i * tm) & (seqlen - 1)) == 0
    ends_seq = ((i * tm + tm) & (seqlen - 1)) == 0
    xm_s[0:HALO, :] = jnp.where(starts_seq, 0.0, xm_s[0:HALO, :])
    xm_s[HALO + tm:, :] = jnp.where(ends_seq, 0.0, xm_s[HALO + tm:, :])
    xm_all = xm_s[...]
    taps = []
    for j in range(CONV_K):
        off = j - CONV_K // 2
        taps.append(xm_all if off == 0 else pltpu.roll(xm_all, (nrow - off) % nrow, axis=0))

    def conv_rows(lo, hi, at_seq_end):
        acc = cb_ref[...]
        for j in range(CONV_K):
            off = j - CONV_K // 2
            term = taps[j][HALO + lo:HALO + hi]
            if at_seq_end and off != 0:
                pos = (i * tm + lo + lax.broadcasted_iota(jnp.int32, (hi - lo, 1), 0)) & (seqlen - 1)
                inside = (pos + off >= 0) if off < 0 else (pos + off < seqlen)
                term = jnp.where(inside, term, 0.0)
            acc = acc + term * cw_ref[j]
        return acc

    edge = F32_ROWS
    lo = 0
    for b in range(SEQ, tm, SEQ):
        acc_s[lo:b - edge, :] = conv_rows(lo, b - edge, False)
        acc_s[b - edge:b + edge, :] = conv_rows(b - edge, b + edge, True)
        lo = b + edge
    acc_s[lo:tm, :] = conv_rows(lo, tm, False)
    xc = _silu(acc_s[...])
    xco_ref[...] = xc.astype(BF16)

    for t in range(INNER // MXU_TILE):
        cols = slice(t * MXU_TILE, (t + 1) * MXU_TILE)
        xc_t = xc[:, cols].astype(BF16)
        q_ref[:, cols] = _dot(xc_t, wqkv_ref[0, t]).astype(BF16)
        k_ref[:, cols] = (_dot(xc_t, wqkv_ref[1, t]) / math.sqrt(DH)).astype(BF16)
        xm = xm_s[pl.ds(HALO, tm), cols].astype(BF16)
        v_ref[:, cols] = _dot(xm, wqkv_ref[2, t]).astype(BF16)

    g = (_dot(q_ref[...], wg_ref[0:INNER, :]) + _dot(k_ref[...], wg_ref[INNER:2 * INNER, :])
         + _dot(v_ref[...], wg_ref[2 * INNER:, :]) + bg_ref[...])
    lane = lax.broadcasted_iota(jnp.int32, g.shape, 1)
    logsig = jnp.minimum(g, 0.0) - jnp.log1p(jnp.exp(-jnp.abs(g)))
    gl = jnp.where(lane < 2 * NGATE, jnp.where((lane & HEADS) != 0, logsig, g), 0.0)
    tri_prefix = jnp.where(_tri(True), 1.0, 0.0).astype(BF16)
    lane_c = lax.broadcasted_iota(jnp.int32, (CHUNK, GATE_LANES), 1)
    for c in range(tm // CHUNK):
        rows = slice(c * CHUNK, (c + 1) * CHUNK)
        gl_c = gl[rows]
        prefix = _split_dot(tri_prefix, gl_c, 3)
        suffix = prefix[CHUNK - 1:CHUNK, :] - prefix + gl_c
        cum = jnp.where((lane_c & (2 * HEADS)) == 0, prefix, suffix)
        gates = jnp.where(lane_c < NGATE, gl_c, cum)
        gate_ref[rows, :] = gates
        gate_t_ref[:, rows] = jnp.transpose(gates)[0:2 * NGATE, :]


def _mlstm_pre(hm, wup, conv_w, conv_b, wqkv, wg, bg, *, a):
    tm = PRE_TM
    nhalo = NTOK // HALO
    act = jax.ShapeDtypeStruct((NTOK, INNER), BF16)
    act_spec = pl.BlockSpec((tm, INNER), lambda i: (i, 0))
    return pl.pallas_call(
        _mlstm_pre_kernel,
        grid=(NTOK // tm,),
        in_specs=[
            pl.BlockSpec((HALO, D), lambda i: (jnp.maximum(i * (tm // HALO) - 1, 0), 0)),
            pl.BlockSpec((tm, D), lambda i: (i, 0)),
            pl.BlockSpec((HALO, D), lambda i: (jnp.minimum((i + 1) * (tm // HALO), nhalo - 1), 0)),
            _resident((None, D, 2 * INNER), lambda i: (a, 0, 0)),
            _resident((None, CONV_K, 1, INNER), lambda i: (a, 0, 0, 0)),
            _resident((None, 1, INNER), lambda i: (a, 0, 0)),
            _resident((None, 3, INNER // MXU_TILE, MXU_TILE, MXU_TILE), lambda i: (a, 0, 0, 0, 0)),
            _resident((None, 3 * INNER, GATE_LANES), lambda i: (a, 0, 0)),
            _resident((None, 1, GATE_LANES), lambda i: (a, 0, 0)),
        ],
        out_specs=[act_spec] * 5 + [pl.BlockSpec((tm, GATE_LANES), lambda i: (i, 0)),
                                    pl.BlockSpec((2 * NGATE, tm), lambda i: (0, i))],
        out_shape=[act] * 5 + [jax.ShapeDtypeStruct((NTOK, GATE_LANES), F32),
                               jax.ShapeDtypeStruct((2 * NGATE, NTOK), F32)],
        scratch_shapes=[pltpu.VMEM((tm + 2 * HALO, INNER), F32), pltpu.VMEM((tm, INNER), F32)],
        compiler_params=_params(1),
        name="mlstm_pre",
    )(hm, hm, hm, wup, conv_w, conv_b, wqkv, wg, bg)


def _tri(lower):
    r = lax.broadcasted_iota(jnp.int32, (CHUNK, CHUNK), 0)
    c = lax.broadcasted_iota(jnp.int32, (CHUNK, CHUNK), 1)
    return (c <= r) if lower else (c >= r)


def _pick_col(x, col):
    lane = lax.broadcasted_iota(jnp.int32, x.shape, 1)
    return jnp.sum(jnp.where(lane == col, x, 0.0), axis=1, keepdims=True)


def _pick_row(x, row):
    if isinstance(row, int):
        return x[row:row + 1, :]
    sub = lax.broadcasted_iota(jnp.int32, x.shape, 0)
    return jnp.sum(jnp.where(sub == row, x, 0.0), axis=0, keepdims=True)


def _unit_gates(gc, gt, head, lower, m0):
    d = 0 if lower else 1
    col_i = d * 2 * HEADS + head
    col_f = NGATE + col_i + HEADS
    bc = _pick_col(gc, col_f)
    ic = _pick_col(gc, col_i)
    br = _pick_row(gt, col_f)
    ir = _pick_row(gt, col_i)
    log_d = jnp.where(_tri(lower), bc - br + ir, -jnp.inf)
    inter = bc + m0
    mt = jnp.maximum(inter, jnp.max(log_d, axis=-1, keepdims=True))
    decay = jnp.exp(log_d - mt)
    ws = jnp.exp(inter - mt)
    b_last = bc[CHUNK - 1:CHUNK, :] if lower else bc[0:1, :]
    g = b_last - bc + ic
    m_new = jnp.maximum(b_last + m0, jnp.max(g, axis=0, keepdims=True))
    w = jnp.exp(g - m_new)
    w_row = jnp.exp(b_last - br + ir - m_new)
    dec = jnp.exp(b_last + m0 - m_new)
    return decay, mt, ws, w, m_new, dec, w_row


def _unit_apply(q, k, v, qk, gates, state):
    decay, mt, ws, w, _, dec, w_row = gates
    s = qk * decay
    num = _dot(s.astype(BF16), v)
    den = jnp.sum(s, axis=-1, keepdims=True)
    if state is not None:
        c0, n0 = state
        num = num + ws * _dot_nt(q, c0.astype(BF16))
        qn = _dot_nt(q, jnp.broadcast_to(n0, (GATE_LANES, DH)).astype(BF16))
        den = den + ws * qn[:, 0:1]
    h = num / jnp.maximum(jnp.abs(den), jnp.exp(-mt))
    c_new = _dot_tn((v.astype(F32) * w).astype(BF16), k)
    n_new = _dot(jnp.broadcast_to(w_row, (BF16_ROWS, CHUNK)).astype(BF16), k)[0:1, :]
    if state is not None:
        c_new = dec * c0 + c_new
        n_new = dec * n0 + n_new
    return h, c_new, n_new


def _scan_prompt_kernel(q_ref, k_ref, v_ref, gc_ref, gt_ref, h_ref, c_ref, n_ref, m_ref):
    gc, gt = gc_ref[...], gt_ref[...]
    m0 = jnp.zeros((1, 1), F32)
    gates = [[_unit_gates(gc, gt, j, d == 0, m0) for d in range(2)] for j in range(HEADS)]
    for j in range(HEADS):
        cols = slice(j * DH, (j + 1) * DH)
        q, k, v = q_ref[:, cols], k_ref[:, cols], v_ref[:, cols]
        qk = _dot_nt(q, k)
        h_sum = None
        for d in range(2):
            h, c_new, n_new = _unit_apply(q, k, v, qk, gates[j][d], None)
            h_sum = h if h_sum is None else h_sum + h
            c_ref[d, j] = c_new
            n_ref[d, j:j + 1, :] = n_new
            m_ref[d, j:j + 1, :] = jnp.broadcast_to(gates[j][d][4], (1, GATE_LANES))
        h_ref[:, cols] = h_sum.astype(BF16)


def _scan_prompt(q, k, v, gates, gates_t):
    act_spec = pl.BlockSpec((CHUNK, INNER), lambda b: (b, 0))
    return pl.pallas_call(
        _scan_prompt_kernel,
        grid=(BATCH,),
        in_specs=[act_spec, act_spec, act_spec,
                  pl.BlockSpec((CHUNK, GATE_LANES), lambda b: (b, 0)),
                  pl.BlockSpec((2 * NGATE, CHUNK), lambda b: (0, b))],
        out_specs=[act_spec,
                   pl.BlockSpec((None, None, 2, HEADS, DH, DH), lambda b: (b, 0, 0, 0, 0, 0)),
                   pl.BlockSpec((None, None, 2, HEADS, DH), lambda b: (b, 0, 0, 0, 0)),
                   pl.BlockSpec((None, 2, HEADS, GATE_LANES), lambda b: (b, 0, 0, 0))],
        out_shape=[jax.ShapeDtypeStruct((NP_TOK, INNER), BF16),
                   jax.ShapeDtypeStruct((BATCH, 1, 2, HEADS, DH, DH), F32),
                   jax.ShapeDtypeStruct((BATCH, 1, 2, HEADS, DH), F32),
                   jax.ShapeDtypeStruct((BATCH, 2, HEADS, GATE_LANES), F32)],
        compiler_params=_params(1),
        name="scan_prompt",
    )(q, k, v, gates, gates_t)


def _scan_sample_kernel(qf_ref, kf_ref, vf_ref, gcf_ref, gtf_ref, qb_ref, kb_ref, vb_ref, gcb_ref,
                        gtb_ref, c0_ref, n0_ref, m0_ref, hf_ref, hb_ref, c_s, n_s, m_s):
    @pl.when(pl.program_id(1) == 0)
    def _():
        c_s[...] = c0_ref[...]
        n_s[...] = n0_ref[...]
        for d in range(2):
            for j in range(HEADS):
                m_s[d, j:j + 1, :] = jnp.broadcast_to(m0_ref[d:d + 1, j:j + 1], (1, GATE_LANES))

    units = ((qf_ref, kf_ref, vf_ref, gcf_ref, gtf_ref, hf_ref), (qb_ref, kb_ref, vb_ref, gcb_ref, gtb_ref, hb_ref))
    gates = [[_unit_gates(u[3][...], u[4][...], j, d == 0, m_s[d, j:j + 1, 0:1]) for d, u in enumerate(units)]
             for j in range(HEADS)]
    for j in range(HEADS):
        cols = slice(j * DH, (j + 1) * DH)
        for d, (q_ref, k_ref, v_ref, _, _, h_ref) in enumerate(units):
            q, k, v = q_ref[:, cols], k_ref[:, cols], v_ref[:, cols]
            qk = _dot_nt(q, k)
            state = (c_s[d, j], n_s[d, j:j + 1, :])
            h, c_new, n_new = _unit_apply(q, k, v, qk, gates[j][d], state)
            h_ref[:, cols] = h.astype(BF16)
            c_s[d, j] = c_new
            n_s[d, j:j + 1, :] = n_new
            m_s[d, j:j + 1, :] = jnp.broadcast_to(gates[j][d][4], (1, GATE_LANES))


def _scan_sample(q, k, v, gates, gates_t, c0, n0, m0, *, a):
    nc = DEC_SEQ // CHUNK
    base = NP_TOK // CHUNK

    def fwd(b, c):
        return b * nc + c

    def bwd(b, c):
        return b * nc + (nc - 1 - c)

    def specs(blk):
        act = pl.BlockSpec((CHUNK, INNER), lambda b, c: (base + blk(b, c), 0))
        return [act, act, act,
                pl.BlockSpec((CHUNK, GATE_LANES), lambda b, c: (base + blk(b, c), 0)),
                pl.BlockSpec((2 * NGATE, CHUNK), lambda b, c: (0, base + blk(b, c)))]

    return pl.pallas_call(
        _scan_sample_kernel,
        grid=(DEC_BATCH, nc),
        in_specs=specs(fwd) + specs(bwd) + [
            pl.BlockSpec((None, None, 2, HEADS, DH, DH), lambda b, c: (b, a, 0, 0, 0, 0)),
            pl.BlockSpec((None, None, 2, HEADS, DH), lambda b, c: (b, a, 0, 0, 0)),
            pl.BlockSpec((None, None, 2, HEADS), lambda b, c: (b, a, 0, 0))],
        out_specs=[pl.BlockSpec((CHUNK, INNER), lambda b, c: (fwd(b, c), 0)),
                   pl.BlockSpec((CHUNK, INNER), lambda b, c: (bwd(b, c), 0))],
        out_shape=[jax.ShapeDtypeStruct((NS_TOK, INNER), BF16)] * 2,
        scratch_shapes=[pltpu.VMEM((2, HEADS, DH, DH), F32), pltpu.VMEM((2, HEADS, DH), F32),
                        pltpu.VMEM((2, HEADS, GATE_LANES), F32)],
        compiler_params=_params(2),
        name="scan_sample",
    )(q, k, v, gates, gates_t, q, k, v, gates, gates_t, c0, n0, m0)


def _mlstm_out_kernel(hp_ref, hfs_ref, hbs_ref, xc_hbm, z_hbm, ng_ref, skip_ref, wd_ref, o_ref,
                      xc_buf, z_buf, sems):
    i = pl.program_id(0)
    n_tiles = NTOK // OUT_TM

    def copies(t):
        slot = t % OUT_BUFS
        rows = pl.ds(t * OUT_TM, OUT_TM)
        return (pltpu.make_async_copy(xc_hbm.at[rows, :], xc_buf.at[slot], sems.at[0, slot]),
                pltpu.make_async_copy(z_hbm.at[rows, :], z_buf.at[slot], sems.at[1, slot]))

    @pl.when(i == 0)
    def _():
        for t in range(OUT_BUFS - 1):
            for cp in copies(t):
                cp.start()

    @pl.when(i + OUT_BUFS - 1 < n_tiles)
    def _():
        for cp in copies(i + OUT_BUFS - 1):
            cp.start()

    for cp in copies(i):
        cp.wait()
    xc_ref, z_ref = xc_buf.at[i % OUT_BUFS], z_buf.at[i % OUT_BUFS]

    def tile(h_sum):
        y = None
        for h in range(HEADS):
            cols = slice(h * DH, (h + 1) * DH)
            seg = h_sum(cols)
            mu = jnp.mean(seg, axis=-1, keepdims=True)
            cen = seg - mu
            var = jnp.mean(cen * cen, axis=-1, keepdims=True)
            hn = cen * lax.rsqrt(var + EPS) * ng_ref[:, cols]
            out = (hn + skip_ref[:, cols] * xc_ref[:, cols].astype(F32)) * _silu(z_ref[:, cols].astype(F32))
            part = _dot(out.astype(BF16), wd_ref[cols, :].astype(BF16))
            y = part if y is None else y + part
        o_ref[...] = y

    is_prompt = pl.program_id(0) < NP_TOK // OUT_TM

    @pl.when(is_prompt)
    def _():
        tile(lambda cols: hp_ref[:, cols].astype(F32))

    @pl.when(jnp.logical_not(is_prompt))
    def _():
        tile(lambda cols: hfs_ref[:, cols].astype(F32) + hbs_ref[:, cols].astype(F32))


def _mlstm_out(h_p, hf_s, hb_s, xc, z, norm_g, skip, w_down, *, a):
    tm = OUT_TM
    npt = NP_TOK // tm
    hbm_spec = pl.BlockSpec(memory_space=pl.ANY)
    prompt_spec = pl.BlockSpec((tm, INNER), lambda i: (jnp.minimum(i, npt - 1), 0))
    sample_spec = pl.BlockSpec((tm, INNER), lambda i: (jnp.maximum(i - npt, 0), 0))
    return pl.pallas_call(
        _mlstm_out_kernel,
        grid=(NTOK // tm,),
        in_specs=[
            prompt_spec, sample_spec, sample_spec, hbm_spec, hbm_spec,
            _resident((None, 1, INNER), lambda i: (a, 0, 0)),
            _resident((None, 1, INNER), lambda i: (a, 0, 0)),
            _resident((None, INNER, D), lambda i: (a, 0, 0)),
        ],
        out_specs=pl.BlockSpec((tm, D), lambda i: (i, 0)),
        out_shape=jax.ShapeDtypeStruct((NTOK, D), F32),
        scratch_shapes=[pltpu.VMEM((OUT_BUFS, tm, INNER), BF16), pltpu.VMEM((OUT_BUFS, tm, INNER), BF16),
                        pltpu.SemaphoreType.DMA((2, OUT_BUFS))],
        compiler_params=_params(1),
        name="mlstm_out",
    )(h_p, hf_s, hb_s, xc, z, norm_g, skip, w_down)


def _window_count(idx, win, n):
    lo = np.clip(idx - win // 2, 0, n)
    hi = np.clip(idx - win // 2 + win, 0, n)
    return (hi - lo).astype(np.float32)


def _pool_finish(o_ref, h_g, p, g, wp_ref, ps_ref, rows):
    cols = slice(g * POOL_GC, (g + 1) * POOL_GC)
    o_ref[rows, cols] = _dot((p - h_g).astype(BF16), wp_ref[g]) * ps_ref[:, cols]


def _pool_prompt_kernel(h_ref, a_ref, cnt_ref, wp_ref, ps_ref, o_ref):
    for g in range(len(POOL_WINDOWS)):
        for s in range(POOL_TM // SEQ):
            rows = slice(s * SEQ, (s + 1) * SEQ)
            h_g = h_ref[rows, g * POOL_GC:(g + 1) * POOL_GC]
            p = _dot(a_ref[g], h_g) / cnt_ref[g]
            _pool_finish(o_ref, h_g.astype(F32), p, g, wp_ref, ps_ref, rows)


def _pool_sample_kernel(hp_ref, hc_ref, hn_ref, a_ref, crow_ref, ccol_ref, wp_ref, ps_ref, o_ref, hext):
    tm = POOL_TM
    tiles_per_seq = DEC_SEQ // tm
    j = pl.program_id(0) % tiles_per_seq
    hext[0:tm, :] = jnp.where(j > 0, hp_ref[...].astype(F32), 0.0)
    hext[tm:2 * tm, :] = hc_ref[...].astype(F32)
    hext[2 * tm:3 * tm, :] = jnp.where(j < tiles_per_seq - 1, hn_ref[...].astype(F32), 0.0)
    all_rows = slice(0, tm)
    for g, win in enumerate(POOL_WINDOWS):
        cols = slice(g * POOL_GC, (g + 1) * POOL_GC)
        acc = None
        for dr in range(-(win // 2), win - win // 2):
            term = hext[pl.ds(tm + GRID_W * dr, tm), cols]
            acc = term if acc is None else acc + term
        p_rows = acc / crow_ref[g]
        p = _dot(a_ref[g], p_rows.astype(BF16)) / ccol_ref[g]
        _pool_finish(o_ref, hext[tm:2 * tm, cols], p, g, wp_ref, ps_ref, all_rows)


def _band(n, win):
    t = np.arange(n)
    lo = np.clip(t - win // 2, 0, n)
    hi = np.clip(t - win // 2 + win, 0, n)
    u = np.arange(n)
    return ((u[None, :] >= lo[:, None]) & (u[None, :] < hi[:, None])).astype(np.float32)


def _pool(hm, pool_w, pool_scale, *, p):
    tm = POOL_TM
    a1 = jnp.asarray(np.stack([_band(SEQ, w) for w in POOL_WINDOWS]), BF16)
    a2 = jnp.asarray(np.stack([np.kron(np.eye(tm // GRID_W, dtype=np.float32), _band(GRID_W, w))
                               for w in POOL_WINDOWS]), BF16)
    ngroups = len(POOL_WINDOWS)
    base = NP_TOK // tm
    per_seq = DEC_SEQ // tm
    last = NTOK // tm - 1
    tok = np.arange(DEC_SEQ)
    cnt1 = jnp.asarray(np.stack([_window_count(np.arange(SEQ), w, SEQ) for w in POOL_WINDOWS])[..., None])
    crow = np.stack([_window_count(tok // GRID_W, w, DEC_SEQ // GRID_W) for w in POOL_WINDOWS])
    crow = jnp.asarray(crow.reshape(ngroups, per_seq, tm, 1).transpose(1, 0, 2, 3))
    ccol = jnp.asarray(np.stack([_window_count(tok[:tm] % GRID_W, w, GRID_W) for w in POOL_WINDOWS])[..., None])

    wp_spec = _resident((None, ngroups, POOL_GC, POOL_GC), lambda i: (p, 0, 0, 0))
    ps_spec = _resident((None, 1, D), lambda i: (p, 0, 0))
    tile = pl.BlockSpec((tm, D), lambda i: (i, 0))
    out_p = pl.pallas_call(
        _pool_prompt_kernel,
        grid=(NP_TOK // tm,),
        in_specs=[tile,
                  _resident((ngroups, SEQ, SEQ), lambda i: (0, 0, 0)),
                  _resident((ngroups, SEQ, 1), lambda i: (0, 0, 0)), wp_spec, ps_spec],
        out_specs=tile,
        out_shape=jax.ShapeDtypeStruct((NP_TOK, D), F32),
        compiler_params=_params(1),
        name="pool_prompt",
    )(hm, a1, cnt1, pool_w, pool_scale)

    out_s = pl.pallas_call(
        _pool_sample_kernel,
        grid=(NS_TOK // tm,),
        in_specs=[pl.BlockSpec((tm, D), lambda i: (jnp.maximum(base + i - 1, base), 0)),
                  pl.BlockSpec((tm, D), lambda i: (base + i, 0)),
                  pl.BlockSpec((tm, D), lambda i: (jnp.minimum(base + i + 1, last), 0)),
                  _resident((ngroups, tm, tm), lambda i: (0, 0, 0)),
                  pl.BlockSpec((None, ngroups, tm, 1), lambda i: (i % per_seq, 0, 0, 0)),
                  _resident((ngroups, tm, 1), lambda i: (0, 0, 0)), wp_spec, ps_spec],
        out_specs=tile,
        out_shape=jax.ShapeDtypeStruct((NS_TOK, D), F32),
        scratch_shapes=[pltpu.VMEM((3 * tm, D), F32)],
        compiler_params=_params(1),
        name="pool_sample",
    )(hm, hm, hm, a2, crow, ccol, pool_w, pool_scale)
    return out_p, out_s


def _block_diag_tiles(w_q, w_k, w_v):
    w = jnp.stack([w_q, w_k, w_v], axis=1)
    na = w.shape[0]
    ntile = INNER // MXU_TILE
    rows = jnp.swapaxes(w, -1, -2).reshape(na, 3, ntile, MXU_TILE, QKV_BLOCK)
    rows = jnp.tile(rows, (1, 1, 1, 1, MXU_TILE // QKV_BLOCK))
    blk = np.arange(MXU_TILE) // QKV_BLOCK
    mask = jnp.asarray(blk[:, None] == blk[None, :])
    return jnp.where(mask, rows, 0.0).astype(BF16)


def kernel(x_prompt, x_sample, state_C, state_n, state_m, c, c_ctx, ada_w, ada_b, norm_g, ffn_w_in,
           ffn_w_out, mlstm_w_up, mlstm_conv_w, mlstm_conv_b, mlstm_w_q, mlstm_w_k, mlstm_w_v,
           mlstm_w_gate, mlstm_b_gate, mlstm_norm_g, mlstm_skip, mlstm_w_down, pool_w, pool_scale,
           final_g):
    na = mlstm_w_up.shape[0]
    xs = (x_prompt.reshape(NP_TOK, D), x_sample.reshape(NS_TOK, D))
    cond8 = jnp.concatenate([c_ctx[None, :], c, jnp.zeros((8 - 1 - DEC_BATCH, D), F32)], axis=0)
    mods = _adaln(cond8, ada_w, ada_b)

    w_in = ffn_w_in
    w_out = ffn_w_out
    fg = final_g.reshape(1, D)
    wup = mlstm_w_up
    wqkv = _block_diag_tiles(mlstm_w_q, mlstm_w_k, mlstm_w_v)
    lane_pad = ((0, 0), (0, 0), (0, GATE_LANES - 2 * NGATE))
    wg = jnp.pad(jnp.tile(mlstm_w_gate.reshape(na, 3 * INNER, NGATE), (1, 1, 2)), lane_pad).astype(BF16)
    bg = jnp.pad(jnp.tile(mlstm_b_gate.reshape(na, 1, NGATE), (1, 1, 2)), lane_pad)
    conv_b = mlstm_conv_b.reshape(na, 1, INNER)
    mng = mlstm_norm_g.reshape(na, 1, INNER)
    skip = mlstm_skip.reshape(na, 1, INNER)
    wdown = mlstm_w_down
    pw = pool_w.astype(BF16)
    ps = pool_scale.reshape(-1, 1, D)

    new_c, new_n, new_m = [], [], []
    for l in range(DEPTH):
        x, hm = _ffn(xs if l == 0 else (x,), mods, norm_g, w_in, w_out, fg, layer=l, sub=0, which=0)
        if l % 2 == 0:
            a = l // 2
            q, k, v, xc, z, gates, gates_t = _mlstm_pre(hm, wup, mlstm_conv_w, conv_b, wqkv, wg, bg, a=a)
            h_p, c_new, n_new, m_new = _scan_prompt(q, k, v, gates, gates_t)
            hf_s, hb_s = _scan_sample(q, k, v, gates, gates_t, state_C, state_n, state_m, a=a)
            ys = (_mlstm_out(h_p, hf_s, hb_s, xc, z, mng, skip, wdown, a=a),)
            new_c.append(c_new)
            new_n.append(n_new)
            new_m.append(m_new[:, None, :, :, 0])
        else:
            ys = _pool(hm, pw, ps, p=l // 2)
        x = _ffn((x,), mods, norm_g, w_in, w_out, fg, layer=l, sub=2, which=1, ys=ys, final=(l == DEPTH - 1))
    y_prompt = x[0].reshape(BATCH, SEQ, D)
    y_sample = x[1].reshape(DEC_BATCH, DEC_SEQ, D)
    return (y_prompt, y_sample, jnp.concatenate(new_c, axis=1), jnp.concatenate(new_n, axis=1),
            jnp.concatenate(new_m, axis=1))
```
